```python
import math
import jax
import jax.numpy as jnp
from jax import lax
import numpy as np

D_MODEL = 1024
BATCH = 1
SEQ = 16384
DEPTH = 2
DEC_BATCH = 32
DEC_SEQ = 1
PAST_LEN = 16384
PAGE_SIZE = 128

N_BRANCH = 4
MIX_WIDTH = D_MODEL // 4
HEAD_DIM = 64
NSA_HEADS = MIX_WIDTH // HEAD_DIM
NSA_KV_HEADS = 1
CMP_STRIDE = 16
CMP_LEN = 2 * CMP_STRIDE
SLC_BLOCK = 64
N_SELECT = 16
WINDOW = 512
N_BUCKETS = 32
REL_EXACT = 16
REL_MAX_DIST = 128
CONV_CH = MIX_WIDTH
CONV_WIDTH = 31
CONV_HIST = CONV_WIDTH - 1
POOL_CH = MIX_WIDTH
POOL_WINDOWS = (2, 4, 8, 16)
POOL_GROUP = POOL_CH // len(POOL_WINDOWS)
POOL_HIST = max(POOL_WINDOWS) - 1
FOX_HEADS = MIX_WIDTH // HEAD_DIM
D_FF = ((8 * D_MODEL // 3) + 127) // 128 * 128
Q_BLOCK = 128
EPS = 1e-6
NEG_INF = -1e30
BIG = 1e6
SPLIT_SIZES = (
    NSA_HEADS * HEAD_DIM,
    6 * NSA_KV_HEADS * HEAD_DIM,
    3 * NSA_HEADS,
    2 * CONV_CH,
    POOL_CH,
    3 * FOX_HEADS * HEAD_DIM,
    FOX_HEADS,
    N_BRANCH * D_MODEL,
)
N_IN = sum(SPLIT_SIZES)

kernel_name = 'hybrid_nsa_conv_pool_fox_decode_step'


def rmsnorm(x, g):
    xf = x.astype(jnp.float32)
    y = xf * lax.rsqrt(jnp.mean(xf * xf, axis=-1, keepdims=True) + EPS)
    return (y * g.astype(jnp.float32)).astype(x.dtype)


def layernorm(x, g, b):
    xf = x.astype(jnp.float32)
    mu = jnp.mean(xf, axis=-1, keepdims=True)
    var = jnp.mean(jnp.square(xf - mu), axis=-1, keepdims=True)
    return ((xf - mu) * lax.rsqrt(var + EPS) * g.astype(jnp.float32) + b.astype(jnp.float32)).astype(x.dtype)


def swiglu(x, w_gate, w_up, w_down):
    return (jax.nn.silu(x @ w_gate) * (x @ w_up)) @ w_down


def masked_softmax(s, mask):
    p = jax.nn.softmax(jnp.where(mask, s, NEG_INF), axis=-1)
    return jnp.where(mask, p, 0.0)


def t5_bucket(dist):
    n = jnp.maximum(dist, 0)
    log_ratio = jnp.log(jnp.maximum(n, 1).astype(jnp.float32) / REL_EXACT) / math.log(REL_MAX_DIST / REL_EXACT)
    large = REL_EXACT + (log_ratio * (N_BUCKETS - REL_EXACT)).astype(jnp.int32)
    return jnp.where(n < REL_EXACT, n, jnp.minimum(large, N_BUCKETS - 1))


def query_blocks(L):
    qb = Q_BLOCK if L % Q_BLOCK == 0 else L
    return qb, L // qb


def split_cols(z, sizes):
    outs, start = [], 0
    for size in sizes:
        outs.append(z[..., start:start + size])
        start += size
    return outs


def gather_pages(pool, page_table):
    pages = pool[page_table]
    return pages.reshape(page_table.shape[0], -1, *pool.shape[2:])


def nsa_compress(k_raw, pos_emb, w_cmp):
    B, T = k_raw.shape[:2]
    nc = (T - CMP_LEN) // CMP_STRIDE + 1
    r = k_raw[:, :(nc + 1) * CMP_STRIDE].reshape(B, nc + 1, CMP_STRIDE, *k_raw.shape[2:])
    blocks = jnp.concatenate([r[:, :-1], r[:, 1:]], axis=2)
    return jnp.einsum('bnlgd,lde->bnge', blocks + pos_emb[:, None, :], w_cmp)


def to_select_blocks(k_raw):
    B, T, G, dk = k_raw.shape
    ns = -(-T // SLC_BLOCK)
    kp = jnp.pad(k_raw, ((0, 0), (0, ns * SLC_BLOCK - T), (0, 0), (0, 0)))
    return kp.reshape(B, ns, SLC_BLOCK, G, dk).transpose(0, 3, 1, 2, 4)


def nsa_attend(q, q_pos, kc, vc, c_end, overlap, ks_blk, vs_blk, kw, vw, w_pos, gates, rel_table):
    B, Q, H, dk = q.shape
    G = kc.shape[2]
    Hg = H // G
    f32 = jnp.float32
    qg = (q * dk ** -0.5).reshape(B, Q, G, Hg, dk)

    def head_bias(dist):
        return jnp.moveaxis(rel_table[t5_bucket(dist)], -1, 0).reshape(G, Hg, *dist.shape)

    s_c = jnp.einsum('bqghd,bngd->bghqn', qg, kc, preferred_element_type=f32) + head_bias(q_pos[:, None] - c_end[None, :])
    p_c = masked_softmax(s_c, c_end[None, :] <= q_pos[:, None])
    o_c = jnp.einsum('bghqn,bngd->bqghd', p_c.astype(vc.dtype), vc)

    ns = ks_blk.shape[2]
    imp = jnp.einsum('bghqn,nj->bgqj', p_c, overlap)
    j = jnp.arange(ns)
    qblk = (q_pos // SLC_BLOCK)[:, None]
    forced = (j == 0) | (j == qblk) | (j == qblk - 1)
    valid = j * SLC_BLOCK <= q_pos[:, None]
    imp = jnp.where(forced, BIG, jnp.where(valid, imp, -BIG))
    _, idx = lax.top_k(imp, min(N_SELECT, ns))
    gather = jax.vmap(jax.vmap(lambda blk, ix: blk[ix]))
    ks = gather(ks_blk, idx).reshape(B, G, Q, -1, dk)
    vs = gather(vs_blk, idx).reshape(B, G, Q, -1, dk)
    pos_s = (idx[..., None] * SLC_BLOCK + jnp.arange(SLC_BLOCK)).reshape(B, G, Q, -1)
    bucket_s = t5_bucket(q_pos[:, None] - pos_s)
    bias_s = jax.vmap(lambda tab, bk: tab[bk], in_axes=(1, 1), out_axes=1)(rel_table.reshape(N_BUCKETS, G, Hg), bucket_s)
    s_s = jnp.einsum('bqghd,bgqkd->bghqk', qg, ks, preferred_element_type=f32) + jnp.moveaxis(bias_s, -1, 2)
    p_s = masked_softmax(s_s, (pos_s <= q_pos[:, None])[:, :, None])
    o_s = jnp.einsum('bghqk,bgqkd->bqghd', p_s.astype(vs.dtype), vs)

    dist_w = q_pos[:, None] - w_pos[None, :]
    s_w = jnp.einsum('bqghd,bkgd->bghqk', qg, kw, preferred_element_type=f32) + head_bias(dist_w)
    p_w = masked_softmax(s_w, (dist_w >= 0) & (dist_w <= WINDOW) & (w_pos[None, :] >= 0))
    o_w = jnp.einsum('bghqk,bkgd->bqghd', p_w.astype(vw.dtype), vw)

    g = gates.reshape(B, Q, 3, G, Hg, 1)
    o = g[:, :, 0] * o_c + g[:, :, 1] * o_s + g[:, :, 2] * o_w
    return o.reshape(B, Q, H * dk)


def nsa_mixer(q, pos0, kv_full, win_all, gates, cmp_pos, cmp_w, rel_table):
    B, L = q.shape[:2]
    kc = nsa_compress(kv_full[:, :, 0], cmp_pos[0], cmp_w[0])
    vc = nsa_compress(kv_full[:, :, 1], cmp_pos[1], cmp_w[1])
    c_start = jnp.arange(kc.shape[1]) * CMP_STRIDE
    c_end = c_start + CMP_LEN - 1
    ks_blk = to_select_blocks(kv_full[:, :, 2])
    vs_blk = to_select_blocks(kv_full[:, :, 3])
    j0 = jnp.arange(ks_blk.shape[2]) * SLC_BLOCK
    overlap = ((c_start[:, None] < j0[None, :] + SLC_BLOCK) & (c_start[:, None] + CMP_LEN > j0[None, :])).astype(jnp.float32)
    qb, nb = query_blocks(L)

    def block(i):
        q0 = i * qb
        qi = lax.dynamic_slice_in_dim(q, q0, qb, axis=1)
        gi = lax.dynamic_slice_in_dim(gates, q0, qb, axis=1)
        wi = lax.dynamic_slice_in_dim(win_all, q0, WINDOW + qb, axis=1)
        q_pos = pos0 + q0 + jnp.arange(qb)
        w_pos = pos0 - WINDOW + q0 + jnp.arange(WINDOW + qb)
        return nsa_attend(qi, q_pos, kc, vc, c_end, overlap, ks_blk, vs_blk, wi[:, :, 0], wi[:, :, 1], w_pos, gi, rel_table)

    out = lax.map(block, jnp.arange(nb))
    return jnp.moveaxis(out, 0, 1).reshape(B, L, -1)


def fox_mixer(q, pos0, k_all, v_all, c_all):
    B, L, H, dh = q.shape
    k_pos = jnp.arange(k_all.shape[1])
    c_k = jnp.moveaxis(c_all, 1, 2)
    qb, nb = query_blocks(L)

    def block(i):
        q0 = i * qb
        qi = lax.dynamic_slice_in_dim(q, q0, qb, axis=1) * dh ** -0.5
        c_q = lax.dynamic_slice_in_dim(c_k, pos0 + q0, qb, axis=2)
        q_pos = pos0 + q0 + jnp.arange(qb)
        s = jnp.einsum('bqhd,bkhd->bhqk', qi, k_all, preferred_element_type=jnp.float32) + c_q[..., None] - c_k[:, :, None, :]
        p = masked_softmax(s, k_pos[None, :] <= q_pos[:, None])
        return jnp.einsum('bhqk,bkhd->bqhd', p.astype(v_all.dtype), v_all)

    out = lax.map(block, jnp.arange(nb))
    return jnp.moveaxis(out, 0, 1).reshape(B, L, H * dh)


def pool_mixer(u, pos, past_pool, pool_w, pool_scale):
    B, L, C = u.shape
    ext = jnp.concatenate([past_pool, u], axis=1)
    cs = jnp.pad(jnp.cumsum(ext.astype(jnp.float32), axis=1), ((0, 0), (1, 0), (0, 0)))
    h0 = POOL_HIST + 1
    groups = []
    for gi, w in enumerate(POOL_WINDOWS):
        sl = slice(gi * POOL_GROUP, (gi + 1) * POOL_GROUP)
        win_sum = cs[:, h0:, sl] - cs[:, h0 - w:h0 - w + L, sl]
        count = jnp.minimum(w, pos + 1).astype(jnp.float32)[None, :, None]
        groups.append(win_sum / count - u[..., sl].astype(jnp.float32))
    pooled = jnp.stack(groups, axis=2).astype(u.dtype)
    y = jnp.einsum('blgc,gcd->blgd', pooled, pool_w).reshape(B, L, C) * pool_scale
    return y, ext[:, -POOL_HIST:]


def trunk_layer(x, pos0, past_nsa, past_win, past_conv, past_pool, past_fox_kv, past_fox_logf, lw, rel_table, win_keep):
    (norm_g, ffn_gate, ffn_up, ffn_down, w_in, fox_f_bias, cmp_pos, cmp_w,
     conv_w, conv_b, conv_ln, pool_w, pool_scale, w_branch, w_out) = lw
    B, L, _ = x.shape
    pos = pos0 + jnp.arange(L)
    x = x + 0.5 * swiglu(rmsnorm(x, norm_g[0]), ffn_gate[0], ffn_up[0], ffn_down[0])
    h = rmsnorm(x, norm_g[1])
    a_q, a_kv, a_g, b_in, c_in, d_qkv, d_f, merge_g = split_cols(h @ w_in, SPLIT_SIZES)

    kv_new = a_kv.reshape(B, L, 6, NSA_KV_HEADS, HEAD_DIM)
    new_nsa = kv_new[:, :, :4]
    kv_full = jnp.concatenate([past_nsa, new_nsa], axis=1)
    hist_win = jnp.pad(past_win, ((0, 0), (WINDOW - past_win.shape[1], 0), (0, 0), (0, 0), (0, 0)))
    win_all = jnp.concatenate([hist_win, kv_new[:, :, 4:]], axis=1)
    new_win = win_all[:, win_all.shape[1] - win_keep:]
    y_a = nsa_mixer(a_q.reshape(B, L, NSA_HEADS, HEAD_DIM), pos0, kv_full, win_all,
                    jax.nn.sigmoid(a_g).reshape(B, L, 3, NSA_HEADS), cmp_pos, cmp_w, rel_table)

    u = b_in[..., :CONV_CH] * jax.nn.sigmoid(b_in[..., CONV_CH:])
    ext_b = jnp.concatenate([past_conv, u], axis=1)
    yb = lax.conv_general_dilated(ext_b, conv_w[:, None, :], window_strides=(1,), padding='VALID',
                                  dimension_numbers=('NWC', 'WIO', 'NWC'), feature_group_count=CONV_CH) + conv_b
    y_b = jax.nn.silu(layernorm(yb, conv_ln[0], conv_ln[1]))
    new_conv = ext_b[:, -CONV_HIST:]

    y_c, new_pool = pool_mixer(c_in, pos, past_pool, pool_w, pool_scale)

    qkv = d_qkv.reshape(B, L, 3, FOX_HEADS, HEAD_DIM)
    logf = jax.nn.log_sigmoid((d_f + fox_f_bias).astype(jnp.float32))
    new_fox_kv = qkv[:, :, 1:]
    kv_d = jnp.concatenate([past_fox_kv, new_fox_kv], axis=1)
    c_all = jnp.cumsum(jnp.concatenate([past_fox_logf.astype(jnp.float32), logf], axis=1), axis=1)
    y_d = fox_mixer(qkv[:, :, 0], pos0, kv_d[:, :, 0], kv_d[:, :, 1], c_all)

    branches = jnp.stack([y_a, y_b, y_c, y_d], axis=2)
    proj = jnp.einsum('blnc,ncd->blnd', branches, w_branch)
    gate = jax.nn.sigmoid(merge_g.reshape(B, L, N_BRANCH, D_MODEL))
    x = x + jnp.sum(gate * proj, axis=2) @ w_out
    x = x + 0.5 * swiglu(rmsnorm(x, norm_g[2]), ffn_gate[1], ffn_up[1], ffn_down[1])
    return x, new_nsa, new_fox_kv, logf.astype(past_fox_logf.dtype), new_win, new_conv, new_pool


def setup_inputs(seed: int = 0) -> dict:
    key = jax.random.key(seed)
    it = iter(jax.random.split(key, 32))

    def nrm(shape, scale=1.0):
        return scale * jax.random.normal(next(it), shape, jnp.float32)

    n_pages = PAST_LEN // PAGE_SIZE
    n_used = DEC_BATCH * n_pages
    n_phys = n_used + (n_used + 3) // 4
    win_keep = min(WINDOW, PAST_LEN)
    page_table = jax.random.permutation(next(it), n_phys)[:n_used].reshape(DEC_BATCH, n_pages).astype(jnp.int32)
    return dict(
        x_prompt=nrm((BATCH, SEQ, D_MODEL)),
        x_sample=nrm((DEC_BATCH, DEC_SEQ, D_MODEL)),
        cache_nsa=nrm((DEPTH, n_phys, PAGE_SIZE, 4, NSA_KV_HEADS, HEAD_DIM)),
        cache_fox_kv=nrm((DEPTH, n_phys, PAGE_SIZE, 2, FOX_HEADS, HEAD_DIM)),
        cache_fox_logf=jax.nn.log_sigmoid(3.0 + nrm((DEPTH, n_phys, PAGE_SIZE, FOX_HEADS), 0.5)),
        state_nsa_win=nrm((DEPTH, DEC_BATCH, win_keep, 2, NSA_KV_HEADS, HEAD_DIM)),
        state_conv=nrm((DEPTH, DEC_BATCH, CONV_HIST, CONV_CH)),
        state_pool=nrm((DEPTH, DEC_BATCH, POOL_HIST, POOL_CH)),
        page_table=page_table,
        norm_g=1.0 + nrm((DEPTH, 3, D_MODEL), 0.02),
        ffn_gate=nrm((DEPTH, 2, D_MODEL, D_FF), D_MODEL ** -0.5),
        ffn_up=nrm((DEPTH, 2, D_MODEL, D_FF), D_MODEL ** -0.5),
        ffn_down=nrm((DEPTH, 2, D_FF, D_MODEL), D_FF ** -0.5),
        w_in=nrm((DEPTH, D_MODEL, N_IN), D_MODEL ** -0.5),
        fox_f_bias=3.0 + nrm((DEPTH, FOX_HEADS), 0.1),
        nsa_cmp_pos=nrm((DEPTH, 2, CMP_LEN, HEAD_DIM), 0.02),
        nsa_cmp_w=nrm((DEPTH, 2, CMP_LEN, HEAD_DIM, HEAD_DIM), (CMP_LEN * HEAD_DIM) ** -0.5),
        rel_bias=nrm((N_BUCKETS, NSA_HEADS), 0.5),
        conv_w=nrm((DEPTH, CONV_WIDTH, CONV_CH), CONV_WIDTH ** -0.5),
        conv_b=nrm((DEPTH, CONV_CH), 0.02),
        conv_ln=jnp.stack([1.0 + nrm((DEPTH, CONV_CH), 0.02), nrm((DEPTH, CONV_CH), 0.02)], axis=1),
        pool_w=nrm((DEPTH, len(POOL_WINDOWS), POOL_GROUP, POOL_GROUP), POOL_GROUP ** -0.5),
        pool_scale=1.0 + nrm((DEPTH, POOL_CH), 0.02),
        w_branch=nrm((DEPTH, N_BRANCH, MIX_WIDTH, D_MODEL), MIX_WIDTH ** -0.5),
        w_out=nrm((DEPTH, D_MODEL, D_MODEL), 0.5 * D_MODEL ** -0.5),
        final_norm_g=1.0 + nrm((D_MODEL,), 0.02),
    )


def reference(x_prompt, x_sample, cache_nsa, cache_fox_kv, cache_fox_logf, state_nsa_win, state_conv, state_pool,
              page_table, norm_g, ffn_gate, ffn_up, ffn_down, w_in, fox_f_bias, nsa_cmp_pos, nsa_cmp_w, rel_bias,
              conv_w, conv_b, conv_ln, pool_w, pool_scale, w_branch, w_out, final_norm_g):
    B = x_prompt.shape[0]
    dt = x_prompt.dtype
    past_len = page_table.shape[1] * cache_nsa.shape[2]
    win_keep = state_nsa_win.shape[2]
    xp, xs = x_prompt, x_sample
    new_p, new_s = [], []
    for l in range(DEPTH):
        lw = (norm_g[l], ffn_gate[l], ffn_up[l], ffn_down[l], w_in[l], fox_f_bias[l], nsa_cmp_pos[l], nsa_cmp_w[l],
              conv_w[l], conv_b[l], conv_ln[l], pool_w[l], pool_scale[l], w_branch[l], w_out[l])
        xp, *st = trunk_layer(
            xp, 0,
            jnp.zeros((B, 0, 4, NSA_KV_HEADS, HEAD_DIM), dt),
            jnp.zeros((B, WINDOW, 2, NSA_KV_HEADS, HEAD_DIM), dt),
            jnp.zeros((B, CONV_HIST, CONV_CH), dt),
            jnp.zeros((B, POOL_HIST, POOL_CH), dt),
            jnp.zeros((B, 0, 2, FOX_HEADS, HEAD_DIM), dt),
            jnp.zeros((B, 0, FOX_HEADS), cache_fox_logf.dtype),
            lw, rel_bias, win_keep)
        new_p.append(st)
        xs, *st = trunk_layer(
            xs, past_len,
            gather_pages(cache_nsa[l], page_table), state_nsa_win[l], state_conv[l], state_pool[l],
            gather_pages(cache_fox_kv[l], page_table), gather_pages(cache_fox_logf[l], page_table),
            lw, rel_bias, win_keep)
        new_s.append(st)
    y_prompt = rmsnorm(xp, final_norm_g)
    y_sample = rmsnorm(xs, final_norm_g)
    new_nsa_prompt = jnp.stack([s[0] for s in new_p])
    new_nsa_sample = jnp.stack([s[0] for s in new_s])
    new_fox_kv_prompt = jnp.stack([s[1] for s in new_p])
    new_fox_kv_sample = jnp.stack([s[1] for s in new_s])
    new_fox_logf_prompt = jnp.stack([s[2] for s in new_p])
    new_fox_logf_sample = jnp.stack([s[2] for s in new_s])
    new_win_prompt = jnp.stack([s[3] for s in new_p])
    new_win_sample = jnp.stack([s[3] for s in new_s])
    new_conv_prompt = jnp.stack([s[4] for s in new_p])
    new_conv_sample = jnp.stack([s[4] for s in new_s])
    new_pool_prompt = jnp.stack([s[5] for s in new_p])
    new_pool_sample = jnp.stack([s[5] for s in new_s])
    return (y_prompt, y_sample, new_nsa_prompt, new_nsa_sample, new_fox_kv_prompt, new_fox_kv_sample,
            new_fox_logf_prompt, new_fox_logf_sample, new_win_prompt, new_win_sample,
            new_conv_prompt, new_conv_sample, new_pool_prompt, new_pool_sample)
```

```python
import functools
import math

import numpy as np
import jax
import jax.numpy as jnp
from jax import lax
from jax.experimental import pallas as pl
from jax.experimental.pallas import tpu as pltpu

F32 = jnp.float32
BF16 = jnp.bfloat16

D_MODEL = 1024
N_BRANCH = 4
MIX = D_MODEL // 4
HD = 64
NH = MIX // HD
CMP_STRIDE = 16
CMP_LEN = 32
SLC_BLOCK = 64
N_SELECT = 16
WINDOW = 512
N_BUCKETS = 32
REL_EXACT = 16
REL_MAX_DIST = 128
CONV_WIDTH = 31
CONV_HIST = CONV_WIDTH - 1
POOL_WINDOWS = (2, 4, 8, 16)
POOL_HIST = 15
D_FF = 2816
EPS = 1e-6
NEG = -1e30
BIG = 1e6
PAGE = 128

LANE = 128
TQ = 128
TK = 512
SUPER = 64 * SLC_BLOCK
TF = 256
TM = 256
FF_CHUNK = 1408
HALO = 32
NDIST = 1024
PG = 8
VMEM_LIMIT = 56 * 1024 * 1024

C_AQ, C_NSA, C_WIN, C_MISC, C_CONV, C_POOL, C_FOX = 0, 256, 512, 640, 768, 1280, 1536
N_SMALL = 2304


def _cparams(sem):
    return pltpu.CompilerParams(dimension_semantics=sem, vmem_limit_bytes=VMEM_LIMIT)


def _full(shape, single=True):
    nd = len(shape)
    kw = dict(pipeline_mode=pl.Buffered(1)) if single else {}
    return pl.BlockSpec(shape, lambda *a, _nd=nd: (0,) * _nd, **kw)


def _rms(x, g):
    ms = jnp.mean(x * x, axis=-1, keepdims=True)
    return x * lax.rsqrt(ms + EPS) * g


def _dot(a, b):
    return jnp.dot(a, b, preferred_element_type=F32)


def _dot_nt(a, b):
    return lax.dot_general(a, b, (((1,), (1,)), ((), ())), preferred_element_type=F32)


def _split3(x):
    hi = x.astype(BF16)
    r1 = x - hi.astype(F32)
    mid = r1.astype(BF16)
    lo = (r1 - mid.astype(F32)).astype(BF16)
    return hi, mid, lo


def _dot01(x, ones_bf16):
    hi, mid, lo = _split3(x)
    return _dot(hi, ones_bf16) + _dot(mid, ones_bf16) + _dot(lo, ones_bf16)


def _dot01_l(ones_bf16, x):
    hi, mid, lo = _split3(x)
    return _dot(ones_bf16, hi) + _dot(ones_bf16, mid) + _dot(ones_bf16, lo)


def _log_sigmoid(x):
    return jnp.minimum(x, 0.0) - jnp.log(1.0 + jnp.exp(-jnp.abs(x)))


def _t5_bucket(dist):
    n = jnp.maximum(dist, 0)
    ratio = jnp.log(jnp.maximum(n, 1).astype(F32) / REL_EXACT) / math.log(REL_MAX_DIST / REL_EXACT)
    large = REL_EXACT + (ratio * (N_BUCKETS - REL_EXACT)).astype(jnp.int32)
    return jnp.where(n < REL_EXACT, n, jnp.minimum(large, N_BUCKETS - 1))


def _t5_bias(relt, dist):
    bucket = _t5_bucket(dist)
    rows = lax.broadcasted_iota(jnp.int32, (N_BUCKETS, dist.shape[1]), 0)
    onehot = jnp.where(rows == bucket, 1.0, 0.0).astype(BF16)
    return _dot01(relt, onehot)


def _ffn_kernel(x_ref, g_ref, wg_ref, wu_ref, wd_ref, fg_ref, o_ref, *, final):
    x = x_ref[...]
    h = _rms(x, g_ref[...]).astype(BF16)
    acc = jnp.zeros_like(x)
    for c in range(D_FF // FF_CHUNK):
        sl = slice(c * FF_CHUNK, (c + 1) * FF_CHUNK)
        a = _dot(h, wg_ref[:, sl])
        b = _dot(h, wu_ref[:, sl])
        t = (a * jax.nn.sigmoid(a) * b).astype(BF16)
        acc = acc + _dot(t, wd_ref[sl, :])
    out = x + 0.5 * acc
    o_ref[...] = _rms(out, fg_ref[...]) if final else out


def _ffn(x, g, wg, wu, wd, fg, final=False):
    m = x.shape[0]
    tm = min(TM, m)
    row = pl.BlockSpec((tm, D_MODEL), lambda i: (i, 0))
    return pl.pallas_call(
        functools.partial(_ffn_kernel, final=final),
        grid=(m // tm,),
        in_specs=[row, _full((1, D_MODEL)), _full((D_MODEL, D_FF)), _full((D_MODEL, D_FF)),
                  _full((D_FF, D_MODEL)), _full((1, D_MODEL))],
        out_specs=row,
        out_shape=jax.ShapeDtypeStruct((m, D_MODEL), F32),
        compiler_params=_cparams(("arbitrary",)),
        name="ffn",
    )(x, g, wg, wu, wd, fg)


def _inproj_kernel(x_ref, g_ref, w_ref, fb_ref,
                   qn_ref, nsa_ref, win_ref, misc_ref, u_ref, cin_ref, fq_ref, fkv_ref,
                   ksel_ref, kwin_ref, vcat_ref, fk_ref, fv_ref):
    tm = x_ref.shape[0]
    h = _rms(x_ref[...], g_ref[...]).astype(BF16)
    z = _dot(h, w_ref[...])
    scale = HD ** -0.5
    for hh in range(NH):
        qn_ref[hh] = (z[:, C_AQ + HD * hh:C_AQ + HD * (hh + 1)] * scale).astype(BF16)
        fq_ref[hh] = (z[:, C_FOX + HD * hh:C_FOX + HD * (hh + 1)] * scale).astype(BF16)
        fk_ref[hh] = z[:, C_FOX + MIX + HD * hh:C_FOX + MIX + HD * (hh + 1)].astype(BF16)
        fv_ref[hh] = z[:, C_FOX + 2 * MIX + HD * hh:C_FOX + 2 * MIX + HD * (hh + 1)].astype(BF16)
    nsa = z[:, C_NSA:C_NSA + 256]
    nsa_ref[...] = nsa
    win = z[:, C_WIN:C_WIN + 128]
    win_ref[...] = win
    zm = z[:, C_MISC:C_MISC + 128]
    lane = lax.broadcasted_iota(jnp.int32, zm.shape, 1)
    misc_ref[...] = jnp.where(lane < 3 * NH, jax.nn.sigmoid(zm), _log_sigmoid(zm + fb_ref[...]))
    u_ref[...] = z[:, C_CONV:C_CONV + MIX] * jax.nn.sigmoid(z[:, C_CONV + MIX:C_CONV + 2 * MIX])
    cin_ref[...] = z[:, C_POOL:C_POOL + MIX]
    fkv_ref[...] = z[:, C_FOX + MIX:C_FOX + 3 * MIX]
    t = pl.program_id(0) * tm + lax.broadcasted_iota(jnp.int32, (tm, HD), 0)
    feat = lax.broadcasted_iota(jnp.int32, (tm, HD), 1)
    onehot = jnp.where(((t // SLC_BLOCK) % 64) == feat, 1.0, 0.0).astype(BF16)
    ksel_ref[...] = jnp.concatenate([nsa[:, 128:192].astype(BF16), onehot], axis=1)
    kwin_ref[...] = jnp.concatenate([win[:, 0:64].astype(BF16), jnp.zeros((tm, HD), BF16)], axis=1)
    vcat_ref[...] = jnp.concatenate([nsa[:, 192:256].astype(BF16), win[:, 64:128].astype(BF16)], axis=1)


def _inproj(x, g, w_small, fbias_row):
    m = x.shape[0]
    tm = min(TM, m)

    def row(n):
        return pl.BlockSpec((tm, n), lambda i: (i, 0))

    heads = pl.BlockSpec((NH, tm, HD), lambda i: (0, i, 0))
    shapes = [((NH, m, HD), BF16, heads), ((m, 256), F32, row(256)), ((m, 128), F32, row(128)),
              ((m, 128), F32, row(128)), ((m, MIX), F32, row(MIX)), ((m, MIX), F32, row(MIX)),
              ((NH, m, HD), BF16, heads), ((m, 2 * MIX), F32, row(2 * MIX)),
              ((m, 128), BF16, row(128)), ((m, 128), BF16, row(128)), ((m, 128), BF16, row(128)),
              ((NH, m, HD), BF16, heads), ((NH, m, HD), BF16, heads)]
    return pl.pallas_call(
        _inproj_kernel,
        grid=(m // tm,),
        in_specs=[row(D_MODEL), _full((1, D_MODEL)), _full((D_MODEL, N_SMALL)), _full((1, 128))],
        out_specs=[s[2] for s in shapes],
        out_shape=[jax.ShapeDtypeStruct(s[0], s[1]) for s in shapes],
        compiler_params=_cparams(("arbitrary",)),
        name="inproj",
    )(x, g, w_small, fbias_row)


def _merge_kernel(x_ref, g_ref, wm_ref, wb_ref, wo_ref, ya_ref, yb_ref, yc_ref, yd_ref, o_ref):
    x = x_ref[...]
    h = _rms(x, g_ref[...]).astype(BF16)
    mix = jnp.zeros_like(x)
    for n in range(N_BRANCH):
        gate = jax.nn.sigmoid(_dot(h, wm_ref[:, n * D_MODEL:(n + 1) * D_MODEL]))
        if n in (0, 3):
            y_ref = ya_ref if n == 0 else yd_ref
            proj = jnp.zeros_like(x)
            for hh in range(NH):
                proj = proj + _dot(y_ref[hh], wb_ref[n, hh * HD:(hh + 1) * HD, :])
        else:
            proj = _dot((yb_ref if n == 1 else yc_ref)[...], wb_ref[n])
        mix = mix + gate * proj
    o_ref[...] = x + _dot(mix.astype(BF16), wo_ref[...])


def _merge(x, g, w_merge, w_branch, w_out, ya, yb, yc, yd):
    m = x.shape[0]
    tm = min(TM, m)
    row = pl.BlockSpec((tm, D_MODEL), lambda i: (i, 0))
    heads = pl.BlockSpec((NH, tm, HD), lambda i: (0, i, 0))
    mixrow = pl.BlockSpec((tm, MIX), lambda i: (i, 0))
    return pl.pallas_call(
        _merge_kernel,
        grid=(m // tm,),
        in_specs=[row, _full((1, D_MODEL)), _full((D_MODEL, N_BRANCH * D_MODEL)),
                  _full((N_BRANCH, MIX, D_MODEL)), _full((D_MODEL, D_MODEL)),
                  heads, mixrow, mixrow, heads],
        out_specs=row,
        out_shape=jax.ShapeDtypeStruct((m, D_MODEL), F32),
        compiler_params=_cparams(("arbitrary",)),
        name="merge",
    )(x, g, w_merge, w_branch, w_out, ya, yb, yc, yd)


def _bias_table_kernel(relt_ref, o_ref):
    dist = lax.broadcasted_iota(jnp.int32, (1, NDIST), 1)
    o_ref[...] = _t5_bias(relt_ref[...], dist)


def _bias_table(relt8):
    return pl.pallas_call(
        _bias_table_kernel,
        out_shape=jax.ShapeDtypeStruct((8, NDIST), F32),
        name="bias_table",
    )(relt8)


def _compress(r, pos_ref, w_ref, t, scr):
    n = r.shape[0]
    a = _dot((r + pos_ref[t, 0]).astype(BF16), w_ref[t, 0])
    b = _dot((r + pos_ref[t, 1]).astype(BF16), w_ref[t, 1])
    scr[pl.ds(0, n), :] = b
    scr[pl.ds(n, 8), :] = jnp.zeros((8, HD), F32)
    return a + scr[pl.ds(1, n), :]


def _compress_kernel(rk_ref, rv_ref, pos_ref, w_ref, kc_ref, vc_ref, scr):
    kc_ref[...] = _compress(rk_ref[...], pos_ref, w_ref, 0, scr).astype(BF16)
    vc_ref[...] = _compress(rv_ref[...], pos_ref, w_ref, 1, scr).astype(BF16)


def _compress_prompt(rk, rv, pos, w):
    r = rk.shape[0]
    return pl.pallas_call(
        _compress_kernel,
        in_specs=[_full((r, 1024), False), _full((r, 1024), False), _full((2, 2, 1, 1024), False),
                  _full((2, 2, 1024, HD), False)],
        out_specs=[_full((r, HD), False)] * 2,
        grid=(1,),
        out_shape=[jax.ShapeDtypeStruct((r, HD), BF16)] * 2,
        scratch_shapes=[pltpu.VMEM((r + 8, HD), F32)],
        compiler_params=_cparams(("arbitrary",)),
        name="nsa_compress",
    )(rk, rv, pos, w)


def _select_topk(val, n_sel):
    ns = val.shape[1]
    j = lax.broadcasted_iota(jnp.int32, val.shape, 1).astype(F32)
    picks = []
    for _ in range(n_sel):
        mx = jnp.max(val, axis=1, keepdims=True)
        first = jnp.min(jnp.where(val == mx, j, float(ns)), axis=1, keepdims=True)
        picks.append(first)
        val = jnp.where(j == first, -jnp.inf, val)
    return picks


def _cmp_attn_kernel(q_ref, kc_ref, vc_ref, ovl_ref, tab_ref, far_ref, misc_ref, oc_ref, selb_ref,
                     *, n_rows, n_blocks):
    i = pl.program_id(0)
    q = q_ref[...].reshape(NH * TQ, HD)
    w0 = 8 * i - 8
    c0 = jnp.clip((w0 // 128) * 128, 0, n_rows - 256)
    c0 = pl.multiple_of(c0, 128)
    s = _dot_nt(q, kc_ref[...])
    n_idx = lax.broadcasted_iota(jnp.int32, s.shape, 1)
    s_far = s + jnp.where(n_idx < c0, far_ref[...], NEG)
    s_win = _dot_nt(q, kc_ref[pl.ds(c0, 256), :]) + tab_ref[0]
    m = jnp.maximum(jnp.max(s_far, axis=1, keepdims=True), jnp.max(s_win, axis=1, keepdims=True))
    p_far = jnp.exp(s_far - m)
    p_win = jnp.exp(s_win - m)
    l = jnp.sum(p_far, axis=1, keepdims=True) + jnp.sum(p_win, axis=1, keepdims=True)
    inv = jnp.where(m > 0.5 * NEG, 1.0 / l, 0.0)
    pb_far = (p_far * inv).astype(BF16)
    pb_win = (p_win * inv).astype(BF16)
    o = _dot(pb_far, vc_ref[...]) + _dot(pb_win, vc_ref[pl.ds(c0, 256), :])
    impf = _dot(pb_far, ovl_ref[...]) + _dot(pb_win, ovl_ref[pl.ds(c0, 256), :])
    imp = impf[0:TQ] + impf[TQ:2 * TQ] + impf[2 * TQ:3 * TQ] + impf[3 * TQ:4 * TQ]
    gates = misc_ref[...]
    for hh in range(NH):
        oc_ref[hh] = o[hh * TQ:(hh + 1) * TQ] * gates[:, hh:hh + 1]
    j = lax.broadcasted_iota(jnp.int32, imp.shape, 1)
    qpos = i * TQ + lax.broadcasted_iota(jnp.int32, imp.shape, 0)
    qblk = qpos // SLC_BLOCK
    forced = (j == 0) | (j == qblk) | (j == qblk - 1)
    val = jnp.where(forced, BIG, jnp.where(j * SLC_BLOCK <= qpos, imp, -BIG))
    jf = j.astype(F32)
    selb = jnp.full(imp.shape, NEG, F32)
    for first in _select_topk(val, min(N_SELECT, n_blocks)):
        selb = jnp.where(jf == first, 0.0, selb)
    selb_ref[...] = selb.astype(BF16)


def _cmp_attn(qh, kc, vc, ovl, tab, far, misc):
    l = qh.shape[1]
    r = kc.shape[0]
    ns = ovl.shape[1]

    def variant(i):
        w0 = 8 * i - 8
        c0 = jnp.clip((w0 // 128) * 128, 0, r - 256)
        return ((w0 - c0 + 8) // 8, 0, 0)

    heads = pl.BlockSpec((NH, TQ, HD), lambda i: (0, i, 0))
    return pl.pallas_call(
        functools.partial(_cmp_attn_kernel, n_rows=r, n_blocks=ns),
        grid=(l // TQ,),
        in_specs=[heads, _full((r, HD)), _full((r, HD)), _full((r, ns)),
                  pl.BlockSpec((1, NH * TQ, 256), variant), _full((NH * TQ, 1)),
                  pl.BlockSpec((TQ, 128), lambda i: (i, 0))],
        out_specs=[heads, pl.BlockSpec((TQ, ns), lambda i: (i, 0))],
        out_shape=[jax.ShapeDtypeStruct((NH, l, HD), F32), jax.ShapeDtypeStruct((l, ns), BF16)],
        compiler_params=_cparams(("arbitrary",)),
        name="nsa_cmp_attn",
    )(qh, kc, vc, ovl, tab, far, misc)


def _flash_step(carry, s, v):
    m, l, acc = carry
    m_new = jnp.maximum(m, jnp.max(s, axis=1, keepdims=True))
    alpha = jnp.exp(m - m_new)
    p = jnp.exp(s - m_new)
    l = alpha * l + jnp.sum(p, axis=1, keepdims=True)
    acc = alpha * acc + _dot(p.astype(BF16), v)
    return m_new, l, acc


def _flash_init(rows, width):
    return (jnp.full((rows, 1), NEG, F32), jnp.zeros((rows, 1), F32), jnp.zeros((rows, width), F32))


def _nsa_attn_kernel(q_ref, selb_ref, ksel_ref, kwin_ref, vcat_ref, tsel_ref, twin_ref, misc_ref, oc_ref,
                     ya_ref, qx_ref, *, n_super):
    i = pl.program_id(0)
    jd = i // (TK // TQ)
    for jj in range(n_super):
        sb = selb_ref[:, jj * 64:(jj + 1) * 64]
        for hh in range(NH):
            qx_ref[jj, hh * TQ:(hh + 1) * TQ, :] = jnp.concatenate([q_ref[hh], sb], axis=1)

    def sel_body(j, carry):
        k0 = pl.multiple_of(j * TK, TK)
        s = _dot_nt(qx_ref[j // (SUPER // TK)], ksel_ref[pl.ds(k0, TK), :])
        s = s + tsel_ref[0, jnp.maximum(j - jd + 2, 0)]
        return _flash_step(carry, s, vcat_ref[pl.ds(k0, TK), :])

    _, l_s, acc_s = lax.fori_loop(0, jd + 1, sel_body, _flash_init(NH * TQ, 128))

    def win_body(j, carry):
        k0 = pl.multiple_of(j * TK, TK)
        s = _dot_nt(qx_ref[0], kwin_ref[pl.ds(k0, TK), :]) + twin_ref[0, j - jd + 1]
        return _flash_step(carry, s, vcat_ref[pl.ds(k0, TK), :])

    _, l_w, acc_w = lax.fori_loop(jnp.maximum(jd - 1, 0), jd + 1, win_body, _flash_init(NH * TQ, 128))

    o_s = acc_s[:, 0:HD] / l_s
    o_w = acc_w[:, HD:2 * HD] / l_w
    gates = misc_ref[...]
    for hh in range(NH):
        rows = slice(hh * TQ, (hh + 1) * TQ)
        y = oc_ref[hh] + gates[:, NH + hh:NH + hh + 1] * o_s[rows] + gates[:, 2 * NH + hh:2 * NH + hh + 1] * o_w[rows]
        ya_ref[hh] = y.astype(BF16)


def _nsa_attn(qh, selb, ksel, kwin, vcat, tsel, twin, misc, oc):
    l = qh.shape[1]
    ns = selb.shape[1]
    n_super = ns // 64
    heads = pl.BlockSpec((NH, TQ, HD), lambda i: (0, i, 0))
    nv = TK // TQ
    return pl.pallas_call(
        functools.partial(_nsa_attn_kernel, n_super=n_super),
        grid=(l // TQ,),
        in_specs=[heads, pl.BlockSpec((TQ, ns), lambda i: (i, 0)),
                  _full((l, 128)), _full((l, 128)), _full((l, 128)),
                  pl.BlockSpec((1, 3, NH * TQ, TK), lambda i: (i % nv, 0, 0, 0)),
                  pl.BlockSpec((1, 2, NH * TQ, TK), lambda i: (i % nv, 0, 0, 0)),
                  pl.BlockSpec((TQ, 128), lambda i: (i, 0)), heads],
        out_specs=heads,
        out_shape=jax.ShapeDtypeStruct((NH, l, HD), BF16),
        scratch_shapes=[pltpu.VMEM((n_super, NH * TQ, 128), BF16)],
        compiler_params=_cparams(("arbitrary",)),
        name="nsa_attn",
    )(qh, selb, ksel, kwin, vcat, tsel, twin, misc, oc)


def _cumsum_kernel(x_ref, o_ref):
    a = x_ref.shape[1]
    upper = jnp.where(lax.broadcasted_iota(jnp.int32, (128, 128), 0) <= lax.broadcasted_iota(jnp.int32, (128, 128), 1),
                      1.0, 0.0).astype(BF16)
    lower = jnp.where(lax.broadcasted_iota(jnp.int32, (a, a), 1) < lax.broadcasted_iota(jnp.int32, (a, a), 0),
                      1.0, 0.0).astype(BF16)
    for hh in range(NH):
        x = x_ref[hh]
        within = _dot01(x, upper)
        before = _dot01_l(lower, jnp.broadcast_to(within[:, 127:128], within.shape))
        o_ref[hh] = -(within + before)


def _neg_cumsum(x):
    a = x.shape[1]
    return pl.pallas_call(
        _cumsum_kernel,
        out_shape=jax.ShapeDtypeStruct((NH, a, 128), F32),
        name="fox_cumsum",
    )(x)


def _fox_attn_kernel(q_ref, k_ref, v_ref, nc_ref, o_ref):
    i = pl.program_id(1)
    q = q_ref[0]

    def tile(j):
        k0 = pl.multiple_of(j * TF, TF)
        s = _dot_nt(q, k_ref[0, pl.ds(k0, TF), :]) + nc_ref[0, j]
        return s, v_ref[0, pl.ds(k0, TF), :]

    def body(j, carry):
        s, v = tile(j)
        return _flash_step(carry, s, v)

    carry = lax.fori_loop(0, i, body, _flash_init(TF, HD))
    s, v = tile(i)
    causal = lax.broadcasted_iota(jnp.int32, s.shape, 1) <= lax.broadcasted_iota(jnp.int32, s.shape, 0)
    _, l, acc = _flash_step(carry, jnp.where(causal, s, NEG), v)
    o_ref[0] = (acc / l).astype(BF16)


def _fox_attn(fq, fk, fv, negc):
    l = fq.shape[1]
    nt = l // TF
    return pl.pallas_call(
        _fox_attn_kernel,
        grid=(NH, nt),
        in_specs=[pl.BlockSpec((1, TF, HD), lambda h, i: (h, i, 0)),
                  pl.BlockSpec((1, l, HD), lambda h, i: (h, 0, 0)),
                  pl.BlockSpec((1, l, HD), lambda h, i: (h, 0, 0)),
                  pl.BlockSpec((1, nt, 1, TF), lambda h, i: (h, 0, 0, 0))],
        out_specs=pl.BlockSpec((1, TF, HD), lambda h, i: (h, i, 0)),
        out_shape=jax.ShapeDtypeStruct((NH, l, HD), BF16),
        compiler_params=_cparams(("arbitrary", "arbitrary")),
        name="fox_attn",
    )(fq, fk, fv, negc)


def _layernorm_silu(y, ln_ref):
    mu = jnp.mean(y, axis=-1, keepdims=True)
    d = y - mu
    var = jnp.mean(d * d, axis=-1, keepdims=True)
    z = d * lax.rsqrt(var + EPS) * ln_ref[0] + ln_ref[1]
    return z * jax.nn.sigmoid(z)


def _pool_select(sums, counts, u):
    lane = lax.broadcasted_iota(jnp.int32, u.shape, 1)
    group = MIX // len(POOL_WINDOWS)
    out = sums[-1] / counts[-1]
    for g in range(len(POOL_WINDOWS) - 2, -1, -1):
        out = jnp.where(lane < (g + 1) * group, sums[g] / counts[g], out)
    return out - u


def _convpool_kernel(u_ref, uh_ref, up_ref, c_ref, ch_ref, cp_ref, cw_ref, cb_ref, ln_ref, pw_ref, ps_ref,
                     yb_ref, yc_ref, ext_ref, *, pos0):
    i = pl.program_id(0)
    tm = u_ref.shape[0]
    ext_ref[pl.ds(0, HALO), :] = jnp.where(i == 0, up_ref[...], uh_ref[...])
    ext_ref[pl.ds(HALO, tm), :] = u_ref[...]
    acc = jnp.zeros((tm, MIX), F32) + cb_ref[...]
    for w in range(CONV_WIDTH):
        acc = acc + ext_ref[pl.ds(HALO - CONV_HIST + w, tm), :] * cw_ref[pl.ds(w, 1), :]
    yb_ref[...] = _layernorm_silu(acc, ln_ref).astype(BF16)
    c = c_ref[...]
    ext_ref[pl.ds(0, HALO), :] = jnp.where(i == 0, cp_ref[...], ch_ref[...])
    ext_ref[pl.ds(HALO, tm), :] = c
    pos = pos0 + i * tm + lax.broadcasted_iota(jnp.int32, (tm, 1), 0)
    run = c
    sums, counts = [], []
    for k in range(1, max(POOL_WINDOWS)):
        run = run + ext_ref[pl.ds(HALO - k, tm), :]
        if k + 1 in POOL_WINDOWS:
            sums.append(run)
            counts.append(jnp.minimum(k + 1, pos + 1).astype(F32))
    pooled = _pool_select(sums, counts, c).astype(BF16)
    yc_ref[...] = (_dot(pooled, pw_ref[...]) * ps_ref[...]).astype(BF16)


def _convpool(u, u_past, c, c_past, cw, cb, ln, pw_bd, ps, pos0):
    l = u.shape[0]
    tm = min(TM, l)
    nh = tm // HALO
    row = pl.BlockSpec((tm, MIX), lambda i: (i, 0))
    halo = pl.BlockSpec((HALO, MIX), lambda i: (jnp.maximum(i * nh - 1, 0), 0))
    return pl.pallas_call(
        functools.partial(_convpool_kernel, pos0=pos0),
        grid=(l // tm,),
        in_specs=[row, halo, _full((HALO, MIX)), row, halo, _full((HALO, MIX)),
                  _full((HALO, MIX)), _full((1, MIX)), _full((2, 1, MIX)), _full((MIX, MIX)), _full((1, MIX))],
        out_specs=[row, row],
        out_shape=[jax.ShapeDtypeStruct((l, MIX), BF16)] * 2,
        scratch_shapes=[pltpu.VMEM((tm + HALO, MIX), F32)],
        compiler_params=_cparams(("arbitrary",)),
        name="convpool",
    )(u, u, u_past, c, c, c_past, cw, cb, ln, pw_bd, ps)


def _convpool_step_kernel(u_ref, up_ref, c_ref, cp_ref, cw_ref, cb_ref, ln_ref, pw_ref, ps_ref, yb_ref, yc_ref,
                          *, pos0):
    u = u_ref[...]
    acc = cb_ref[...] + u * cw_ref[pl.ds(CONV_WIDTH - 1, 1), :]
    for w in range(CONV_HIST):
        acc = acc + up_ref[w] * cw_ref[pl.ds(w, 1), :]
    yb_ref[...] = _layernorm_silu(acc, ln_ref).astype(BF16)
    c = c_ref[...]
    run = c
    sums, counts = [], []
    for k in range(1, max(POOL_WINDOWS)):
        run = run + cp_ref[POOL_HIST - k]
        if k + 1 in POOL_WINDOWS:
            sums.append(run)
            counts.append(float(min(k + 1, pos0 + 1)))
    pooled = _pool_select(sums, counts, c).astype(BF16)
    yc_ref[...] = (_dot(pooled, pw_ref[...]) * ps_ref[...]).astype(BF16)


def _convpool_step(u, u_past, c, c_past, cw, cb, ln, pw_bd, ps, pos0):
    b = u.shape[0]
    return pl.pallas_call(
        functools.partial(_convpool_step_kernel, pos0=pos0),
        out_shape=[jax.ShapeDtypeStruct((b, MIX), BF16)] * 2,
        name="convpool_step",
    )(u, u_past, c, c_past, cw, cb, ln, pw_bd, ps)


def _nsa_paged_cmp_kernel(pt_ref, *refs, n_pages, past):
    pages = refs[:PG]
    (q_ref, pos_ref, w_ref, ovl_ref, bias_ref, misc_ref, oc_ref, idx_ref, rk_ref, rv_ref, scr) = refs[PG:]
    b = pl.program_id(0)
    p = pl.program_id(1)
    n_rows = past // CMP_STRIDE
    for g in range(PG):
        r0 = pl.multiple_of((p * PG + g) * 8, 8)
        for t in range(CMP_STRIDE):
            rows = pages[g][0, pl.ds(t, 8, stride=CMP_STRIDE), :]
            rk_ref[pl.ds(r0, 8), t * HD:(t + 1) * HD] = rows[:, 0:HD]
            rv_ref[pl.ds(r0, 8), t * HD:(t + 1) * HD] = rows[:, HD:2 * HD]

    @pl.when(p == n_pages // PG - 1)
    def _():
        kc = _compress(rk_ref[...], pos_ref, w_ref, 0, scr).astype(BF16)
        vc = _compress(rv_ref[...], pos_ref, w_ref, 1, scr).astype(BF16)
        q = q_ref[0]
        s = _dot_nt(q, kc) + bias_ref[...]
        m = jnp.max(s, axis=1, keepdims=True)
        e = jnp.exp(s - m)
        pr = (e / jnp.sum(e, axis=1, keepdims=True)).astype(BF16)
        o = _dot(pr, vc)
        gates = misc_ref[0]
        oc_ref[0] = o * gates[:, 0:1]
        impf = _dot(pr, ovl_ref[...])
        imp = jnp.sum(impf[0:NH], axis=0, keepdims=True)
        ns = imp.shape[1]
        j = lax.broadcasted_iota(jnp.int32, imp.shape, 1)
        qblk = past // SLC_BLOCK
        forced = (j == 0) | (j == qblk) | (j == qblk - 1)
        val = jnp.where(forced, BIG, jnp.where(j * SLC_BLOCK <= past, imp, -BIG))
        val = jnp.where(j <= qblk, val, -jnp.inf)
        lane = lax.broadcasted_iota(jnp.int32, (1, 128), 1)
        idx = jnp.zeros((1, 128), F32)
        for t, first in enumerate(_select_topk(val, min(N_SELECT, qblk + 1))):
            idx = jnp.where(lane == t, first, idx)
        idx_ref[0] = idx.astype(jnp.int32)


def _nsa_paged_cmp(page_table, cache, q8, pos, w, ovl, bias, misc8, past):
    bsz, n_pages = page_table.shape
    n_rows = past // CMP_STRIDE
    ns = ovl.shape[1]

    def page_spec(g):
        return pl.BlockSpec((1, PAGE, 128), lambda b, p, pt, _g=g: (pt[b, p * PG + _g], 0, 0))

    def const(shape):
        nd = len(shape)
        return pl.BlockSpec(shape, lambda b, p, pt, _nd=nd: (0,) * _nd)

    def per_b(shape):
        return pl.BlockSpec(shape, lambda b, p, pt: (b, 0, 0))

    grid_spec = pltpu.PrefetchScalarGridSpec(
        num_scalar_prefetch=1,
        grid=(bsz, n_pages // PG),
        in_specs=[page_spec(g) for g in range(PG)] + [
            per_b((1, 8, HD)), const((2, 2, 1, 1024)), const((2, 2, 1024, HD)), const((n_rows, ns)),
            const((8, n_rows)), per_b((1, 8, 128))],
        out_specs=[per_b((1, 8, HD)), per_b((1, 1, 128))],
        scratch_shapes=[pltpu.VMEM((n_rows, 1024), F32), pltpu.VMEM((n_rows, 1024), F32),
                        pltpu.VMEM((n_rows + 8, HD), F32)],
    )
    return pl.pallas_call(
        functools.partial(_nsa_paged_cmp_kernel, n_pages=n_pages, past=past),
        grid_spec=grid_spec,
        out_shape=[jax.ShapeDtypeStruct((bsz, 8, HD), F32), jax.ShapeDtypeStruct((bsz, 1, 128), jnp.int32)],
        compiler_params=_cparams(("arbitrary", "arbitrary")),
        name="nsa_paged_cmp",
    )(page_table, *([cache] * PG), q8, pos, w, ovl, bias, misc8)


def _nsa_paged_attn_kernel(idx_ref, *refs, past):
    blocks = refs[:N_SELECT]
    (q_ref, new_ref, win_ref, neww_ref, relt_ref, wbias_ref, misc_ref, oc_ref, ya_ref, kv_ref, wkv_ref) = refs[N_SELECT:]
    b = pl.program_id(0)
    n_sel = N_SELECT * SLC_BLOCK
    q = q_ref[0]
    for t in range(N_SELECT):
        kv_ref[pl.ds(t * SLC_BLOCK, SLC_BLOCK), :] = blocks[t][0, :, 128:256]
    kv_ref[pl.ds(n_sel, 128), :] = jnp.broadcast_to(new_ref[0][:, 128:256], (128, 128))
    lane = lax.broadcasted_iota(jnp.int32, (1, n_sel + 128), 1)
    pos = jnp.full((1, n_sel + 128), past, jnp.int32)
    for t in range(N_SELECT):
        in_blk = (lane >= t * SLC_BLOCK) & (lane < (t + 1) * SLC_BLOCK)
        pos = jnp.where(in_blk, idx_ref[b, N_SELECT + t] * SLC_BLOCK + lane - t * SLC_BLOCK, pos)
    hide = jnp.where(lane < n_sel, jnp.where(pos < past, 0.0, NEG), jnp.where(lane == n_sel, 0.0, NEG))
    bias = _t5_bias(relt_ref[...], past - pos)
    kv = kv_ref[...].astype(BF16)
    s_all = _dot_nt(q, kv) + bias + hide
    m = jnp.max(s_all, axis=1, keepdims=True)
    e = jnp.exp(s_all - m)
    pr = (e / jnp.sum(e, axis=1, keepdims=True)).astype(BF16)
    o_s = _dot(pr, kv)[:, HD:2 * HD]
    n_win = win_ref.shape[1]
    wkv_ref[pl.ds(0, n_win), :] = win_ref[0]
    wkv_ref[pl.ds(n_win, 128), :] = jnp.broadcast_to(neww_ref[0], (128, 128))
    wkv = wkv_ref[...].astype(BF16)
    s_w = _dot_nt(q, wkv) + wbias_ref[...]
    m = jnp.max(s_w, axis=1, keepdims=True)
    e = jnp.exp(s_w - m)
    pr = (e / jnp.sum(e, axis=1, keepdims=True)).astype(BF16)
    o_w = _dot(pr, wkv)[:, HD:2 * HD]
    gates = misc_ref[0]
    ya_ref[0] = (oc_ref[0] + gates[:, 1:2] * o_s + gates[:, 2:3] * o_w).astype(BF16)


def _nsa_paged_attn(sel, cache_half, qx8, nsa_new, win_state, win_new, relt8, wbias, misc8, oc, past):
    bsz = sel.shape[0]
    n_win = win_state.shape[1]

    def blk_spec(t):
        return pl.BlockSpec((1, SLC_BLOCK, 256), lambda b, ix, _t=t: (ix[b, _t], 0, 0))

    def const(shape):
        nd = len(shape)
        return pl.BlockSpec(shape, lambda b, ix, _nd=nd: (0,) * _nd)

    def per_b(shape):
        return pl.BlockSpec(shape, lambda b, ix: (b, 0, 0))

    n_sel = N_SELECT * SLC_BLOCK
    grid_spec = pltpu.PrefetchScalarGridSpec(
        num_scalar_prefetch=1,
        grid=(bsz,),
        in_specs=[blk_spec(t) for t in range(N_SELECT)] + [
            per_b((1, 8, 128)), per_b((1, 1, 256)), per_b((1, n_win, 128)), per_b((1, 1, 128)),
            const((8, N_BUCKETS)), const((8, n_win + 128)), per_b((1, 8, 128)), per_b((1, 8, HD))],
        out_specs=per_b((1, 8, HD)),
        scratch_shapes=[pltpu.VMEM((n_sel + 128, 128), F32), pltpu.VMEM((n_win + 128, 128), F32)],
    )
    return pl.pallas_call(
        functools.partial(_nsa_paged_attn_kernel, past=past),
        grid_spec=grid_spec,
        out_shape=jax.ShapeDtypeStruct((bsz, 8, HD), BF16),
        compiler_params=_cparams(("arbitrary",)),
        name="nsa_paged_attn",
    )(sel, *([cache_half] * N_SELECT), qx8, nsa_new, win_state, win_new, relt8, wbias, misc8, oc)


def _fox_paged_kernel(pt_ref, *refs, n_pages):
    kv_pages = refs[:PG]
    lf_pages = refs[PG:2 * PG]
    (qbd_ref, new_ref, lfnew_ref, o_ref, m_ref, l_ref, acc_ref, car_ref) = refs[2 * PG:]
    p = pl.program_id(1)
    qbd = qbd_ref[0]

    @pl.when(p == 0)
    def _():
        new = new_ref[0]
        kn = new[:, 0:MIX].astype(BF16).astype(F32)
        m_ref[...] = jnp.sum(qbd.astype(F32) * kn, axis=1, keepdims=True)
        l_ref[...] = jnp.ones((8, 1), F32)
        acc_ref[...] = jnp.broadcast_to(new[:, MIX:2 * MIX].astype(BF16).astype(F32), (8, MIX))
        car_ref[...] = lfnew_ref[0]

    later = jnp.where(lax.broadcasted_iota(jnp.int32, (PAGE, PAGE), 0) > lax.broadcasted_iota(jnp.int32, (PAGE, PAGE), 1),
                      1.0, 0.0).astype(BF16)
    carry = car_ref[...]
    s_parts = [None] * PG
    for g in range(PG - 1, -1, -1):
        lf = lf_pages[g][0]
        k = kv_pages[g][0, :, 0:MIX].astype(BF16)
        s_parts[g] = _dot_nt(qbd, k) + carry + _dot01(lf, later)
        carry = carry + jnp.sum(lf, axis=1, keepdims=True)
    car_ref[...] = carry
    s = jnp.concatenate(s_parts, axis=1)
    m_old = m_ref[...]
    m_new = jnp.maximum(m_old, jnp.max(s, axis=1, keepdims=True))
    alpha = jnp.exp(m_old - m_new)
    pr = jnp.exp(s - m_new)
    l_ref[...] = alpha * l_ref[...] + jnp.sum(pr, axis=1, keepdims=True)
    acc = alpha * acc_ref[...]
    prb = pr.astype(BF16)
    for g in range(PG):
        acc = acc + _dot(prb[:, g * PAGE:(g + 1) * PAGE], kv_pages[g][0, :, MIX:2 * MIX].astype(BF16))
    acc_ref[...] = acc
    m_ref[...] = m_new

    @pl.when(p == n_pages // PG - 1)
    def _():
        out = acc / l_ref[...]
        own = lax.broadcasted_iota(jnp.int32, out.shape, 1) // HD == lax.broadcasted_iota(jnp.int32, out.shape, 0)
        o_ref[0] = jnp.sum(jnp.where(own, out, 0.0), axis=0, keepdims=True).astype(BF16)


def _fox_paged(page_table, cache_kv, cache_lf, qbd, kv_new, lf_new):
    bsz, n_pages = page_table.shape
    nchunk = n_pages // PG

    def kv_spec(g):
        return pl.BlockSpec((1, PAGE, 2 * MIX), lambda b, p, pt, _g=g: (pt[b, (nchunk - 1 - p) * PG + _g], 0, 0))

    def lf_spec(g):
        return pl.BlockSpec((1, 8, PAGE), lambda b, p, pt, _g=g: (pt[b, (nchunk - 1 - p) * PG + _g], 0, 0))

    def per_b(shape):
        return pl.BlockSpec(shape, lambda b, p, pt: (b, 0, 0))

    grid_spec = pltpu.PrefetchScalarGridSpec(
        num_scalar_prefetch=1,
        grid=(bsz, nchunk),
        in_specs=[kv_spec(g) for g in range(PG)] + [lf_spec(g) for g in range(PG)] + [
            per_b((1, 8, MIX)), per_b((1, 1, 2 * MIX)), per_b((1, 8, 1))],
        out_specs=per_b((1, 1, MIX)),
        scratch_shapes=[pltpu.VMEM((8, 1), F32), pltpu.VMEM((8, 1), F32), pltpu.VMEM((8, MIX), F32),
                        pltpu.VMEM((8, 1), F32)],
    )
    return pl.pallas_call(
        functools.partial(_fox_paged_kernel, n_pages=n_pages),
        grid_spec=grid_spec,
        out_shape=jax.ShapeDtypeStruct((bsz, 1, MIX), BF16),
        compiler_params=_cparams(("arbitrary", "arbitrary")),
        name="fox_paged",
    )(page_table, *([cache_kv] * PG), *([cache_lf] * PG), qbd, kv_new, lf_new)


def _layer_params(l, norm_g, ffn_gate, ffn_up, ffn_down, w_in, fox_f_bias, nsa_cmp_pos, nsa_cmp_w,
                  conv_w, conv_b, conv_ln, pool_w, pool_scale, w_branch, w_out):
    wi = w_in[l]
    o_aq, o_akv, o_ag, o_b, o_c, o_d, o_f, o_m = np.cumsum([0, 256, 384, 12, 512, 256, 768, 4])
    misc = jnp.concatenate([wi[:, o_ag:o_ag + 12], wi[:, o_f:o_f + 4], jnp.zeros((D_MODEL, 112), F32)], axis=1)
    w_small = jnp.concatenate([wi[:, o_aq:o_aq + 256], wi[:, o_akv:o_akv + 384], misc, wi[:, o_b:o_b + 512],
                               wi[:, o_c:o_c + 256], wi[:, o_d:o_d + 768]], axis=1).astype(BF16)
    fb_row = jnp.zeros((1, 128), F32).at[0, 12:16].set(fox_f_bias[l])
    cw = nsa_cmp_w[l].reshape(2, 2, CMP_STRIDE * HD, HD).astype(BF16)
    cpos = nsa_cmp_pos[l].reshape(2, 2, 1, CMP_STRIDE * HD)
    group = MIX // len(POOL_WINDOWS)
    pw_bd = jnp.zeros((MIX, MIX), F32)
    for g in range(len(POOL_WINDOWS)):
        pw_bd = pw_bd.at[g * group:(g + 1) * group, g * group:(g + 1) * group].set(pool_w[l, g])
    cw_conv = jnp.concatenate([conv_w[l], jnp.zeros((HALO - CONV_WIDTH, MIX), F32)], axis=0)
    return dict(
        g=[norm_g[l, k].reshape(1, D_MODEL) for k in range(3)],
        ffn=[(ffn_gate[l, k].astype(BF16), ffn_up[l, k].astype(BF16), ffn_down[l, k].astype(BF16)) for k in range(2)],
        w_small=w_small, w_merge=wi[:, o_m:].astype(BF16), fb_row=fb_row, cmp_w=cw, cmp_pos=cpos,
        conv_w=cw_conv, conv_b=conv_b[l].reshape(1, MIX), conv_ln=conv_ln[l].reshape(2, 1, MIX),
        pool_w=pw_bd.astype(BF16), pool_scale=pool_scale[l].reshape(1, MIX),
        w_branch=w_branch[l].astype(BF16), w_out=w_out[l].astype(BF16))


def _overlap(n_rows, n_blocks, n_pad):
    cs = np.arange(n_rows)[:, None] * CMP_STRIDE
    j0 = np.arange(n_pad)[None, :] * SLC_BLOCK
    ov = (cs < j0 + SLC_BLOCK) & (cs + CMP_LEN > j0) & (np.arange(n_pad)[None, :] < n_blocks)
    return jnp.asarray(ov.astype(np.float32), dtype=BF16)


def _lookup(bt, dist, valid):
    d = np.clip(dist, 0, NDIST - 1)
    return jnp.where(jnp.asarray(valid)[None], bt[:, d], NEG)


def _prompt_tables(bt, n_rows):
    r = np.arange(TQ)[:, None]
    nvar = 33
    m = np.arange(256)[None, :]
    dist = np.stack([r - CMP_STRIDE * (m - 8 * v) - (CMP_LEN - 1) for v in range(nvar)])
    tab_c = _lookup(bt, dist, dist >= 0)
    tab_c = jnp.transpose(tab_c, (1, 0, 2, 3)).reshape(nvar, NH * TQ, 256)
    far = jnp.repeat(bt[:, NDIST - 1], TQ).reshape(NH * TQ, 1)
    nv = TK // TQ
    mk = np.arange(TK)[None, :]
    d_prev = np.stack([TQ * v + TK + r - mk for v in range(nv)])
    d_diag = np.stack([TQ * v + r - mk for v in range(nv)])
    ones = np.ones_like(d_prev, bool)
    tsel = jnp.stack([_lookup(bt, np.full_like(d_prev, NDIST - 1), ones), _lookup(bt, d_prev, ones),
                      _lookup(bt, d_diag, d_diag >= 0)], axis=2)
    twin = jnp.stack([_lookup(bt, d_prev, d_prev <= WINDOW), _lookup(bt, d_diag, d_diag >= 0)], axis=2)
    tsel = jnp.transpose(tsel, (1, 2, 0, 3, 4)).reshape(nv, 3, NH * TQ, TK)
    twin = jnp.transpose(twin, (1, 2, 0, 3, 4)).reshape(nv, 2, NH * TQ, TK)
    return tab_c, far, tsel, twin


def _prompt_layer(x, lp, bt, fg, final):
    l = x.shape[0]
    x = _ffn(x, lp["g"][0], *lp["ffn"][0], fg)
    (qn, nsa_kv, win_kv, misc, u, cin, fq, fox_kv, ksel, kwin, vcat, fk, fv) = _inproj(x, lp["g"][1], lp["w_small"], lp["fb_row"])
    n_rows = l // CMP_STRIDE
    ns = l // SLC_BLOCK
    rk = nsa_kv[:, 0:HD].reshape(n_rows, CMP_STRIDE * HD)
    rv = nsa_kv[:, HD:2 * HD].reshape(n_rows, CMP_STRIDE * HD)
    kc, vc = _compress_prompt(rk, rv, lp["cmp_pos"], lp["cmp_w"])
    tab_c, far, tsel, twin = _prompt_tables(bt, n_rows)
    oc, selb = _cmp_attn(qn, kc, vc, _overlap(n_rows, ns, ns), tab_c, far, misc)
    ya = _nsa_attn(qn, selb, ksel, kwin, vcat, tsel, twin, misc, oc)
    zeros = jnp.zeros((HALO, MIX), F32)
    yb, yc = _convpool(u, zeros, cin, zeros, lp["conv_w"], lp["conv_b"], lp["conv_ln"], lp["pool_w"], lp["pool_scale"], 0)
    logf = misc[:, 12:16]
    negc = _neg_cumsum(logf.T.reshape(NH, l // 128, 128)).reshape(NH, l // TF, 1, TF)
    yd = _fox_attn(fq, fk, fv, negc)
    x = _merge(x, lp["g"][1], lp["w_merge"], lp["w_branch"], lp["w_out"], ya, yb, yc, yd)
    x = _ffn(x, lp["g"][2], *lp["ffn"][1], fg, final)
    state = (nsa_kv, fox_kv, logf, win_kv[l - min(WINDOW, l):], u[l - CONV_HIST:], cin[l - POOL_HIST:])
    return x, state


def _sample_layer(x, lp, relt8, bt, fg, final, page_table, cache_nsa, cache_fox_kv, cache_fox_logf, st_win, st_conv,
                  st_pool):
    bsz = x.shape[0]
    n_pages = page_table.shape[1]
    past = n_pages * PAGE
    n_phys = cache_nsa.shape[0]
    x = _ffn(x, lp["g"][0], *lp["ffn"][0], fg)
    (qn, nsa_kv, win_kv, misc, u, cin, fq, fox_kv, _, _, _, _, _) = _inproj(x, lp["g"][1], lp["w_small"], lp["fb_row"])
    pad4 = lambda a: jnp.concatenate([a, jnp.zeros((bsz, 8 - NH) + a.shape[2:], a.dtype)], axis=1)
    q8 = pad4(jnp.transpose(qn, (1, 0, 2)))
    qx8 = jnp.concatenate([q8, jnp.zeros_like(q8)], axis=2)
    gates = misc[:, 0:3 * NH].reshape(bsz, 3, NH)
    misc8 = jnp.zeros((bsz, 8, 128), F32).at[:, 0:NH, 0:3].set(jnp.transpose(gates, (0, 2, 1)))
    n_rows = past // CMP_STRIDE
    ns = past // SLC_BLOCK + 1
    ns_pad = -(-ns // 128) * 128
    c_end = np.arange(n_rows) * CMP_STRIDE + CMP_LEN - 1
    valid_c = (c_end <= past) & (np.arange(n_rows) < n_rows - 1)
    bias_c = jnp.concatenate([_lookup(bt, past - c_end, valid_c), jnp.full((8 - NH, n_rows), NEG, F32)], axis=0)
    oc, idx = _nsa_paged_cmp(page_table, cache_nsa.reshape(n_phys, PAGE, 4 * HD), q8, lp["cmp_pos"], lp["cmp_w"],
                             _overlap(n_rows, ns, ns_pad), bias_c, misc8, past)
    n_win = st_win.shape[1]
    dw = np.concatenate([n_win - np.arange(n_win), np.zeros(128, np.int64)])
    vw = np.concatenate([np.ones(n_win + 1, bool), np.zeros(127, bool)])
    wbias = jnp.concatenate([_lookup(bt, dw, vw), jnp.full((8 - NH, n_win + 128), NEG, F32)], axis=0)
    blk = idx.reshape(bsz, 128)[:, 0:N_SELECT]
    held = jnp.minimum(blk, past // SLC_BLOCK - 1)
    rows = jnp.take_along_axis(page_table, held // 2, axis=1) * 2 + held % 2
    ya = _nsa_paged_attn(jnp.concatenate([rows, blk], axis=1), cache_nsa.reshape(n_phys * 2, SLC_BLOCK, 4 * HD), qx8,
                         nsa_kv.reshape(bsz, 1, 256), st_win.reshape(bsz, n_win, 128), win_kv.reshape(bsz, 1, 128),
                         relt8, wbias, misc8, oc, past)
    ya = jnp.transpose(ya[:, 0:NH], (1, 0, 2))
    yb, yc = _convpool_step(u, jnp.transpose(st_conv, (1, 0, 2)), cin, jnp.transpose(st_pool, (1, 0, 2)),
                            lp["conv_w"], lp["conv_b"], lp["conv_ln"], lp["pool_w"], lp["pool_scale"], past)
    logf = misc[:, 12:16]
    fq8 = pad4(jnp.transpose(fq, (1, 0, 2)))
    own = (np.arange(MIX)[None, :] // HD == np.arange(8)[:, None])
    qbd = jnp.where(jnp.asarray(own)[None], jnp.tile(fq8, (1, 1, NH)), jnp.zeros((), BF16))
    lf_pages = jnp.concatenate([jnp.transpose(cache_fox_logf, (0, 2, 1)), jnp.zeros((n_phys, 8 - NH, PAGE), F32)], axis=1)
    lf_new = pad4(logf.reshape(bsz, NH, 1))
    yd = _fox_paged(page_table, cache_fox_kv.reshape(n_phys, PAGE, 2 * MIX), lf_pages, qbd,
                    fox_kv.reshape(bsz, 1, 2 * MIX), lf_new)
    yd = jnp.transpose(yd.reshape(bsz, NH, HD), (1, 0, 2))
    x = _merge(x, lp["g"][1], lp["w_merge"], lp["w_branch"], lp["w_out"], ya, yb, yc, yd)
    x = _ffn(x, lp["g"][2], *lp["ffn"][1], fg, final)
    new_win =jnp.concatenate([st_win.reshape(bsz, n_win, 128)[:, 1:], win_kv[:, None, :]], axis=1)
    new_conv = jnp.concatenate([st_conv[:, 1:], u[:, None, :]], axis=1)
    new_pool = jnp.concatenate([st_pool[:, 1:], cin[:, None, :]], axis=1)
    state = (nsa_kv, fox_kv, logf, new_win, new_conv, new_pool)
    return x, state


def kernel(x_prompt, x_sample, cache_nsa, cache_fox_kv, cache_fox_logf, state_nsa_win, state_conv, state_pool,
           page_table, norm_g, ffn_gate, ffn_up, ffn_down, w_in, fox_f_bias, nsa_cmp_pos, nsa_cmp_w, rel_bias,
           conv_w, conv_b, conv_ln, pool_w, pool_scale, w_branch, w_out, final_norm_g):
    assert x_prompt.shape[0] == 1 and x_sample.shape[1] == 1
    depth = norm_g.shape[0]
    l = x_prompt.shape[1]
    bsz = x_sample.shape[0]
    win_keep = state_nsa_win.shape[2]
    assert l % SUPER == 0 and page_table.shape[1] * PAGE >= max(WINDOW, SUPER) and win_keep == WINDOW
    relt8 = jnp.concatenate([rel_bias.T, jnp.zeros((8 - NH, N_BUCKETS), F32)], axis=0)
    bt = _bias_table(relt8)[0:NH]
    fg = final_norm_g.reshape(1, D_MODEL)
    xp = x_prompt.reshape(l, D_MODEL)
    xs = x_sample.reshape(bsz, D_MODEL)
    st_p, st_s = [], []
    for layer in range(depth):
        lp = _layer_params(layer, norm_g, ffn_gate, ffn_up, ffn_down, w_in, fox_f_bias, nsa_cmp_pos, nsa_cmp_w,
                           conv_w, conv_b, conv_ln, pool_w, pool_scale, w_branch, w_out)
        final = layer == depth - 1
        xp, sp = _prompt_layer(xp, lp, bt, fg, final)
        st_p.append(sp)
        xs, ss = _sample_layer(xs, lp, relt8, bt, fg, final, page_table, cache_nsa[layer], cache_fox_kv[layer],
                               cache_fox_logf[layer], state_nsa_win[layer], state_conv[layer], state_pool[layer])
        st_s.append(ss)

    def stack(states, k, shape):
        return jnp.stack([s[k] for s in states]).reshape(shape)

    return (xp.reshape(1, l, D_MODEL), xs.reshape(bsz, 1, D_MODEL),
            stack(st_p, 0, (depth, 1, l, 4, 1, HD)), stack(st_s, 0, (depth, bsz, 1, 4, 1, HD)),
            stack(st_p, 1, (depth, 1, l, 2, NH, HD)), stack(st_s, 1, (depth, bsz, 1, 2, NH, HD)),
            stack(st_p, 2, (depth, 1, l, NH)), stack(st_s, 2, (depth, bsz, 1, NH)),
            stack(st_p, 3, (depth, 1, win_keep, 2, 1, HD)), stack(st_s, 3, (depth, bsz, win_keep, 2, 1, HD)),
            stack(st_p, 4, (depth, 1, CONV_HIST, MIX)), stack(st_s, 4, (depth, bsz, CONV_HIST, MIX)),
            stack(st_p, 5, (depth, 1, POOL_HIST, MIX)), stack(st_s, 5, (depth, bsz, POOL_HIST, MIX)))
```

```python
import functools
import math

import numpy as np
import jax
import jax.numpy as jnp
from jax import lax
from jax.experimental import pallas as pl
from jax.experimental.pallas import tpu as pltpu

F32 = jnp.float32
BF16 = jnp.bfloat16

D_MODEL = 1024
N_BRANCH = 4
MIX = D_MODEL // 4
HD = 64
NH = MIX // HD
CMP_STRIDE = 16
CMP_LEN = 32
SLC_BLOCK = 64
N_SELECT = 16
WINDOW = 512
N_BUCKETS = 32
REL_EXACT = 16
REL_MAX_DIST = 128
CONV_WIDTH = 31
CONV_HIST = CONV_WIDTH - 1
POOL_WINDOWS = (2, 4, 8, 16)
POOL_HIST = 15
D_FF = 2816
EPS = 1e-6
NEG = -1e30
BIG = 1e6
PAGE = 128

LANE = 128
TQ = 128
TK = 1024
SUPER = 64 * SLC_BLOCK
TF = 256
TKF = 1024
TM = 256
FF_CHUNK = 1408
HALO = 32
NDIST = 1024
PG = 8
VMEM_LIMIT = 56 * 1024 * 1024

C_AQ, C_NSA, C_WIN, C_MISC, C_CONV, C_POOL, C_FOX = 0, 256, 512, 640, 768, 1280, 1536
N_SMALL = 2304


def _cparams(sem):
    return pltpu.CompilerParams(dimension_semantics=sem, vmem_limit_bytes=VMEM_LIMIT)


def _full(shape, single=True):
    nd = len(shape)
    kw = dict(pipeline_mode=pl.Buffered(1)) if single else {}
    return pl.BlockSpec(shape, lambda *a, _nd=nd: (0,) * _nd, **kw)


def _rms(x, g):
    ms = jnp.mean(x * x, axis=-1, keepdims=True)
    return x * lax.rsqrt(ms + EPS) * g


def _dot(a, b):
    return jnp.dot(a, b, preferred_element_type=F32)


def _dot_nt(a, b):
    return lax.dot_general(a, b, (((1,), (1,)), ((), ())), preferred_element_type=F32)


def _split3(x):
    hi = x.astype(BF16)
    r1 = x - hi.astype(F32)
    mid = r1.astype(BF16)
    lo = (r1 - mid.astype(F32)).astype(BF16)
    return hi, mid, lo


def _dot01(x, ones_bf16):
    hi, mid, lo = _split3(x)
    return _dot(hi, ones_bf16) + _dot(mid, ones_bf16) + _dot(lo, ones_bf16)


def _dot01_l(ones_bf16, x):
    hi, mid, lo = _split3(x)
    return _dot(ones_bf16, hi) + _dot(ones_bf16, mid) + _dot(ones_bf16, lo)


def _log_sigmoid(x):
    return jnp.minimum(x, 0.0) - jnp.log(1.0 + jnp.exp(-jnp.abs(x)))


def _t5_bucket(dist):
    n = jnp.maximum(dist, 0)
    ratio = jnp.log(jnp.maximum(n, 1).astype(F32) / REL_EXACT) / math.log(REL_MAX_DIST / REL_EXACT)
    large = REL_EXACT + (ratio * (N_BUCKETS - REL_EXACT)).astype(jnp.int32)
    return jnp.where(n < REL_EXACT, n, jnp.minimum(large, N_BUCKETS - 1))


def _t5_bias(relt, dist):
    bucket = _t5_bucket(dist)
    rows = lax.broadcasted_iota(jnp.int32, (N_BUCKETS, dist.shape[1]), 0)
    onehot = jnp.where(rows == bucket, 1.0, 0.0).astype(BF16)
    return _dot01(relt, onehot)


def _ffn_kernel(x_ref, g_ref, wg_ref, wu_ref, wd_ref, fg_ref, o_ref, *, final):
    x = x_ref[...]
    h = _rms(x, g_ref[...]).astype(BF16)
    acc = jnp.zeros_like(x)
    for c in range(D_FF // FF_CHUNK):
        sl = slice(c * FF_CHUNK, (c + 1) * FF_CHUNK)
        a = _dot(h, wg_ref[:, sl])
        b = _dot(h, wu_ref[:, sl])
        t = (a * jax.nn.sigmoid(a) * b).astype(BF16)
        acc = acc + _dot(t, wd_ref[sl, :])
    out = x + 0.5 * acc
    o_ref[...] = _rms(out, fg_ref[...]) if final else out


def _ffn(x, g, wg, wu, wd, fg, final=False):
    m = x.shape[0]
    tm = min(TM, m)
    row = pl.BlockSpec((tm, D_MODEL), lambda i: (i, 0))
    return pl.pallas_call(
        functools.partial(_ffn_kernel, final=final),
        grid=(m // tm,),
        in_specs=[row, _full((1, D_MODEL)), _full((D_MODEL, D_FF)), _full((D_MODEL, D_FF)),
                  _full((D_FF, D_MODEL)), _full((1, D_MODEL))],
        out_specs=row,
        out_shape=jax.ShapeDtypeStruct((m, D_MODEL), F32),
        compiler_params=_cparams(("arbitrary",)),
        name="ffn",
    )(x, g, wg, wu, wd, fg)


def _inproj_kernel(x_ref, g_ref, w_ref, fb_ref,
                   qn_ref, nsa_ref, win_ref, misc_ref, u_ref, cin_ref, fq_ref, fkv_ref,
                   ksel_ref, kwin_ref, vcat_ref, fk_ref, fv_ref, csum_ref):
    tm = x_ref.shape[0]
    h = _rms(x_ref[...], g_ref[...]).astype(BF16)
    z = _dot(h, w_ref[...])
    scale = HD ** -0.5
    zm = z[:, C_MISC:C_MISC + 128]
    lane = lax.broadcasted_iota(jnp.int32, zm.shape, 1)
    logf = _log_sigmoid(zm + fb_ref[...])
    misc_ref[...] = jnp.where(lane < 3 * NH, jax.nn.sigmoid(zm), logf)
    @pl.when(pl.program_id(0) == 0)
    def _():
        csum_ref[...] = jnp.zeros_like(csum_ref)
    tri = jnp.where(lax.broadcasted_iota(jnp.int32, (tm, tm), 1) <= lax.broadcasted_iota(jnp.int32, (tm, tm), 0),
                    1.0, 0.0).astype(BF16)
    lf = jnp.where((lane >= 3 * NH) & (lane < 4 * NH), logf, 0.0)
    csum = _dot01_l(tri, lf) + csum_ref[...]
    csum_ref[...] = csum[tm - 1:tm, :]
    feat = lax.broadcasted_iota(jnp.int32, (tm, HD), 1)
    ones3 = jnp.where(feat < 3, 1.0, 0.0).astype(BF16)
    for hh in range(NH):
        qn_ref[hh] = (z[:, C_AQ + HD * hh:C_AQ + HD * (hh + 1)] * scale).astype(BF16)
        fq = (z[:, C_FOX + HD * hh:C_FOX + HD * (hh + 1)] * scale).astype(BF16)
        fq_ref[hh] = jnp.concatenate([fq, ones3], axis=1)
        hi, mid, lo = (p.astype(F32) for p in _split3(-csum[:, 3 * NH + hh:3 * NH + hh + 1]))
        extra = jnp.where(feat == 0, hi, jnp.where(feat == 1, mid, jnp.where(feat == 2, lo, 0.0)))
        fk = z[:, C_FOX + MIX + HD * hh:C_FOX + MIX + HD * (hh + 1)].astype(BF16)
        fk_ref[hh] = jnp.concatenate([fk, extra.astype(BF16)], axis=1)
    fv_ref[...] = z[:, C_FOX + 2 * MIX:C_FOX + 3 * MIX].astype(BF16)
    nsa = z[:, C_NSA:C_NSA + 256]
    nsa_ref[...] = nsa
    win = z[:, C_WIN:C_WIN + 128]
    win_ref[...] = win
    u_ref[...] = z[:, C_CONV:C_CONV + MIX] * jax.nn.sigmoid(z[:, C_CONV + MIX:C_CONV + 2 * MIX])
    cin_ref[...] = z[:, C_POOL:C_POOL + MIX]
    fkv_ref[...] = z[:, C_FOX + MIX:C_FOX + 3 * MIX]
    t = pl.program_id(0) * tm + lax.broadcasted_iota(jnp.int32, (tm, HD), 0)
    feat = lax.broadcasted_iota(jnp.int32, (tm, HD), 1)
    onehot = jnp.where(((t // SLC_BLOCK) % 64) == feat, 1.0, 0.0).astype(BF16)
    ksel_ref[...] = jnp.concatenate([nsa[:, 128:192].astype(BF16), onehot], axis=1)
    kwin_ref[...] = jnp.concatenate([win[:, 0:64].astype(BF16), jnp.zeros((tm, HD), BF16)], axis=1)
    vcat_ref[...] = jnp.concatenate([nsa[:, 192:256].astype(BF16), win[:, 64:128].astype(BF16)], axis=1)


def _inproj(x, g, w_small, fbias_row):
    m = x.shape[0]
    tm = min(TM, m)

    def row(n):
        return pl.BlockSpec((tm, n), lambda i: (i, 0))

    heads = pl.BlockSpec((NH, tm, HD), lambda i: (0, i, 0))
    wide = pl.BlockSpec((NH, tm, 128), lambda i: (0, i, 0))
    shapes = [((NH, m, HD), BF16, heads), ((m, 256), F32, row(256)), ((m, 128), F32, row(128)),
              ((m, 128), F32, row(128)), ((m, MIX), F32, row(MIX)), ((m, MIX), F32, row(MIX)),
              ((NH, m, 128), BF16, wide), ((m, 2 * MIX), F32, row(2 * MIX)),
              ((m, 128), BF16, row(128)), ((m, 128), BF16, row(128)), ((m, 128), BF16, row(128)),
              ((NH, m, 128), BF16, wide), ((m, MIX), BF16, row(MIX))]
    return pl.pallas_call(
        _inproj_kernel,
        grid=(m // tm,),
        in_specs=[row(D_MODEL), _full((1, D_MODEL)), _full((D_MODEL, N_SMALL)), _full((1, 128))],
        out_specs=[s[2] for s in shapes],
        out_shape=[jax.ShapeDtypeStruct(s[0], s[1]) for s in shapes],
        scratch_shapes=[pltpu.VMEM((1, 128), F32)],
        compiler_params=_cparams(("arbitrary",)),
        name="inproj",
    )(x, g, w_small, fbias_row)


def _merge_kernel(x_ref, g_ref, wm_ref, wb_ref, wo_ref, ya_ref, yb_ref, yc_ref, yd_ref, o_ref):
    x = x_ref[...]
    h = _rms(x, g_ref[...]).astype(BF16)
    mix = jnp.zeros_like(x)
    for n in range(N_BRANCH):
        gate = jax.nn.sigmoid(_dot(h, wm_ref[:, n * D_MODEL:(n + 1) * D_MODEL]))
        if n == 0:
            proj = jnp.zeros_like(x)
            for hh in range(NH):
                proj = proj + _dot(ya_ref[hh], wb_ref[n, hh * HD:(hh + 1) * HD, :])
        else:
            proj = _dot((yb_ref, yc_ref, yd_ref)[n - 1][...], wb_ref[n])
        mix = mix + gate * proj
    o_ref[...] = x + _dot(mix.astype(BF16), wo_ref[...])


def _merge(x, g, w_merge, w_branch, w_out, ya, yb, yc, yd):
    m = x.shape[0]
    tm = min(TM, m)
    row = pl.BlockSpec((tm, D_MODEL), lambda i: (i, 0))
    heads = pl.BlockSpec((NH, tm, HD), lambda i: (0, i, 0))
    mixrow = pl.BlockSpec((tm, MIX), lambda i: (i, 0))
    return pl.pallas_call(
        _merge_kernel,
        grid=(m // tm,),
        in_specs=[row, _full((1, D_MODEL)), _full((D_MODEL, N_BRANCH * D_MODEL)),
                  _full((N_BRANCH, MIX, D_MODEL)), _full((D_MODEL, D_MODEL)),
                  heads, mixrow, mixrow, mixrow],
        out_specs=row,
        out_shape=jax.ShapeDtypeStruct((m, D_MODEL), F32),
        compiler_params=_cparams(("arbitrary",)),
        name="merge",
    )(x, g, w_merge, w_branch, w_out, ya, yb, yc, yd)


def _bias_table_kernel(relt_ref, o_ref):
    dist = lax.broadcasted_iota(jnp.int32, (1, NDIST), 1)
    o_ref[...] = _t5_bias(relt_ref[...], dist)


def _bias_table(relt8):
    return pl.pallas_call(
        _bias_table_kernel,
        out_shape=jax.ShapeDtypeStruct((8, NDIST), F32),
        name="bias_table",
    )(relt8)


def _compress(r, pos_ref, w_ref, t, scr):
    n = r.shape[0]
    a = _dot((r + pos_ref[t, 0]).astype(BF16), w_ref[t, 0])
    b = _dot((r + pos_ref[t, 1]).astype(BF16), w_ref[t, 1])
    scr[pl.ds(0, n), :] = b
    scr[pl.ds(n, 8), :] = jnp.zeros((8, HD), F32)
    return a + scr[pl.ds(1, n), :]


def _compress_kernel(rk_ref, rv_ref, pos_ref, w_ref, kc_ref, vc_ref, scr):
    kc_ref[...] = _compress(rk_ref[...], pos_ref, w_ref, 0, scr).astype(BF16)
    vc_ref[...] = _compress(rv_ref[...], pos_ref, w_ref, 1, scr).astype(BF16)


def _compress_prompt(rk, rv, pos, w):
    r = rk.shape[0]
    return pl.pallas_call(
        _compress_kernel,
        in_specs=[_full((r, 1024), False), _full((r, 1024), False), _full((2, 2, 1, 1024), False),
                  _full((2, 2, 1024, HD), False)],
        out_specs=[_full((r, HD), False)] * 2,
        grid=(1,),
        out_shape=[jax.ShapeDtypeStruct((r, HD), BF16)] * 2,
        scratch_shapes=[pltpu.VMEM((r + 8, HD), F32)],
        compiler_params=_cparams(("arbitrary",)),
        name="nsa_compress",
    )(rk, rv, pos, w)


def _select_topk(val, n_sel):
    ns = val.shape[1]
    j = lax.broadcasted_iota(jnp.int32, val.shape, 1).astype(F32)
    picks = []
    for _ in range(n_sel):
        mx = jnp.max(val, axis=1, keepdims=True)
        first = jnp.min(jnp.where(val == mx, j, float(ns)), axis=1, keepdims=True)
        picks.append(first)
        val = jnp.where(j == first, -jnp.inf, val)
    return picks


def _cmp_attn_kernel(q_ref, kc_ref, vc_ref, ovl_ref, tab_ref, misc_ref, oc_ref, selb_ref,
                     *, n_rows, n_blocks):
    i = pl.program_id(0)
    q = q_ref[...].reshape(NH * TQ, HD)
    w0 = 8 * i - 8
    c0 = jnp.clip((w0 // 128) * 128, 0, n_rows - 256)
    c0 = pl.multiple_of(c0, 128)
    s = _dot_nt(q, kc_ref[...])
    n_idx = lax.broadcasted_iota(jnp.int32, s.shape, 1)
    s_far = s + jnp.where(n_idx < c0, 0.0, NEG)
    s_win = _dot_nt(q, kc_ref[pl.ds(c0, 256), :]) + tab_ref[0]
    m = jnp.maximum(jnp.max(s_far, axis=1, keepdims=True), jnp.max(s_win, axis=1, keepdims=True))
    p_far = jnp.exp(s_far - m)
    p_win = jnp.exp(s_win - m)
    l = jnp.sum(p_far, axis=1, keepdims=True) + jnp.sum(p_win, axis=1, keepdims=True)
    inv = jnp.where(m > 0.5 * NEG, 1.0 / l, 0.0)
    pb_far = (p_far * inv).astype(BF16)
    pb_win = (p_win * inv).astype(BF16)
    o = _dot(pb_far, vc_ref[...]) + _dot(pb_win, vc_ref[pl.ds(c0, 256), :])
    impf = _dot(pb_far, ovl_ref[...]) + _dot(pb_win, ovl_ref[pl.ds(c0, 256), :])
    imp = impf[0:TQ] + impf[TQ:2 * TQ] + impf[2 * TQ:3 * TQ] + impf[3 * TQ:4 * TQ]
    gates = misc_ref[...]
    for hh in range(NH):
        oc_ref[hh] = o[hh * TQ:(hh + 1) * TQ] * gates[:, hh:hh + 1]
    j = lax.broadcasted_iota(jnp.int32, imp.shape, 1)
    qpos = i * TQ + lax.broadcasted_iota(jnp.int32, imp.shape, 0)
    qblk = qpos // SLC_BLOCK
    forced = (j == 0) | (j == qblk) | (j == qblk - 1)
    val = jnp.where(forced, BIG, jnp.where(j * SLC_BLOCK <= qpos, imp, -BIG))
    jf = j.astype(F32)
    selb = jnp.full(imp.shape, NEG, F32)
    for first in _select_topk(val, min(N_SELECT, n_blocks)):
        selb = jnp.where(jf == first, 0.0, selb)
    selb_ref[...] = selb.astype(BF16)


def _cmp_attn(qh, kc, vc, ovl, tab, misc):
    l = qh.shape[1]
    r = kc.shape[0]
    ns = ovl.shape[1]

    def variant(i):
        w0 = 8 * i - 8
        c0 = jnp.clip((w0 // 128) * 128, 0, r - 256)
        return ((w0 - c0 + 8) // 8, 0, 0)

    heads = pl.BlockSpec((NH, TQ, HD), lambda i: (0, i, 0))
    return pl.pallas_call(
        functools.partial(_cmp_attn_kernel, n_rows=r, n_blocks=ns),
        grid=(l // TQ,),
        in_specs=[heads, _full((r, HD)), _full((r, HD)), _full((r, ns)),
                  pl.BlockSpec((1, NH * TQ, 256), variant),
                  pl.BlockSpec((TQ, 128), lambda i: (i, 0))],
        out_specs=[heads, pl.BlockSpec((TQ, ns), lambda i: (i, 0))],
        out_shape=[jax.ShapeDtypeStruct((NH, l, HD), F32), jax.ShapeDtypeStruct((l, ns), BF16)],
        compiler_params=_cparams(("arbitrary",)),
        name="nsa_cmp_attn",
    )(qh, kc, vc, ovl, tab, misc)


def _flash_step(carry, s, v):
    m, l, acc = carry
    m_new = jnp.maximum(m, jnp.max(s, axis=1, keepdims=True))
    alpha = jnp.exp(m - m_new)
    p = jnp.exp(s - m_new)
    l = alpha * l + jnp.sum(p, axis=1, keepdims=True)
    acc = alpha * acc + _dot(p.astype(BF16), v)
    return m_new, l, acc


def _flash_init(rows, width):
    return (jnp.full((rows, 1), NEG, F32), jnp.zeros((rows, 1), F32), jnp.zeros((rows, width), F32))


def _nsa_attn_kernel(q_ref, selb_ref, ksel_ref, kwin_ref, vcat_ref, tnear_ref, twin_ref, misc_ref, oc_ref,
                     ya_ref, qx_ref, *, n_super):
    i = pl.program_id(0)
    q0 = i * TQ
    for jj in range(n_super):
        sb = selb_ref[:, jj * 64:(jj + 1) * 64]
        for hh in range(NH):
            qx_ref[jj, hh * TQ:(hh + 1) * TQ, :] = jnp.concatenate([q_ref[hh], sb], axis=1)

    ks = jnp.maximum(q0 - TQ, 0)
    n_full = ks // TK

    def chunk(j):
        k0 = pl.multiple_of(j * TK, TK)
        s = _dot_nt(qx_ref[j // (SUPER // TK)], ksel_ref[pl.ds(k0, TK), :])
        return s, vcat_ref[pl.ds(WINDOW + k0, TK), :]

    def sel_body(j, carry):
        s, v = chunk(j)
        return _flash_step(carry, s, v)

    carry = lax.fori_loop(0, n_full, sel_body, _flash_init(NH * TQ, 128))
    s, v = chunk(n_full)
    col = n_full * TK + lax.broadcasted_iota(jnp.int32, (1, TK), 1)
    carry = _flash_step(carry, s + jnp.where(col < ks, 0.0, NEG), v)
    parts = []
    for half in range(2):
        kh = pl.multiple_of(ks + half * TQ, TQ)
        parts.append(_dot_nt(qx_ref[kh // SUPER], ksel_ref[pl.ds(kh, TQ), :]))
    s = jnp.concatenate(parts, axis=1) + tnear_ref[jnp.minimum(i, 1)]
    _, l_s, acc_s = _flash_step(carry, s, vcat_ref[pl.ds(pl.multiple_of(WINDOW + ks, TQ), 2 * TQ), :])

    w0 = pl.multiple_of(q0, TQ)
    wpos = q0 - WINDOW + lax.broadcasted_iota(jnp.int32, (1, WINDOW + TQ), 1)
    s_w = _dot_nt(qx_ref[0], kwin_ref[pl.ds(w0, WINDOW + TQ), :]) + twin_ref[...] + jnp.where(wpos < 0, NEG, 0.0)
    p_w = jnp.exp(s_w - jnp.max(s_w, axis=1, keepdims=True))
    l_w = jnp.sum(p_w, axis=1, keepdims=True)
    acc_w = _dot(p_w.astype(BF16), vcat_ref[pl.ds(w0, WINDOW + TQ), :])

    o_s = acc_s[:, 0:HD] / l_s
    o_w = acc_w[:, HD:2 * HD] / l_w
    gates = misc_ref[...]
    for hh in range(NH):
        rows = slice(hh * TQ, (hh + 1) * TQ)
        y = oc_ref[hh] + gates[:, NH + hh:NH + hh + 1] * o_s[rows] + gates[:, 2 * NH + hh:2 * NH + hh + 1] * o_w[rows]
        ya_ref[hh] = y.astype(BF16)


def _nsa_attn(qh, selb, ksel, kwin, vcat, tnear, twin, misc, oc):
    l = qh.shape[1]
    ns = selb.shape[1]
    n_super = ns // 64
    heads = pl.BlockSpec((NH, TQ, HD), lambda i: (0, i, 0))
    return pl.pallas_call(
        functools.partial(_nsa_attn_kernel, n_super=n_super),
        grid=(l // TQ,),
        in_specs=[heads, pl.BlockSpec((TQ, ns), lambda i: (i, 0)),
                  _full((l, 128)), _full((l + WINDOW, 128)), _full((l + WINDOW, 128)),
                  _full((2, NH * TQ, 2 * TQ)), _full((NH * TQ, WINDOW + TQ)),
                  pl.BlockSpec((TQ, 128), lambda i: (i, 0)), heads],
        out_specs=heads,
        out_shape=jax.ShapeDtypeStruct((NH, l, HD), BF16),
        scratch_shapes=[pltpu.VMEM((n_super, NH * TQ, 128), BF16)],
        compiler_params=_cparams(("arbitrary",)),
        name="nsa_attn",
    )(qh, selb, ksel, kwin, vcat, tnear, twin, misc, oc)


def _fox_attn_kernel(q_ref, k_ref, v_ref, o_ref):
    i = pl.program_id(1)
    q0 = i * TF
    n_full = q0 // TKF
    outs = []
    for hh in range(2):
        q = q_ref[hh]

        def tile(j, hh=hh, q=q):
            k0 = pl.multiple_of(j * TKF, TKF)
            return _dot_nt(q, k_ref[hh, pl.ds(k0, TKF), :]), v_ref[pl.ds(k0, TKF), :]

        def body(j, carry, tile=tile):
            s, v = tile(j)
            return _flash_step(carry, s, v)

        carry = lax.fori_loop(0, n_full, body, _flash_init(TF, 128))
        s, v = tile(n_full)
        key = n_full * TKF + lax.broadcasted_iota(jnp.int32, s.shape, 1)
        row = q0 + lax.broadcasted_iota(jnp.int32, s.shape, 0)
        _, l, acc = _flash_step(carry, jnp.where(key <= row, s, NEG), v)
        outs.append(acc / l)
    lane = lax.broadcasted_iota(jnp.int32, outs[0].shape, 1)
    o_ref[...] = jnp.where(lane < HD, outs[0], outs[1]).astype(BF16)


def _fox_attn(fqx, fkx, fv):
    l = fqx.shape[1]
    return pl.pallas_call(
        _fox_attn_kernel,
        grid=(NH // 2, l // TF),
        in_specs=[pl.BlockSpec((2, TF, 128), lambda p, i: (p, i, 0)),
                  pl.BlockSpec((2, l, 128), lambda p, i: (p, 0, 0), pipeline_mode=pl.Buffered(1)),
                  pl.BlockSpec((l, 128), lambda p, i: (0, p), pipeline_mode=pl.Buffered(1))],
        out_specs=pl.BlockSpec((TF, 128), lambda p, i: (i, p)),
        out_shape=jax.ShapeDtypeStruct((l, MIX), BF16),
        compiler_params=_cparams(("arbitrary", "arbitrary")),
        name="fox_attn",
    )(fqx, fkx, fv)


def _layernorm_silu(y, ln_ref):
    mu = jnp.mean(y, axis=-1, keepdims=True)
    d = y - mu
    var = jnp.mean(d * d, axis=-1, keepdims=True)
    z = d * lax.rsqrt(var + EPS) * ln_ref[0] + ln_ref[1]
    return z * jax.nn.sigmoid(z)


def _pool_select(sums, counts, u):
    lane = lax.broadcasted_iota(jnp.int32, u.shape, 1)
    group = MIX // len(POOL_WINDOWS)
    out = sums[-1] / counts[-1]
    for g in range(len(POOL_WINDOWS) - 2, -1, -1):
        out = jnp.where(lane < (g + 1) * group, sums[g] / counts[g], out)
    return out - u


def _convpool_kernel(u_ref, uh_ref, up_ref, c_ref, ch_ref, cp_ref, cw_ref, cb_ref, ln_ref, pw_ref, ps_ref,
                     yb_ref, yc_ref, ext_ref, *, pos0):
    i = pl.program_id(0)
    tm = u_ref.shape[0]
    ext_ref[pl.ds(0, HALO), :] = jnp.where(i == 0, up_ref[...], uh_ref[...])
    ext_ref[pl.ds(HALO, tm), :] = u_ref[...]
    acc = jnp.zeros((tm, MIX), F32) + cb_ref[...]
    for w in range(CONV_WIDTH):
        acc = acc + ext_ref[pl.ds(HALO - CONV_HIST + w, tm), :] * cw_ref[pl.ds(w, 1), :]
    yb_ref[...] = _layernorm_silu(acc, ln_ref).astype(BF16)
    c = c_ref[...]
    ext_ref[pl.ds(0, HALO), :] = jnp.where(i == 0, cp_ref[...], ch_ref[...])
    ext_ref[pl.ds(HALO, tm), :] = c
    pos = pos0 + i * tm + lax.broadcasted_iota(jnp.int32, (tm, 1), 0)
    run = c
    sums, counts = [], []
    for k in range(1, max(POOL_WINDOWS)):
        run = run + ext_ref[pl.ds(HALO - k, tm), :]
        if k + 1 in POOL_WINDOWS:
            sums.append(run)
            counts.append(jnp.minimum(k + 1, pos + 1).astype(F32))
    pooled = _pool_select(sums, counts, c).astype(BF16)
    yc_ref[...] = (_dot(pooled, pw_ref[...]) * ps_ref[...]).astype(BF16)


def _convpool(u, u_past, c, c_past, cw, cb, ln, pw_bd, ps, pos0):
    l = u.shape[0]
    tm = min(TM, l)
    nh = tm // HALO
    row = pl.BlockSpec((tm, MIX), lambda i: (i, 0))
    halo = pl.BlockSpec((HALO, MIX), lambda i: (jnp.maximum(i * nh - 1, 0), 0))
    return pl.pallas_call(
        functools.partial(_convpool_kernel, pos0=pos0),
        grid=(l // tm,),
        in_specs=[row, halo, _full((HALO, MIX)), row, halo, _full((HALO, MIX)),
                  _full((HALO, MIX)), _full((1, MIX)), _full((2, 1, MIX)), _full((MIX, MIX)), _full((1, MIX))],
        out_specs=[row, row],
        out_shape=[jax.ShapeDtypeStruct((l, MIX), BF16)] * 2,
        scratch_shapes=[pltpu.VMEM((tm + HALO, MIX), F32)],
        compiler_params=_cparams(("arbitrary",)),
        name="convpool",
    )(u, u, u_past, c, c, c_past, cw, cb, ln, pw_bd, ps)


def _convpool_step_kernel(u_ref, up_ref, c_ref, cp_ref, cw_ref, cb_ref, ln_ref, pw_ref, ps_ref, yb_ref, yc_ref,
                          *, pos0):
    u = u_ref[...]
    acc = cb_ref[...] + u * cw_ref[pl.ds(CONV_WIDTH - 1, 1), :]
    for w in range(CONV_HIST):
        acc = acc + up_ref[w] * cw_ref[pl.ds(w, 1), :]
    yb_ref[...] = _layernorm_silu(acc, ln_ref).astype(BF16)
    c = c_ref[...]
    run = c
    sums, counts = [], []
    for k in range(1, max(POOL_WINDOWS)):
        run = run + cp_ref[POOL_HIST - k]
        if k + 1 in POOL_WINDOWS:
            sums.append(run)
            counts.append(float(min(k + 1, pos0 + 1)))
    pooled = _pool_select(sums, counts, c).astype(BF16)
    yc_ref[...] = (_dot(pooled, pw_ref[...]) * ps_ref[...]).astype(BF16)


def _convpool_step(u, u_past, c, c_past, cw, cb, ln, pw_bd, ps, pos0):
    b = u.shape[0]
    return pl.pallas_call(
        functools.partial(_convpool_step_kernel, pos0=pos0),
        out_shape=[jax.ShapeDtypeStruct((b, MIX), BF16)] * 2,
        name="convpool_step",
    )(u, u_past, c, c_past, cw, cb, ln, pw_bd, ps)


def _nsa_paged_cmp_kernel(pt_ref, *refs, n_pages, past):
    pages = refs[:PG]
    (q_ref, pos_ref, w_ref, ovl_ref, bias_ref, misc_ref, oc_ref, idx_ref, rk_ref, rv_ref, scr) = refs[PG:]
    b = pl.program_id(0)
    p = pl.program_id(1)
    n_rows = past // CMP_STRIDE
    for g in range(PG):
        r0 = pl.multiple_of((p * PG + g) * 8, 8)
        for t in range(CMP_STRIDE):
            rows = pages[g][0, pl.ds(t, 8, stride=CMP_STRIDE), :]
            rk_ref[pl.ds(r0, 8), t * HD:(t + 1) * HD] = rows[:, 0:HD]
            rv_ref[pl.ds(r0, 8), t * HD:(t + 1) * HD] = rows[:, HD:2 * HD]

    @pl.when(p == n_pages // PG - 1)
    def _():
        kc = _compress(rk_ref[...], pos_ref, w_ref, 0, scr).astype(BF16)
        vc = _compress(rv_ref[...], pos_ref, w_ref, 1, scr).astype(BF16)
        q = q_ref[0]
        s = _dot_nt(q, kc) + bias_ref[...]
        m = jnp.max(s, axis=1, keepdims=True)
        e = jnp.exp(s - m)
        pr = (e / jnp.sum(e, axis=1, keepdims=True)).astype(BF16)
        o = _dot(pr, vc)
        gates = misc_ref[0]
        oc_ref[0] = o * gates[:, 0:1]
        impf = _dot(pr, ovl_ref[...])
        imp = jnp.sum(impf[0:NH], axis=0, keepdims=True)
        ns = imp.shape[1]
        j = lax.broadcasted_iota(jnp.int32, imp.shape, 1)
        qblk = past // SLC_BLOCK
        forced = (j == 0) | (j == qblk) | (j == qblk - 1)
        val = jnp.where(forced, BIG, jnp.where(j * SLC_BLOCK <= past, imp, -BIG))
        val = jnp.where(j <= qblk, val, -jnp.inf)
        lane = lax.broadcasted_iota(jnp.int32, (1, 128), 1)
        idx = jnp.zeros((1, 128), F32)
        for t, first in enumerate(_select_topk(val, min(N_SELECT, qblk + 1))):
            idx = jnp.where(lane == t, first, idx)
        idx_ref[0] = idx.astype(jnp.int32)


def _nsa_paged_cmp(page_table, cache, q8, pos, w, ovl, bias, misc8, past):
    bsz, n_pages = page_table.shape
    n_rows = past // CMP_STRIDE
    ns = ovl.shape[1]

    def page_spec(g):
        return pl.BlockSpec((1, PAGE, 128), lambda b, p, pt, _g=g: (pt[b, p * PG + _g], 0, 0))

    def const(shape):
        nd = len(shape)
        return pl.BlockSpec(shape, lambda b, p, pt, _nd=nd: (0,) * _nd)

    def per_b(shape):
        return pl.BlockSpec(shape, lambda b, p, pt: (b, 0, 0))

    grid_spec = pltpu.PrefetchScalarGridSpec(
        num_scalar_prefetch=1,
        grid=(bsz, n_pages // PG),
        in_specs=[page_spec(g) for g in range(PG)] + [
            per_b((1, 8, HD)), const((2, 2, 1, 1024)), const((2, 2, 1024, HD)), const((n_rows, ns)),
            const((8, n_rows)), per_b((1, 8, 128))],
        out_specs=[per_b((1, 8, HD)), per_b((1, 1, 128))],
        scratch_shapes=[pltpu.VMEM((n_rows, 1024), F32), pltpu.VMEM((n_rows, 1024), F32),
                        pltpu.VMEM((n_rows + 8, HD), F32)],
    )
    return pl.pallas_call(
        functools.partial(_nsa_paged_cmp_kernel, n_pages=n_pages, past=past),
        grid_spec=grid_spec,
        out_shape=[jax.ShapeDtypeStruct((bsz, 8, HD), F32), jax.ShapeDtypeStruct((bsz, 1, 128), jnp.int32)],
        compiler_params=_cparams(("arbitrary", "arbitrary")),
        name="nsa_paged_cmp",
    )(page_table, *([cache] * PG), q8, pos, w, ovl, bias, misc8)


def _nsa_paged_attn_kernel(idx_ref, *refs, past):
    blocks = refs[:N_SELECT]
    (q_ref, new_ref, win_ref, neww_ref, relt_ref, wbias_ref, misc_ref, oc_ref, ya_ref, kv_ref, wkv_ref) = refs[N_SELECT:]
    b = pl.program_id(0)
    n_sel = N_SELECT * SLC_BLOCK
    q = q_ref[0]
    for t in range(N_SELECT):
        kv_ref[pl.ds(t * SLC_BLOCK, SLC_BLOCK), :] = blocks[t][0, :, 128:256]
    kv_ref[pl.ds(n_sel, 128), :] = jnp.broadcast_to(new_ref[0][:, 128:256], (128, 128))
    lane = lax.broadcasted_iota(jnp.int32, (1, n_sel + 128), 1)
    pos = jnp.full((1, n_sel + 128), past, jnp.int32)
    for t in range(N_SELECT):
        in_blk = (lane >= t * SLC_BLOCK) & (lane < (t + 1) * SLC_BLOCK)
        pos = jnp.where(in_blk, idx_ref[b, N_SELECT + t] * SLC_BLOCK + lane - t * SLC_BLOCK, pos)
    hide = jnp.where(lane < n_sel, jnp.where(pos < past, 0.0, NEG), jnp.where(lane == n_sel, 0.0, NEG))
    bias = _t5_bias(relt_ref[...], past - pos)
    kv = kv_ref[...].astype(BF16)
    s_all = _dot_nt(q, kv) + bias + hide
    m = jnp.max(s_all, axis=1, keepdims=True)
    e = jnp.exp(s_all - m)
    pr = (e / jnp.sum(e, axis=1, keepdims=True)).astype(BF16)
    o_s = _dot(pr, kv)[:, HD:2 * HD]
    n_win = win_ref.shape[1]
    wkv_ref[pl.ds(0, n_win), :] = win_ref[0]
    wkv_ref[pl.ds(n_win, 128), :] = jnp.broadcast_to(neww_ref[0], (128, 128))
    wkv = wkv_ref[...].astype(BF16)
    s_w = _dot_nt(q, wkv) + wbias_ref[...]
    m = jnp.max(s_w, axis=1, keepdims=True)
    e = jnp.exp(s_w - m)
    pr = (e / jnp.sum(e, axis=1, keepdims=True)).astype(BF16)
    o_w = _dot(pr, wkv)[:, HD:2 * HD]
    gates = misc_ref[0]
    ya_ref[0] = (oc_ref[0] + gates[:, 1:2] * o_s + gates[:, 2:3] * o_w).astype(BF16)


def _nsa_paged_attn(sel, cache_half, qx8, nsa_new, win_state, win_new, relt8, wbias, misc8, oc, past):
    bsz = sel.shape[0]
    n_win = win_state.shape[1]

    def blk_spec(t):
        return pl.BlockSpec((1, SLC_BLOCK, 256), lambda b, ix, _t=t: (ix[b, _t], 0, 0))

    def const(shape):
        nd = len(shape)
        return pl.BlockSpec(shape, lambda b, ix, _nd=nd: (0,) * _nd)

    def per_b(shape):
        return pl.BlockSpec(shape, lambda b, ix: (b, 0, 0))

    n_sel = N_SELECT * SLC_BLOCK
    grid_spec = pltpu.PrefetchScalarGridSpec(
        num_scalar_prefetch=1,
        grid=(bsz,),
        in_specs=[blk_spec(t) for t in range(N_SELECT)] + [
            per_b((1, 8, 128)), per_b((1, 1, 256)), per_b((1, n_win, 128)), per_b((1, 1, 128)),
            const((8, N_BUCKETS)), const((8, n_win + 128)), per_b((1, 8, 128)), per_b((1, 8, HD))],
        out_specs=per_b((1, 8, HD)),
        scratch_shapes=[pltpu.VMEM((n_sel + 128, 128), F32), pltpu.VMEM((n_win + 128, 128), F32)],
    )
    return pl.pallas_call(
        functools.partial(_nsa_paged_attn_kernel, past=past),
        grid_spec=grid_spec,
        out_shape=jax.ShapeDtypeStruct((bsz, 8, HD), BF16),
        compiler_params=_cparams(("arbitrary",)),
        name="nsa_paged_attn",
    )(sel, *([cache_half] * N_SELECT), qx8, nsa_new, win_state, win_new, relt8, wbias, misc8, oc)


def _fox_paged_kernel(pt_ref, *refs, n_pages):
    kv_pages = refs[:PG]
    lf_pages = refs[PG:2 * PG]
    (qbd_ref, new_ref, lfnew_ref, o_ref, m_ref, l_ref, acc_ref, car_ref) = refs[2 * PG:]
    p = pl.program_id(1)
    qbd = qbd_ref[0]

    @pl.when(p == 0)
    def _():
        new = new_ref[0]
        kn = new[:, 0:MIX].astype(BF16).astype(F32)
        m_ref[...] = jnp.sum(qbd.astype(F32) * kn, axis=1, keepdims=True)
        l_ref[...] = jnp.ones((8, 1), F32)
        acc_ref[...] = jnp.broadcast_to(new[:, MIX:2 * MIX].astype(BF16).astype(F32), (8, MIX))
        car_ref[...] = lfnew_ref[0]

    later = jnp.where(lax.broadcasted_iota(jnp.int32, (PAGE, PAGE), 0) > lax.broadcasted_iota(jnp.int32, (PAGE, PAGE), 1),
                      1.0, 0.0).astype(BF16)
    carry = car_ref[...]
    s_parts = [None] * PG
    for g in range(PG - 1, -1, -1):
        lf = lf_pages[g][0]
        k = kv_pages[g][0, :, 0:MIX].astype(BF16)
        s_parts[g] = _dot_nt(qbd, k) + carry + _dot01(lf, later)
        carry = carry + jnp.sum(lf, axis=1, keepdims=True)
    car_ref[...] = carry
    s = jnp.concatenate(s_parts, axis=1)
    m_old = m_ref[...]
    m_new = jnp.maximum(m_old, jnp.max(s, axis=1, keepdims=True))
    alpha = jnp.exp(m_old - m_new)
    pr = jnp.exp(s - m_new)
    l_ref[...] = alpha * l_ref[...] + jnp.sum(pr, axis=1, keepdims=True)
    acc = alpha * acc_ref[...]
    prb = pr.astype(BF16)
    for g in range(PG):
        acc = acc + _dot(prb[:, g * PAGE:(g + 1) * PAGE], kv_pages[g][0, :, MIX:2 * MIX].astype(BF16))
    acc_ref[...] = acc
    m_ref[...] = m_new

    @pl.when(p == n_pages // PG - 1)
    def _():
        out = acc / l_ref[...]
        own = lax.broadcasted_iota(jnp.int32, out.shape, 1) // HD == lax.broadcasted_iota(jnp.int32, out.shape, 0)
        o_ref[0] = jnp.sum(jnp.where(own, out, 0.0), axis=0, keepdims=True).astype(BF16)


def _fox_paged(page_table, cache_kv, cache_lf, qbd, kv_new, lf_new):
    bsz, n_pages = page_table.shape
    nchunk = n_pages // PG

    def kv_spec(g):
        return pl.BlockSpec((1, PAGE, 2 * MIX), lambda b, p, pt, _g=g: (pt[b, (nchunk - 1 - p) * PG + _g], 0, 0))

    def lf_spec(g):
        return pl.BlockSpec((1, 8, PAGE), lambda b, p, pt, _g=g: (pt[b, (nchunk - 1 - p) * PG + _g], 0, 0))

    def per_b(shape):
        return pl.BlockSpec(shape, lambda b, p, pt: (b, 0, 0))

    grid_spec = pltpu.PrefetchScalarGridSpec(
        num_scalar_prefetch=1,
        grid=(bsz, nchunk),
        in_specs=[kv_spec(g) for g in range(PG)] + [lf_spec(g) for g in range(PG)] + [
            per_b((1, 8, MIX)), per_b((1, 1, 2 * MIX)), per_b((1, 8, 1))],
        out_specs=per_b((1, 1, MIX)),
        scratch_shapes=[pltpu.VMEM((8, 1), F32), pltpu.VMEM((8, 1), F32), pltpu.VMEM((8, MIX), F32),
                        pltpu.VMEM((8, 1), F32)],
    )
    return pl.pallas_call(
        functools.partial(_fox_paged_kernel, n_pages=n_pages),
        grid_spec=grid_spec,
        out_shape=jax.ShapeDtypeStruct((bsz, 1, MIX), BF16),
        compiler_params=_cparams(("arbitrary", "arbitrary")),
        name="fox_paged",
    )(page_table, *([cache_kv] * PG), *([cache_lf] * PG), qbd, kv_new, lf_new)


def _layer_params(l, norm_g, ffn_gate, ffn_up, ffn_down, w_in, fox_f_bias, nsa_cmp_pos, nsa_cmp_w,
                  conv_w, conv_b, conv_ln, pool_w, pool_scale, w_branch, w_out):
    wi = w_in[l]
    o_aq, o_akv, o_ag, o_b, o_c, o_d, o_f, o_m = np.cumsum([0, 256, 384, 12, 512, 256, 768, 4])
    misc = jnp.concatenate([wi[:, o_ag:o_ag + 12], wi[:, o_f:o_f + 4], jnp.zeros((D_MODEL, 112), F32)], axis=1)
    w_small = jnp.concatenate([wi[:, o_aq:o_aq + 256], wi[:, o_akv:o_akv + 384], misc, wi[:, o_b:o_b + 512],
                               wi[:, o_c:o_c + 256], wi[:, o_d:o_d + 768]], axis=1).astype(BF16)
    fb_row = jnp.zeros((1, 128), F32).at[0, 12:16].set(fox_f_bias[l])
    cw = nsa_cmp_w[l].reshape(2, 2, CMP_STRIDE * HD, HD).astype(BF16)
    cpos = nsa_cmp_pos[l].reshape(2, 2, 1, CMP_STRIDE * HD)
    group = MIX // len(POOL_WINDOWS)
    pw_bd = jnp.zeros((MIX, MIX), F32)
    for g in range(len(POOL_WINDOWS)):
        pw_bd = pw_bd.at[g * group:(g + 1) * group, g * group:(g + 1) * group].set(pool_w[l, g])
    cw_conv = jnp.concatenate([conv_w[l], jnp.zeros((HALO - CONV_WIDTH, MIX), F32)], axis=0)
    return dict(
        g=[norm_g[l, k].reshape(1, D_MODEL) for k in range(3)],
        ffn=[(ffn_gate[l, k].astype(BF16), ffn_up[l, k].astype(BF16), ffn_down[l, k].astype(BF16)) for k in range(2)],
        w_small=w_small, w_merge=wi[:, o_m:].astype(BF16), fb_row=fb_row, cmp_w=cw, cmp_pos=cpos,
        conv_w=cw_conv, conv_b=conv_b[l].reshape(1, MIX), conv_ln=conv_ln[l].reshape(2, 1, MIX),
        pool_w=pw_bd.astype(BF16), pool_scale=pool_scale[l].reshape(1, MIX),
        w_branch=w_branch[l].astype(BF16), w_out=w_out[l].astype(BF16))


def _overlap(n_rows, n_blocks, n_pad):
    cs = np.arange(n_rows)[:, None] * CMP_STRIDE
    j0 = np.arange(n_pad)[None, :] * SLC_BLOCK
    ov = (cs < j0 + SLC_BLOCK) & (cs + CMP_LEN > j0) & (np.arange(n_pad)[None, :] < n_blocks)
    return jnp.asarray(ov.astype(np.float32), dtype=BF16)


def _lookup(bt, dist, valid):
    d = np.clip(dist, 0, NDIST - 1)
    return jnp.where(jnp.asarray(valid)[None], bt[:, d], NEG)


def _toeplitz_kernel(rel_ref, o_ref, *, base, vstep, stride, hi):
    width = o_ref.shape[2]
    r = lax.broadcasted_iota(jnp.int32, (TQ, width), 0)
    m = lax.broadcasted_iota(jnp.int32, (TQ, width), 1)
    dist = base + vstep * pl.program_id(0) + r - stride * m
    bucket = _t5_bucket(dist)
    hide = jnp.where(dist < 0, NEG, jnp.where(dist > hi, NEG, 0.0))
    for hh in range(NH):
        far = rel_ref[N_BUCKETS - 1, hh]
        val = jnp.zeros((TQ, width), F32)
        for b in range(N_BUCKETS - 1):
            val = jnp.where(bucket == b, rel_ref[b, hh] - far, val)
        o_ref[0, hh * TQ:(hh + 1) * TQ, :] = val + hide


def _toeplitz(rel_bias, n_var, width, base, vstep, stride, hi):
    return pl.pallas_call(
        functools.partial(_toeplitz_kernel, base=base, vstep=vstep, stride=stride, hi=hi),
        grid=(n_var,),
        in_specs=[pl.BlockSpec(memory_space=pltpu.SMEM)],
        out_specs=pl.BlockSpec((1, NH * TQ, width), lambda v: (v, 0, 0)),
        out_shape=jax.ShapeDtypeStruct((n_var, NH * TQ, width), F32),
        compiler_params=_cparams(("arbitrary",)),
        name="t5_tables",
    )(rel_bias)


def _prompt_tables(rel_bias):
    big = 1 << 30
    tab_c = _toeplitz(rel_bias, 33, 256, -(CMP_LEN - 1), TQ, CMP_STRIDE, big)
    tnear = _toeplitz(rel_bias, 2, 2 * TQ, 0, TQ, 1, big)
    twin = _toeplitz(rel_bias, 1, WINDOW + TQ, WINDOW, 0, 1, WINDOW)[0]
    return tab_c, tnear, twin


def _prompt_layer(x, lp, tables, fg, final):
    l = x.shape[0]
    tab_c, tnear, twin = tables
    x = _ffn(x, lp["g"][0], *lp["ffn"][0], fg)
    (qn, nsa_kv, win_kv, misc, u, cin, fq, fox_kv, ksel, kwin, vcat, fk, fv) = _inproj(x, lp["g"][1], lp["w_small"], lp["fb_row"])
    n_rows = l // CMP_STRIDE
    ns = l // SLC_BLOCK
    rk = nsa_kv[:, 0:HD].reshape(n_rows, CMP_STRIDE * HD)
    rv = nsa_kv[:, HD:2 * HD].reshape(n_rows, CMP_STRIDE * HD)
    kc, vc = _compress_prompt(rk, rv, lp["cmp_pos"], lp["cmp_w"])
    oc, selb = _cmp_attn(qn, kc, vc, _overlap(n_rows, ns, ns), tab_c, misc)
    front = jnp.zeros((WINDOW, 128), BF16)
    ya = _nsa_attn(qn, selb, ksel, jnp.concatenate([front, kwin]), jnp.concatenate([front, vcat]), tnear, twin, misc, oc)
    zeros = jnp.zeros((HALO, MIX), F32)
    yb, yc = _convpool(u, zeros, cin, zeros, lp["conv_w"], lp["conv_b"], lp["conv_ln"], lp["pool_w"], lp["pool_scale"], 0)
    logf = misc[:, 12:16]
    yd = _fox_attn(fq, fk, fv)
    x = _merge(x, lp["g"][1], lp["w_merge"], lp["w_branch"], lp["w_out"], ya, yb, yc, yd)
    x = _ffn(x, lp["g"][2], *lp["ffn"][1], fg, final)
    state = (nsa_kv, fox_kv, logf, win_kv[l - min(WINDOW, l):], u[l - CONV_HIST:], cin[l - POOL_HIST:])
    return x, state


def _sample_layer(x, lp, relt8, bt, fg, final, page_table, cache_nsa, cache_fox_kv, cache_fox_logf, st_win, st_conv,
                  st_pool):
    bsz = x.shape[0]
    n_pages = page_table.shape[1]
    past = n_pages * PAGE
    n_phys = cache_nsa.shape[0]
    x = _ffn(x, lp["g"][0], *lp["ffn"][0], fg)
    (qn, nsa_kv, win_kv, misc, u, cin, fq, fox_kv, _, _, _, _, _) = _inproj(x, lp["g"][1], lp["w_small"], lp["fb_row"])
    pad4 = lambda a: jnp.concatenate([a, jnp.zeros((bsz, 8 - NH) + a.shape[2:], a.dtype)], axis=1)
    q8 = pad4(jnp.transpose(qn, (1, 0, 2)))
    qx8 = jnp.concatenate([q8, jnp.zeros_like(q8)], axis=2)
    gates = misc[:, 0:3 * NH].reshape(bsz, 3, NH)
    misc8 = jnp.zeros((bsz, 8, 128), F32).at[:, 0:NH, 0:3].set(jnp.transpose(gates, (0, 2, 1)))
    n_rows = past // CMP_STRIDE
    ns = past // SLC_BLOCK + 1
    ns_pad = -(-ns // 128) * 128
    c_end = np.arange(n_rows) * CMP_STRIDE + CMP_LEN - 1
    valid_c = (c_end <= past) & (np.arange(n_rows) < n_rows - 1)
    bias_c = jnp.concatenate([_lookup(bt, past - c_end, valid_c), jnp.full((8 - NH, n_rows), NEG, F32)], axis=0)
    oc, idx = _nsa_paged_cmp(page_table, cache_nsa.reshape(n_phys, PAGE, 4 * HD), q8, lp["cmp_pos"], lp["cmp_w"],
                             _overlap(n_rows, ns, ns_pad), bias_c, misc8, past)
    n_win = st_win.shape[1]
    dw = np.concatenate([n_win - np.arange(n_win), np.zeros(128, np.int64)])
    vw = np.concatenate([np.ones(n_win + 1, bool), np.zeros(127, bool)])
    wbias = jnp.concatenate([_lookup(bt, dw, vw), jnp.full((8 - NH, n_win + 128), NEG, F32)], axis=0)
    blk = idx.reshape(bsz, 128)[:, 0:N_SELECT]
    held = jnp.minimum(blk, past // SLC_BLOCK - 1)
    rows = jnp.take_along_axis(page_table, held // 2, axis=1) * 2 + held % 2
    ya = _nsa_paged_attn(jnp.concatenate([rows, blk], axis=1), cache_nsa.reshape(n_phys * 2, SLC_BLOCK, 4 * HD), qx8,
                         nsa_kv.reshape(bsz, 1, 256), st_win.reshape(bsz, n_win, 128), win_kv.reshape(bsz, 1, 128),
                         relt8, wbias, misc8, oc, past)
    ya = jnp.transpose(ya[:, 0:NH], (1, 0, 2))
    yb, yc = _convpool_step(u, jnp.transpose(st_conv, (1, 0, 2)), cin, jnp.transpose(st_pool, (1, 0, 2)),
                            lp["conv_w"], lp["conv_b"], lp["conv_ln"], lp["pool_w"], lp["pool_scale"], past)
    logf = misc[:, 12:16]
    fq8 = pad4(jnp.transpose(fq[:, :, 0:HD], (1, 0, 2)))
    own = (np.arange(MIX)[None, :] // HD == np.arange(8)[:, None])
    qbd = jnp.where(jnp.asarray(own)[None], jnp.tile(fq8, (1, 1, NH)), jnp.zeros((), BF16))
    lf_pages = jnp.concatenate([jnp.transpose(cache_fox_logf, (0, 2, 1)), jnp.zeros((n_phys, 8 - NH, PAGE), F32)], axis=1)
    lf_new = pad4(logf.reshape(bsz, NH, 1))
    yd = _fox_paged(page_table, cache_fox_kv.reshape(n_phys, PAGE, 2 * MIX), lf_pages, qbd,
                    fox_kv.reshape(bsz, 1, 2 * MIX), lf_new)
    yd = yd.reshape(bsz, MIX)
    x = _merge(x, lp["g"][1], lp["w_merge"], lp["w_branch"], lp["w_out"], ya, yb, yc, yd)
    x = _ffn(x, lp["g"][2], *lp["ffn"][1], fg, final)
    new_win =jnp.concatenate([st_win.reshape(bsz, n_win, 128)[:, 1:], win_kv[:, None, :]], axis=1)
    new_conv = jnp.concatenate([st_conv[:, 1:], u[:, None, :]], axis=1)
    new_pool = jnp.concatenate([st_pool[:, 1:], cin[:, None, :]], axis=1)
    state = (nsa_kv, fox_kv, logf, new_win, new_conv, new_pool)
    return x, state


def kernel(x_prompt, x_sample, cache_nsa, cache_fox_kv, cache_fox_logf, state_nsa_win, state_conv, state_pool,
           page_table, norm_g, ffn_gate, ffn_up, ffn_down, w_in, fox_f_bias, nsa_cmp_pos, nsa_cmp_w, rel_bias,
           conv_w, conv_b, conv_ln, pool_w, pool_scale, w_branch, w_out, final_norm_g):
    assert x_prompt.shape[0] == 1 and x_sample.shape[1] == 1
    depth = norm_g.shape[0]
    l = x_prompt.shape[1]
    bsz = x_sample.shape[0]
    win_keep = state_nsa_win.shape[2]
    assert l % SUPER == 0 and page_table.shape[1] * PAGE >= max(WINDOW, SUPER) and win_keep == WINDOW
    relt8 = jnp.concatenate([rel_bias.T, jnp.zeros((8 - NH, N_BUCKETS), F32)], axis=0)
    bt = _bias_table(relt8)[0:NH]
    tables = _prompt_tables(rel_bias)
    fg = final_norm_g.reshape(1, D_MODEL)
    xp = x_prompt.reshape(l, D_MODEL)
    xs = x_sample.reshape(bsz, D_MODEL)
    st_p, st_s = [], []
    for layer in range(depth):
        lp = _layer_params(layer, norm_g, ffn_gate, ffn_up, ffn_down, w_in, fox_f_bias, nsa_cmp_pos, nsa_cmp_w,
                           conv_w, conv_b, conv_ln, pool_w, pool_scale, w_branch, w_out)
        final = layer == depth - 1
        xp, sp = _prompt_layer(xp, lp, tables, fg, final)
        st_p.append(sp)
        xs, ss = _sample_layer(xs, lp, relt8, bt, fg, final, page_table, cache_nsa[layer], cache_fox_kv[layer],
                               cache_fox_logf[layer], state_nsa_win[layer], state_conv[layer], state_pool[layer])
        st_s.append(ss)

    def stack(states, k, shape):
        return jnp.stack([s[k] for s in states]).reshape(shape)

    return (xp.reshape(1, l, D_MODEL), xs.reshape(bsz, 1, D_MODEL),
            stack(st_p, 0, (depth, 1, l, 4, 1, HD)), stack(st_s, 0, (depth, bsz, 1, 4, 1, HD)),
            stack(st_p, 1, (depth, 1, l, 2, NH, HD)), stack(st_s, 1, (depth, bsz, 1, 2, NH, HD)),
            stack(st_p, 2, (depth, 1, l, NH)), stack(st_s, 2, (depth, bsz, 1, NH)),
            stack(st_p, 3, (depth, 1, win_keep, 2, 1, HD)), stack(st_s, 3, (depth, bsz, win_keep, 2, 1, HD)),
            stack(st_p, 4, (depth, 1, CONV_HIST, MIX)), stack(st_s, 4, (depth, bsz, CONV_HIST, MIX)),
            stack(st_p, 5, (depth, 1, POOL_HIST, MIX)), stack(st_s, 5, (depth, bsz, POOL_HIST, MIX)))
```

```python
import functools
import math

import numpy as np
import jax
import jax.numpy as jnp
from jax import lax
from jax.experimental import pallas as pl
from jax.experimental.pallas import tpu as pltpu

F32 = jnp.float32
BF16 = jnp.bfloat16

D_MODEL = 1024
N_BRANCH = 4
MIX = D_MODEL // 4
HD = 64
NH = MIX // HD
CMP_STRIDE = 16
CMP_LEN = 32
SLC_BLOCK = 64
N_SELECT = 16
WINDOW = 512
N_BUCKETS = 32
REL_EXACT = 16
REL_MAX_DIST = 128
CONV_WIDTH = 31
CONV_HIST = CONV_WIDTH - 1
POOL_WINDOWS = (2, 4, 8, 16)
POOL_HIST = 15
D_FF = 2816
EPS = 1e-6
NEG = -1e30
BIG = 1e6
PAGE = 128

LANE = 128
TQ = 128
TK = 1024
SUPER = 64 * SLC_BLOCK
TF = 256
TKF = 1024
TM = 256
FF_CHUNK = 1408
HALO = 32
NDIST = 1024
PG = 8
VMEM_LIMIT = 56 * 1024 * 1024

C_AQ, C_NSA, C_WIN, C_MISC, C_CONV, C_POOL, C_FOX = 0, 256, 512, 640, 768, 1280, 1536
N_SMALL = 2304


def _cparams(sem):
    return pltpu.CompilerParams(dimension_semantics=sem, vmem_limit_bytes=VMEM_LIMIT)


def _full(shape, single=True):
    nd = len(shape)
    kw = dict(pipeline_mode=pl.Buffered(1)) if single else {}
    return pl.BlockSpec(shape, lambda *a, _nd=nd: (0,) * _nd, **kw)


def _rms(x, g):
    ms = jnp.mean(x * x, axis=-1, keepdims=True)
    return x * lax.rsqrt(ms + EPS) * g


def _dot(a, b):
    return jnp.dot(a, b, preferred_element_type=F32)


def _dot_nt(a, b):
    return lax.dot_general(a, b, (((1,), (1,)), ((), ())), preferred_element_type=F32)


def _split3(x):
    hi = x.astype(BF16)
    r1 = x - hi.astype(F32)
    mid = r1.astype(BF16)
    lo = (r1 - mid.astype(F32)).astype(BF16)
    return hi, mid, lo


def _dot01(x, ones_bf16):
    hi, mid, lo = _split3(x)
    return _dot(hi, ones_bf16) + _dot(mid, ones_bf16) + _dot(lo, ones_bf16)


def _dot01_l(ones_bf16, x):
    hi, mid, lo = _split3(x)
    return _dot(ones_bf16, hi) + _dot(ones_bf16, mid) + _dot(ones_bf16, lo)


def _log_sigmoid(x):
    return jnp.minimum(x, 0.0) - jnp.log(1.0 + jnp.exp(-jnp.abs(x)))


def _t5_bucket(dist):
    n = jnp.maximum(dist, 0)
    ratio = jnp.log(jnp.maximum(n, 1).astype(F32) / REL_EXACT) / math.log(REL_MAX_DIST / REL_EXACT)
    large = REL_EXACT + (ratio * (N_BUCKETS - REL_EXACT)).astype(jnp.int32)
    return jnp.where(n < REL_EXACT, n, jnp.minimum(large, N_BUCKETS - 1))


def _t5_bias(relt, dist):
    bucket = _t5_bucket(dist)
    rows = lax.broadcasted_iota(jnp.int32, (N_BUCKETS, dist.shape[1]), 0)
    onehot = jnp.where(rows == bucket, 1.0, 0.0).astype(BF16)
    return _dot01(relt, onehot)


def _ffn_kernel(x_ref, g_ref, wg_ref, wu_ref, wd_ref, fg_ref, o_ref, *, final):
    x = x_ref[...]
    h = _rms(x, g_ref[...]).astype(BF16)
    acc = jnp.zeros_like(x)
    for c in range(D_FF // FF_CHUNK):
        sl = slice(c * FF_CHUNK, (c + 1) * FF_CHUNK)
        a = _dot(h, wg_ref[:, sl])
        b = _dot(h, wu_ref[:, sl])
        t = (a * jax.nn.sigmoid(a) * b).astype(BF16)
        acc = acc + _dot(t, wd_ref[sl, :])
    out = x + 0.5 * acc
    o_ref[...] = _rms(out, fg_ref[...]) if final else out


def _ffn(x, g, wg, wu, wd, fg, final=False):
    m = x.shape[0]
    tm = min(TM, m)
    row = pl.BlockSpec((tm, D_MODEL), lambda i: (i, 0))
    return pl.pallas_call(
        functools.partial(_ffn_kernel, final=final),
        grid=(m // tm,),
        in_specs=[row, _full((1, D_MODEL)), _full((D_MODEL, D_FF)), _full((D_MODEL, D_FF)),
                  _full((D_FF, D_MODEL)), _full((1, D_MODEL))],
        out_specs=row,
        out_shape=jax.ShapeDtypeStruct((m, D_MODEL), F32),
        compiler_params=_cparams(("arbitrary",)),
        name="ffn",
    )(x, g, wg, wu, wd, fg)


def _inproj_kernel(x_ref, g_ref, w_ref, fb_ref,
                   qn_ref, nsa_ref, win_ref, misc_ref, u_ref, cin_ref, fq_ref, fkv_ref,
                   ksel_ref, kwin_ref, vcat_ref, fk_ref, fv_ref, csum_ref):
    tm = x_ref.shape[0]
    h = _rms(x_ref[...], g_ref[...]).astype(BF16)
    z = _dot(h, w_ref[...])
    scale = HD ** -0.5
    zm = z[:, C_MISC:C_MISC + 128]
    lane = lax.broadcasted_iota(jnp.int32, zm.shape, 1)
    logf = _log_sigmoid(zm + fb_ref[...])
    misc_ref[...] = jnp.where(lane < 3 * NH, jax.nn.sigmoid(zm), logf)
    @pl.when(pl.program_id(0) == 0)
    def _():
        csum_ref[...] = jnp.zeros_like(csum_ref)
    tri = jnp.where(lax.broadcasted_iota(jnp.int32, (tm, tm), 1) <= lax.broadcasted_iota(jnp.int32, (tm, tm), 0),
                    1.0, 0.0).astype(BF16)
    lf = jnp.where((lane >= 3 * NH) & (lane < 4 * NH), logf, 0.0)
    csum = _dot01_l(tri, lf) + csum_ref[...]
    csum_ref[...] = csum[tm - 1:tm, :]
    feat = lax.broadcasted_iota(jnp.int32, (tm, HD), 1)
    ones3 = jnp.where(feat < 3, 1.0, 0.0).astype(BF16)
    for hh in range(NH):
        qn_ref[hh] = (z[:, C_AQ + HD * hh:C_AQ + HD * (hh + 1)] * scale).astype(BF16)
        fq = (z[:, C_FOX + HD * hh:C_FOX + HD * (hh + 1)] * scale).astype(BF16)
        fq_ref[hh] = jnp.concatenate([fq, ones3], axis=1)
        hi, mid, lo = (p.astype(F32) for p in _split3(-csum[:, 3 * NH + hh:3 * NH + hh + 1]))
        extra = jnp.where(feat == 0, hi, jnp.where(feat == 1, mid, jnp.where(feat == 2, lo, 0.0)))
        fk = z[:, C_FOX + MIX + HD * hh:C_FOX + MIX + HD * (hh + 1)].astype(BF16)
        fk_ref[hh] = jnp.concatenate([fk, extra.astype(BF16)], axis=1)
    fv_ref[...] = z[:, C_FOX + 2 * MIX:C_FOX + 3 * MIX].astype(BF16)
    nsa = z[:, C_NSA:C_NSA + 256]
    nsa_ref[...] = nsa
    win = z[:, C_WIN:C_WIN + 128]
    win_ref[...] = win
    u_ref[...] = z[:, C_CONV:C_CONV + MIX] * jax.nn.sigmoid(z[:, C_CONV + MIX:C_CONV + 2 * MIX])
    cin_ref[...] = z[:, C_POOL:C_POOL + MIX]
    fkv_ref[...] = z[:, C_FOX + MIX:C_FOX + 3 * MIX]
    t = pl.program_id(0) * tm + lax.broadcasted_iota(jnp.int32, (tm, HD), 0)
    feat = lax.broadcasted_iota(jnp.int32, (tm, HD), 1)
    onehot = jnp.where(((t // SLC_BLOCK) % 64) == feat, 1.0, 0.0).astype(BF16)
    ksel_ref[...] = jnp.concatenate([nsa[:, 128:192].astype(BF16), onehot], axis=1)
    kwin_ref[...] = jnp.concatenate([win[:, 0:64].astype(BF16), jnp.zeros((tm, HD), BF16)], axis=1)
    vcat_ref[...] = jnp.concatenate([nsa[:, 192:256].astype(BF16), win[:, 64:128].astype(BF16)], axis=1)


def _inproj(x, g, w_small, fbias_row):
    m = x.shape[0]
    tm = min(TM, m)

    def row(n):
        return pl.BlockSpec((tm, n), lambda i: (i, 0))

    heads = pl.BlockSpec((NH, tm, HD), lambda i: (0, i, 0))
    wide = pl.BlockSpec((NH, tm, 128), lambda i: (0, i, 0))
    shapes = [((NH, m, HD), BF16, heads), ((m, 256), F32, row(256)), ((m, 128), F32, row(128)),
              ((m, 128), F32, row(128)), ((m, MIX), F32, row(MIX)), ((m, MIX), F32, row(MIX)),
              ((NH, m, 128), BF16, wide), ((m, 2 * MIX), F32, row(2 * MIX)),
              ((m, 128), BF16, row(128)), ((m, 128), BF16, row(128)), ((m, 128), BF16, row(128)),
              ((NH, m, 128), BF16, wide), ((m, MIX), BF16, row(MIX))]
    return pl.pallas_call(
        _inproj_kernel,
        grid=(m // tm,),
        in_specs=[row(D_MODEL), _full((1, D_MODEL)), _full((D_MODEL, N_SMALL)), _full((1, 128))],
        out_specs=[s[2] for s in shapes],
        out_shape=[jax.ShapeDtypeStruct(s[0], s[1]) for s in shapes],
        scratch_shapes=[pltpu.VMEM((1, 128), F32)],
        compiler_params=_cparams(("arbitrary",)),
        name="inproj",
    )(x, g, w_small, fbias_row)


def _merge_kernel(x_ref, g_ref, wm_ref, wb_ref, wo_ref, ya_ref, yb_ref, yc_ref, yd_ref, o_ref):
    x = x_ref[...]
    h = _rms(x, g_ref[...]).astype(BF16)
    mix = jnp.zeros_like(x)
    for n in range(N_BRANCH):
        gate = jax.nn.sigmoid(_dot(h, wm_ref[:, n * D_MODEL:(n + 1) * D_MODEL]))
        if n == 0:
            proj = jnp.zeros_like(x)
            for hh in range(NH):
                proj = proj + _dot(ya_ref[hh], wb_ref[n, hh * HD:(hh + 1) * HD, :])
        else:
            proj = _dot((yb_ref, yc_ref, yd_ref)[n - 1][...], wb_ref[n])
        mix = mix + gate * proj
    o_ref[...] = x + _dot(mix.astype(BF16), wo_ref[...])


def _merge(x, g, w_merge, w_branch, w_out, ya, yb, yc, yd):
    m = x.shape[0]
    tm = min(TM, m)
    row = pl.BlockSpec((tm, D_MODEL), lambda i: (i, 0))
    heads = pl.BlockSpec((NH, tm, HD), lambda i: (0, i, 0))
    mixrow = pl.BlockSpec((tm, MIX), lambda i: (i, 0))
    return pl.pallas_call(
        _merge_kernel,
        grid=(m // tm,),
        in_specs=[row, _full((1, D_MODEL)), _full((D_MODEL, N_BRANCH * D_MODEL)),
                  _full((N_BRANCH, MIX, D_MODEL)), _full((D_MODEL, D_MODEL)),
                  heads, mixrow, mixrow, mixrow],
        out_specs=row,
        out_shape=jax.ShapeDtypeStruct((m, D_MODEL), F32),
        compiler_params=_cparams(("arbitrary",)),
        name="merge",
    )(x, g, w_merge, w_branch, w_out, ya, yb, yc, yd)


def _bias_table_kernel(relt_ref, o_ref):
    dist = lax.broadcasted_iota(jnp.int32, (1, NDIST), 1)
    o_ref[...] = _t5_bias(relt_ref[...], dist)


def _bias_table(relt8):
    return pl.pallas_call(
        _bias_table_kernel,
        out_shape=jax.ShapeDtypeStruct((8, NDIST), F32),
        name="bias_table",
    )(relt8)


def _compress(r, pos_ref, w_ref, t, scr):
    n = r.shape[0]
    a = _dot((r + pos_ref[t, 0]).astype(BF16), w_ref[t, 0])
    b = _dot((r + pos_ref[t, 1]).astype(BF16), w_ref[t, 1])
    scr[pl.ds(0, n), :] = b
    scr[pl.ds(n, 8), :] = jnp.zeros((8, HD), F32)
    return a + scr[pl.ds(1, n), :]


def _compress_kernel(rk_ref, rv_ref, pos_ref, w_ref, kc_ref, vc_ref, scr):
    kc_ref[...] = _compress(rk_ref[...], pos_ref, w_ref, 0, scr).astype(BF16)
    vc_ref[...] = _compress(rv_ref[...], pos_ref, w_ref, 1, scr).astype(BF16)


def _compress_prompt(rk, rv, pos, w):
    r = rk.shape[0]
    return pl.pallas_call(
        _compress_kernel,
        in_specs=[_full((r, 1024), False), _full((r, 1024), False), _full((2, 2, 1, 1024), False),
                  _full((2, 2, 1024, HD), False)],
        out_specs=[_full((r, HD), False)] * 2,
        grid=(1,),
        out_shape=[jax.ShapeDtypeStruct((r, HD), BF16)] * 2,
        scratch_shapes=[pltpu.VMEM((r + 8, HD), F32)],
        compiler_params=_cparams(("arbitrary",)),
        name="nsa_compress",
    )(rk, rv, pos, w)


def _select_topk(val, n_sel):
    ns = val.shape[1]
    j = lax.broadcasted_iota(jnp.int32, val.shape, 1).astype(F32)
    picks = []
    for _ in range(n_sel):
        mx = jnp.max(val, axis=1, keepdims=True)
        first = jnp.min(jnp.where(val == mx, j, float(ns)), axis=1, keepdims=True)
        picks.append(first)
        val = jnp.where(j == first, -jnp.inf, val)
    return picks


def _cmp_attn_kernel(q_ref, kc_ref, vc_ref, ovl_ref, tab_ref, misc_ref, oc_ref, selb_ref,
                     *, n_rows, n_blocks):
    i = pl.program_id(0)
    q = q_ref[...].reshape(NH * TQ, HD)
    w0 = 8 * i - 8
    c0 = jnp.clip((w0 // 128) * 128, 0, n_rows - 256)
    c0 = pl.multiple_of(c0, 128)
    s = _dot_nt(q, kc_ref[...])
    n_idx = lax.broadcasted_iota(jnp.int32, s.shape, 1)
    s_far = s + jnp.where(n_idx < c0, 0.0, NEG)
    s_win = _dot_nt(q, kc_ref[pl.ds(c0, 256), :]) + tab_ref[0]
    m = jnp.maximum(jnp.max(s_far, axis=1, keepdims=True), jnp.max(s_win, axis=1, keepdims=True))
    p_far = jnp.exp(s_far - m)
    p_win = jnp.exp(s_win - m)
    l = jnp.sum(p_far, axis=1, keepdims=True) + jnp.sum(p_win, axis=1, keepdims=True)
    inv = jnp.where(m > 0.5 * NEG, 1.0 / l, 0.0)
    pb_far = (p_far * inv).astype(BF16)
    pb_win = (p_win * inv).astype(BF16)
    o = _dot(pb_far, vc_ref[...]) + _dot(pb_win, vc_ref[pl.ds(c0, 256), :])
    impf = _dot(pb_far, ovl_ref[...]) + _dot(pb_win, ovl_ref[pl.ds(c0, 256), :])
    imp = impf[0:TQ] + impf[TQ:2 * TQ] + impf[2 * TQ:3 * TQ] + impf[3 * TQ:4 * TQ]
    gates = misc_ref[...]
    for hh in range(NH):
        oc_ref[hh] = o[hh * TQ:(hh + 1) * TQ] * gates[:, hh:hh + 1]
    j = lax.broadcasted_iota(jnp.int32, imp.shape, 1)
    qpos = i * TQ + lax.broadcasted_iota(jnp.int32, imp.shape, 0)
    qblk = qpos // SLC_BLOCK
    forced = (j == 0) | (j == qblk) | (j == qblk - 1)
    val = jnp.where(forced, BIG, jnp.where(j * SLC_BLOCK <= qpos, imp, -BIG))
    jf = j.astype(F32)
    selb = jnp.full(imp.shape, NEG, F32)
    for first in _select_topk(val, min(N_SELECT, n_blocks)):
        selb = jnp.where(jf == first, 0.0, selb)
    selb_ref[...] = selb.astype(BF16)


def _cmp_attn(qh, kc, vc, ovl, tab, misc):
    l = qh.shape[1]
    r = kc.shape[0]
    ns = ovl.shape[1]

    def variant(i):
        w0 = 8 * i - 8
        c0 = jnp.clip((w0 // 128) * 128, 0, r - 256)
        return ((w0 - c0 + 8) // 8, 0, 0)

    heads = pl.BlockSpec((NH, TQ, HD), lambda i: (0, i, 0))
    return pl.pallas_call(
        functools.partial(_cmp_attn_kernel, n_rows=r, n_blocks=ns),
        grid=(l // TQ,),
        in_specs=[heads, _full((r, HD)), _full((r, HD)), _full((r, ns)),
                  pl.BlockSpec((1, NH * TQ, 256), variant),
                  pl.BlockSpec((TQ, 128), lambda i: (i, 0))],
        out_specs=[heads, pl.BlockSpec((TQ, ns), lambda i: (i, 0))],
        out_shape=[jax.ShapeDtypeStruct((NH, l, HD), F32), jax.ShapeDtypeStruct((l, ns), BF16)],
        compiler_params=_cparams(("arbitrary",)),
        name="nsa_cmp_attn",
    )(qh, kc, vc, ovl, tab, misc)


def _flash_step(carry, s, v):
    m, l, acc = carry
    m_new = jnp.maximum(m, jnp.max(s, axis=1, keepdims=True))
    alpha = jnp.exp(m - m_new)
    p = jnp.exp(s - m_new)
    l = alpha * l + jnp.sum(p, axis=1, keepdims=True)
    acc = alpha * acc + _dot(p.astype(BF16), v)
    return m_new, l, acc


def _flash_init(rows, width):
    return (jnp.full((rows, 1), NEG, F32), jnp.zeros((rows, 1), F32), jnp.zeros((rows, width), F32))


def _nsa_attn_kernel(q_ref, selb_ref, ksel_ref, kwin_ref, vcat_ref, tnear_ref, twin_ref, misc_ref, oc_ref,
                     ya_ref, qx_ref, *, n_super):
    i = pl.program_id(0)
    q0 = i * TQ
    for jj in range(n_super):
        sb = selb_ref[:, jj * 64:(jj + 1) * 64]
        for hh in range(NH):
            qx_ref[jj, hh * TQ:(hh + 1) * TQ, :] = jnp.concatenate([q_ref[hh], sb], axis=1)

    ks = jnp.maximum(q0 - TQ, 0)
    n_full = ks // TK

    def chunk(j):
        k0 = pl.multiple_of(j * TK, TK)
        s = _dot_nt(qx_ref[j // (SUPER // TK)], ksel_ref[pl.ds(k0, TK), :])
        return s, vcat_ref[pl.ds(WINDOW + k0, TK), :]

    def sel_body(j, carry):
        s, v = chunk(j)
        return _flash_step(carry, s, v)

    carry = lax.fori_loop(0, n_full, sel_body, _flash_init(NH * TQ, 128))
    s, v = chunk(n_full)
    col = n_full * TK + lax.broadcasted_iota(jnp.int32, (1, TK), 1)
    carry = _flash_step(carry, s + jnp.where(col < ks, 0.0, NEG), v)
    parts = []
    for half in range(2):
        kh = pl.multiple_of(ks + half * TQ, TQ)
        parts.append(_dot_nt(qx_ref[kh // SUPER], ksel_ref[pl.ds(kh, TQ), :]))
    s = jnp.concatenate(parts, axis=1) + tnear_ref[jnp.minimum(i, 1)]
    _, l_s, acc_s = _flash_step(carry, s, vcat_ref[pl.ds(pl.multiple_of(WINDOW + ks, TQ), 2 * TQ), :])

    w0 = pl.multiple_of(q0, TQ)
    wpos = q0 - WINDOW + lax.broadcasted_iota(jnp.int32, (1, WINDOW + TQ), 1)
    s_w = _dot_nt(qx_ref[0], kwin_ref[pl.ds(w0, WINDOW + TQ), :]) + twin_ref[...] + jnp.where(wpos < 0, NEG, 0.0)
    p_w = jnp.exp(s_w - jnp.max(s_w, axis=1, keepdims=True))
    l_w = jnp.sum(p_w, axis=1, keepdims=True)
    acc_w = _dot(p_w.astype(BF16), vcat_ref[pl.ds(w0, WINDOW + TQ), :])

    o_s = acc_s[:, 0:HD] / l_s
    o_w = acc_w[:, HD:2 * HD] / l_w
    gates = misc_ref[...]
    for hh in range(NH):
        rows = slice(hh * TQ, (hh + 1) * TQ)
        y = oc_ref[hh] + gates[:, NH + hh:NH + hh + 1] * o_s[rows] + gates[:, 2 * NH + hh:2 * NH + hh + 1] * o_w[rows]
        ya_ref[hh] = y.astype(BF16)


def _nsa_attn(qh, selb, ksel, kwin, vcat, tnear, twin, misc, oc):
    l = qh.shape[1]
    ns = selb.shape[1]
    n_super = ns // 64
    heads = pl.BlockSpec((NH, TQ, HD), lambda i: (0, i, 0))
    return pl.pallas_call(
        functools.partial(_nsa_attn_kernel, n_super=n_super),
        grid=(l // TQ,),
        in_specs=[heads, pl.BlockSpec((TQ, ns), lambda i: (i, 0)),
                  _full((l, 128)), _full((l + WINDOW, 128)), _full((l + WINDOW, 128)),
                  _full((2, NH * TQ, 2 * TQ)), _full((NH * TQ, WINDOW + TQ)),
                  pl.BlockSpec((TQ, 128), lambda i: (i, 0)), heads],
        out_specs=heads,
        out_shape=jax.ShapeDtypeStruct((NH, l, HD), BF16),
        scratch_shapes=[pltpu.VMEM((n_super, NH * TQ, 128), BF16)],
        compiler_params=_cparams(("arbitrary",)),
        name="nsa_attn",
    )(qh, selb, ksel, kwin, vcat, tnear, twin, misc, oc)


def _fox_attn_kernel(q_ref, k_ref, v_ref, o_ref):
    i = pl.program_id(1)
    q0 = i * TF
    n_full = q0 // TKF
    outs = []
    for hh in range(2):
        q = q_ref[hh]

        def tile(j, hh=hh, q=q):
            k0 = pl.multiple_of(j * TKF, TKF)
            return _dot_nt(q, k_ref[hh, pl.ds(k0, TKF), :]), v_ref[pl.ds(k0, TKF), :]

        def body(j, carry, tile=tile):
            s, v = tile(j)
            return _flash_step(carry, s, v)

        carry = lax.fori_loop(0, n_full, body, _flash_init(TF, 128))
        s, v = tile(n_full)
        key = n_full * TKF + lax.broadcasted_iota(jnp.int32, s.shape, 1)
        row = q0 + lax.broadcasted_iota(jnp.int32, s.shape, 0)
        _, l, acc = _flash_step(carry, jnp.where(key <= row, s, NEG), v)
        outs.append(acc / l)
    lane = lax.broadcasted_iota(jnp.int32, outs[0].shape, 1)
    o_ref[...] = jnp.where(lane < HD, outs[0], outs[1]).astype(BF16)


def _fox_attn(fqx, fkx, fv):
    l = fqx.shape[1]
    return pl.pallas_call(
        _fox_attn_kernel,
        grid=(NH // 2, l // TF),
        in_specs=[pl.BlockSpec((2, TF, 128), lambda p, i: (p, i, 0)),
                  pl.BlockSpec((2, l, 128), lambda p, i: (p, 0, 0), pipeline_mode=pl.Buffered(1)),
                  pl.BlockSpec((l, 128), lambda p, i: (0, p), pipeline_mode=pl.Buffered(1))],
        out_specs=pl.BlockSpec((TF, 128), lambda p, i: (i, p)),
        out_shape=jax.ShapeDtypeStruct((l, MIX), BF16),
        compiler_params=_cparams(("arbitrary", "arbitrary")),
        name="fox_attn",
    )(fqx, fkx, fv)


def _layernorm_silu(y, ln_ref):
    mu = jnp.mean(y, axis=-1, keepdims=True)
    d = y - mu
    var = jnp.mean(d * d, axis=-1, keepdims=True)
    z = d * lax.rsqrt(var + EPS) * ln_ref[0] + ln_ref[1]
    return z * jax.nn.sigmoid(z)


def _pool_select(sums, counts, u):
    lane = lax.broadcasted_iota(jnp.int32, u.shape, 1)
    group = MIX // len(POOL_WINDOWS)
    out = sums[-1] / counts[-1]
    for g in range(len(POOL_WINDOWS) - 2, -1, -1):
        out = jnp.where(lane < (g + 1) * group, sums[g] / counts[g], out)
    return out - u


def _convpool_kernel(u_ref, uh_ref, up_ref, c_ref, ch_ref, cp_ref, cw_ref, cb_ref, ln_ref, pw_ref, ps_ref,
                     yb_ref, yc_ref, ext_ref, *, pos0):
    i = pl.program_id(0)
    tm = u_ref.shape[0]
    ext_ref[pl.ds(0, HALO), :] = jnp.where(i == 0, up_ref[...], uh_ref[...])
    ext_ref[pl.ds(HALO, tm), :] = u_ref[...]
    acc = jnp.zeros((tm, MIX), F32) + cb_ref[...]
    for w in range(CONV_WIDTH):
        acc = acc + ext_ref[pl.ds(HALO - CONV_HIST + w, tm), :] * cw_ref[pl.ds(w, 1), :]
    yb_ref[...] = _layernorm_silu(acc, ln_ref).astype(BF16)
    c = c_ref[...]
    ext_ref[pl.ds(0, HALO), :] = jnp.where(i == 0, cp_ref[...], ch_ref[...])
    ext_ref[pl.ds(HALO, tm), :] = c
    pos = pos0 + i * tm + lax.broadcasted_iota(jnp.int32, (tm, 1), 0)
    run = c
    sums, counts = [], []
    for k in range(1, max(POOL_WINDOWS)):
        run = run + ext_ref[pl.ds(HALO - k, tm), :]
        if k + 1 in POOL_WINDOWS:
            sums.append(run)
            counts.append(jnp.minimum(k + 1, pos + 1).astype(F32))
    pooled = _pool_select(sums, counts, c).astype(BF16)
    yc_ref[...] = (_dot(pooled, pw_ref[...]) * ps_ref[...]).astype(BF16)


def _convpool(u, u_past, c, c_past, cw, cb, ln, pw_bd, ps, pos0):
    l = u.shape[0]
    tm = min(TM, l)
    nh = tm // HALO
    row = pl.BlockSpec((tm, MIX), lambda i: (i, 0))
    halo = pl.BlockSpec((HALO, MIX), lambda i: (jnp.maximum(i * nh - 1, 0), 0))
    return pl.pallas_call(
        functools.partial(_convpool_kernel, pos0=pos0),
        grid=(l // tm,),
        in_specs=[row, halo, _full((HALO, MIX)), row, halo, _full((HALO, MIX)),
                  _full((HALO, MIX)), _full((1, MIX)), _full((2, 1, MIX)), _full((MIX, MIX)), _full((1, MIX))],
        out_specs=[row, row],
        out_shape=[jax.ShapeDtypeStruct((l, MIX), BF16)] * 2,
        scratch_shapes=[pltpu.VMEM((tm + HALO, MIX), F32)],
        compiler_params=_cparams(("arbitrary",)),
        name="convpool",
    )(u, u, u_past, c, c, c_past, cw, cb, ln, pw_bd, ps)


def _convpool_step_kernel(u_ref, up_ref, c_ref, cp_ref, cw_ref, cb_ref, ln_ref, pw_ref, ps_ref, yb_ref, yc_ref,
                          *, pos0):
    u = u_ref[...]
    acc = cb_ref[...] + u * cw_ref[pl.ds(CONV_WIDTH - 1, 1), :]
    for w in range(CONV_HIST):
        acc = acc + up_ref[w] * cw_ref[pl.ds(w, 1), :]
    yb_ref[...] = _layernorm_silu(acc, ln_ref).astype(BF16)
    c = c_ref[...]
    run = c
    sums, counts = [], []
    for k in range(1, max(POOL_WINDOWS)):
        run = run + cp_ref[POOL_HIST - k]
        if k + 1 in POOL_WINDOWS:
            sums.append(run)
            counts.append(float(min(k + 1, pos0 + 1)))
    pooled = _pool_select(sums, counts, c).astype(BF16)
    yc_ref[...] = (_dot(pooled, pw_ref[...]) * ps_ref[...]).astype(BF16)


def _convpool_step(u, u_past, c, c_past, cw, cb, ln, pw_bd, ps, pos0):
    b = u.shape[0]
    return pl.pallas_call(
        functools.partial(_convpool_step_kernel, pos0=pos0),
        out_shape=[jax.ShapeDtypeStruct((b, MIX), BF16)] * 2,
        name="convpool_step",
    )(u, u_past, c, c_past, cw, cb, ln, pw_bd, ps)


def _nsa_paged_cmp_kernel(pt_ref, *refs, n_pages, past):
    pages = refs[:PG]
    (q_ref, pos_ref, w_ref, ovl_ref, bias_ref, misc_ref, oc_ref, idx_ref, rk_ref, rv_ref, scr, tok_ref) = refs[PG:]
    p = pl.program_id(1)
    for g in range(PG):
        r0 = pl.multiple_of((p * PG + g) * 8, 8)
        tok_ref[...] = pages[g][0, 0].reshape(2 * HD, PAGE).T
        for t in range(CMP_STRIDE):
            rows = tok_ref[pl.ds(t, 8, stride=CMP_STRIDE), :]
            rk_ref[pl.ds(r0, 8), t * HD:(t + 1) * HD] = rows[:, 0:HD]
            rv_ref[pl.ds(r0, 8), t * HD:(t + 1) * HD] = rows[:, HD:2 * HD]

    @pl.when(p == n_pages // PG - 1)
    def _():
        kc = _compress(rk_ref[...], pos_ref, w_ref, 0, scr).astype(BF16)
        vc = _compress(rv_ref[...], pos_ref, w_ref, 1, scr).astype(BF16)
        q = q_ref[0]
        s = _dot_nt(q, kc) + bias_ref[...]
        m = jnp.max(s, axis=1, keepdims=True)
        e = jnp.exp(s - m)
        pr = (e / jnp.sum(e, axis=1, keepdims=True)).astype(BF16)
        o = _dot(pr, vc)
        gates = misc_ref[0]
        oc_ref[0] = o * gates[:, 0:1]
        impf = _dot(pr, ovl_ref[...])
        imp = jnp.sum(impf[0:NH], axis=0, keepdims=True)
        ns = imp.shape[1]
        j = lax.broadcasted_iota(jnp.int32, imp.shape, 1)
        qblk = past // SLC_BLOCK
        forced = (j == 0) | (j == qblk) | (j == qblk - 1)
        val = jnp.where(forced, BIG, jnp.where(j * SLC_BLOCK <= past, imp, -BIG))
        val = jnp.where(j <= qblk, val, -jnp.inf)
        lane = lax.broadcasted_iota(jnp.int32, (1, 128), 1)
        idx = jnp.zeros((1, 128), F32)
        for t, first in enumerate(_select_topk(val, min(N_SELECT, qblk + 1))):
            idx = jnp.where(lane == t, first, idx)
        idx_ref[0] = idx.astype(jnp.int32)


def _nsa_paged_cmp(page_table, cache, layer, q8, pos, w, ovl, bias, misc8, past):
    bsz, n_pages = page_table.shape
    n_rows = past // CMP_STRIDE
    ns = ovl.shape[1]

    def page_spec(g):
        return pl.BlockSpec((1, 1, 2, HD, PAGE), lambda b, p, pt, _g=g: (layer, pt[b, p * PG + _g], 0, 0, 0))

    def const(shape):
        nd = len(shape)
        return pl.BlockSpec(shape, lambda b, p, pt, _nd=nd: (0,) * _nd)

    def per_b(shape):
        return pl.BlockSpec(shape, lambda b, p, pt: (b, 0, 0))

    grid_spec = pltpu.PrefetchScalarGridSpec(
        num_scalar_prefetch=1,
        grid=(bsz, n_pages // PG),
        in_specs=[page_spec(g) for g in range(PG)] + [
            per_b((1, 8, HD)), const((2, 2, 1, 1024)), const((2, 2, 1024, HD)), const((n_rows, ns)),
            const((8, n_rows)), per_b((1, 8, 128))],
        out_specs=[per_b((1, 8, HD)), per_b((1, 1, 128))],
        scratch_shapes=[pltpu.VMEM((n_rows, 1024), F32), pltpu.VMEM((n_rows, 1024), F32),
                        pltpu.VMEM((n_rows + 8, HD), F32), pltpu.VMEM((PAGE, 2 * HD), F32)],
    )
    return pl.pallas_call(
        functools.partial(_nsa_paged_cmp_kernel, n_pages=n_pages, past=past),
        grid_spec=grid_spec,
        out_shape=[jax.ShapeDtypeStruct((bsz, 8, HD), F32), jax.ShapeDtypeStruct((bsz, 1, 128), jnp.int32)],
        compiler_params=_cparams(("arbitrary", "arbitrary")),
        name="nsa_paged_cmp",
    )(page_table, *([cache] * PG), q8, pos, w, ovl, bias, misc8)


def _nsa_paged_attn_kernel(idx_ref, *refs, past):
    blocks = refs[:N_SELECT]
    (q_ref, new_ref, win_ref, neww_ref, relt_ref, wbias_ref, misc_ref, oc_ref, ya_ref) = refs[N_SELECT:]
    b = pl.program_id(0)
    q = q_ref[0]
    qf = q.astype(F32)

    def own_logit(k_row):
        return jnp.sum(qf * k_row.astype(BF16).astype(F32), axis=1, keepdims=True)

    def attend(s, s_own, values, v_own):
        m = jnp.maximum(jnp.max(s, axis=1, keepdims=True), s_own)
        e = jnp.exp(s - m)
        e_own = jnp.exp(s_own - m)
        inv = 1.0 / (jnp.sum(e, axis=1, keepdims=True) + e_own)
        pr = (e * inv).astype(BF16)
        o = (e_own * inv) * v_own.astype(BF16).astype(F32)
        for c, v in enumerate(values):
            o = o + _dot_nt(pr[:, c * PAGE:(c + 1) * PAGE], v)
        return o

    lane = lax.broadcasted_iota(jnp.int32, (1, PAGE), 1)
    s_parts, pos_parts, hide_parts, v_parts = [], [], [], []
    for t in range(N_SELECT):
        j = idx_ref[b, N_SELECT + t]
        s_parts.append(_dot(q, blocks[t][0, 0, 0].astype(BF16)))
        v_parts.append(blocks[t][0, 0, 1].astype(BF16))
        pos_parts.append(jnp.minimum(j // 2, past // PAGE - 1) * PAGE + lane)
        half = jnp.where(j < past // SLC_BLOCK, j % 2, 2)
        hide_parts.append(jnp.where(lane // SLC_BLOCK == half, 0.0, NEG))
    pos = jnp.concatenate(pos_parts, axis=1)
    s = jnp.concatenate(s_parts, axis=1) + _t5_bias(relt_ref[...], past - pos) + jnp.concatenate(hide_parts, axis=1)
    new = new_ref[0]
    bias0 = _t5_bias(relt_ref[...], jnp.zeros((1, PAGE), jnp.int32))[:, 0:1]
    o_s = attend(s, own_logit(new[:, 2 * HD:3 * HD]) + bias0, v_parts, new[:, 3 * HD:4 * HD])
    n_win = win_ref.shape[4]
    s_w = _dot(q, win_ref[0, 0, 0].astype(BF16)) + wbias_ref[...]
    vw = [win_ref[0, 0, 1, :, c * PAGE:(c + 1) * PAGE].astype(BF16) for c in range(n_win // PAGE)]
    neww = neww_ref[0]
    o_w = attend(s_w, own_logit(neww[:, 0:HD]) + bias0, vw, neww[:, HD:2 * HD])
    gates = misc_ref[0]
    ya_ref[0] = (oc_ref[0] + gates[:, 1:2] * o_s + gates[:, 2:3] * o_w).astype(BF16)


def _nsa_paged_attn(sel, cache, layer, q8, nsa_new, win_state, win_new, relt8, wbias, misc8, oc, past):
    bsz = sel.shape[0]
    n_win = win_state.shape[4]

    def blk_spec(t):
        return pl.BlockSpec((1, 1, 2, HD, PAGE), lambda b, ix, _t=t: (layer, ix[b, _t], 1, 0, 0))

    def const(shape):
        nd = len(shape)
        return pl.BlockSpec(shape, lambda b, ix, _nd=nd: (0,) * _nd)

    def per_b(shape):
        return pl.BlockSpec(shape, lambda b, ix: (b, 0, 0))

    win_spec = pl.BlockSpec((1, 1, 2, HD, n_win), lambda b, ix: (layer, b, 0, 0, 0))
    grid_spec = pltpu.PrefetchScalarGridSpec(
        num_scalar_prefetch=1,
        grid=(bsz,),
        in_specs=[blk_spec(t) for t in range(N_SELECT)] + [
            per_b((1, 8, HD)), per_b((1, 1, 256)), win_spec, per_b((1, 1, 128)),
            const((8, N_BUCKETS)), const((8, n_win)), per_b((1, 8, 128)), per_b((1, 8, HD))],
        out_specs=per_b((1, 8, HD)),
    )
    return pl.pallas_call(
        functools.partial(_nsa_paged_attn_kernel, past=past),
        grid_spec=grid_spec,
        out_shape=jax.ShapeDtypeStruct((bsz, 8, HD), BF16),
        compiler_params=_cparams(("arbitrary",)),
        name="nsa_paged_attn",
    )(sel, *([cache] * N_SELECT), q8, nsa_new, win_state, win_new, relt8, wbias, misc8, oc)


def _fox_paged_kernel(pt_ref, *refs, n_pages):
    kv_pages = refs[:PG]
    lf_pages = refs[PG:2 * PG]
    (qbd_ref, new_ref, lfnew_ref, o_ref, m_ref, l_ref, acc_ref, car_ref) = refs[2 * PG:]
    p = pl.program_id(1)
    qbd = qbd_ref[0]

    @pl.when(p == 0)
    def _():
        new = new_ref[0]
        kn = new[:, 0:MIX].astype(BF16).astype(F32)
        m_ref[...] = jnp.sum(qbd.astype(F32) * kn, axis=1, keepdims=True)
        l_ref[...] = jnp.ones((8, 1), F32)
        acc_ref[...] = jnp.broadcast_to(new[:, MIX:2 * MIX].astype(BF16).astype(F32), (8, MIX))
        car_ref[...] = lfnew_ref[0]

    later = jnp.where(lax.broadcasted_iota(jnp.int32, (PAGE, PAGE), 0) > lax.broadcasted_iota(jnp.int32, (PAGE, PAGE), 1),
                      1.0, 0.0).astype(BF16)
    carry = car_ref[...]
    s_parts = [None] * PG
    for g in range(PG - 1, -1, -1):
        lf = lf_pages[g][0, 0]
        kt = kv_pages[g][0, 0, 0].astype(BF16)
        s_parts[g] = _dot(qbd, kt) + carry + _dot01(lf, later)
        carry = carry + jnp.sum(lf, axis=1, keepdims=True)
    car_ref[...] = carry
    s = jnp.concatenate(s_parts, axis=1)
    m_old = m_ref[...]
    m_new = jnp.maximum(m_old, jnp.max(s, axis=1, keepdims=True))
    alpha = jnp.exp(m_old - m_new)
    pr = jnp.exp(s - m_new)
    l_ref[...] = alpha * l_ref[...] + jnp.sum(pr, axis=1, keepdims=True)
    acc = alpha * acc_ref[...]
    prb = pr.astype(BF16)
    for g in range(PG):
        acc = acc + _dot_nt(prb[:, g * PAGE:(g + 1) * PAGE], kv_pages[g][0, 0, 1].astype(BF16))
    acc_ref[...] = acc
    m_ref[...] = m_new

    @pl.when(p == n_pages // PG - 1)
    def _():
        out = acc / l_ref[...]
        own = lax.broadcasted_iota(jnp.int32, out.shape, 1) // HD == lax.broadcasted_iota(jnp.int32, out.shape, 0)
        o_ref[0] = jnp.sum(jnp.where(own, out, 0.0), axis=0, keepdims=True).astype(BF16)


def _fox_paged(page_table, cache_kv, cache_lf, layer, qbd, kv_new, lf_new):
    bsz, n_pages = page_table.shape
    nchunk = n_pages // PG

    def kv_spec(g):
        return pl.BlockSpec((1, 1, 2, MIX, PAGE),
                            lambda b, p, pt, _g=g: (layer, pt[b, (nchunk - 1 - p) * PG + _g], 0, 0, 0))

    def lf_spec(g):
        return pl.BlockSpec((1, 1, 8, PAGE), lambda b, p, pt, _g=g: (layer, pt[b, (nchunk - 1 - p) * PG + _g], 0, 0))

    def per_b(shape):
        return pl.BlockSpec(shape, lambda b, p, pt: (b, 0, 0))

    grid_spec = pltpu.PrefetchScalarGridSpec(
        num_scalar_prefetch=1,
        grid=(bsz, nchunk),
        in_specs=[kv_spec(g) for g in range(PG)] + [lf_spec(g) for g in range(PG)] + [
            per_b((1, 8, MIX)), per_b((1, 1, 2 * MIX)), per_b((1, 8, 1))],
        out_specs=per_b((1, 1, MIX)),
        scratch_shapes=[pltpu.VMEM((8, 1), F32), pltpu.VMEM((8, 1), F32), pltpu.VMEM((8, MIX), F32),
                        pltpu.VMEM((8, 1), F32)],
    )
    return pl.pallas_call(
        functools.partial(_fox_paged_kernel, n_pages=n_pages),
        grid_spec=grid_spec,
        out_shape=jax.ShapeDtypeStruct((bsz, 1, MIX), BF16),
        compiler_params=_cparams(("arbitrary", "arbitrary")),
        name="fox_paged",
    )(page_table, *([cache_kv] * PG), *([cache_lf] * PG), qbd, kv_new, lf_new)


def _layer_params(l, norm_g, ffn_gate, ffn_up, ffn_down, w_in, fox_f_bias, nsa_cmp_pos, nsa_cmp_w,
                  conv_w, conv_b, conv_ln, pool_w, pool_scale, w_branch, w_out):
    wi = w_in[l]
    o_aq, o_akv, o_ag, o_b, o_c, o_d, o_f, o_m = np.cumsum([0, 256, 384, 12, 512, 256, 768, 4])
    misc = jnp.concatenate([wi[:, o_ag:o_ag + 12], wi[:, o_f:o_f + 4], jnp.zeros((D_MODEL, 112), F32)], axis=1)
    w_small = jnp.concatenate([wi[:, o_aq:o_aq + 256], wi[:, o_akv:o_akv + 384], misc, wi[:, o_b:o_b + 512],
                               wi[:, o_c:o_c + 256], wi[:, o_d:o_d + 768]], axis=1).astype(BF16)
    fb_row = jnp.zeros((1, 128), F32).at[0, 12:16].set(fox_f_bias[l])
    cw = nsa_cmp_w[l].reshape(2, 2, CMP_STRIDE * HD, HD).astype(BF16)
    cpos = nsa_cmp_pos[l].reshape(2, 2, 1, CMP_STRIDE * HD)
    group = MIX // len(POOL_WINDOWS)
    pw_bd = jnp.zeros((MIX, MIX), F32)
    for g in range(len(POOL_WINDOWS)):
        pw_bd = pw_bd.at[g * group:(g + 1) * group, g * group:(g + 1) * group].set(pool_w[l, g])
    cw_conv = jnp.concatenate([conv_w[l], jnp.zeros((HALO - CONV_WIDTH, MIX), F32)], axis=0)
    return dict(
        g=[norm_g[l, k].reshape(1, D_MODEL) for k in range(3)],
        ffn=[(ffn_gate[l, k].astype(BF16), ffn_up[l, k].astype(BF16), ffn_down[l, k].astype(BF16)) for k in range(2)],
        w_small=w_small, w_merge=wi[:, o_m:].astype(BF16), fb_row=fb_row, cmp_w=cw, cmp_pos=cpos,
        conv_w=cw_conv, conv_b=conv_b[l].reshape(1, MIX), conv_ln=conv_ln[l].reshape(2, 1, MIX),
        pool_w=pw_bd.astype(BF16), pool_scale=pool_scale[l].reshape(1, MIX),
        w_branch=w_branch[l].astype(BF16), w_out=w_out[l].astype(BF16))


def _overlap(n_rows, n_blocks, n_pad):
    cs = np.arange(n_rows)[:, None] * CMP_STRIDE
    j0 = np.arange(n_pad)[None, :] * SLC_BLOCK
    ov = (cs < j0 + SLC_BLOCK) & (cs + CMP_LEN > j0) & (np.arange(n_pad)[None, :] < n_blocks)
    return jnp.asarray(ov.astype(np.float32), dtype=BF16)


def _lookup(bt, dist, valid):
    d = np.clip(dist, 0, NDIST - 1)
    return jnp.where(jnp.asarray(valid)[None], bt[:, d], NEG)


def _toeplitz_kernel(rel_ref, o_ref, *, base, vstep, stride, hi):
    width = o_ref.shape[2]
    r = lax.broadcasted_iota(jnp.int32, (TQ, width), 0)
    m = lax.broadcasted_iota(jnp.int32, (TQ, width), 1)
    dist = base + vstep * pl.program_id(0) + r - stride * m
    bucket = _t5_bucket(dist)
    hide = jnp.where(dist < 0, NEG, jnp.where(dist > hi, NEG, 0.0))
    for hh in range(NH):
        far = rel_ref[N_BUCKETS - 1, hh]
        val = jnp.zeros((TQ, width), F32)
        for b in range(N_BUCKETS - 1):
            val = jnp.where(bucket == b, rel_ref[b, hh] - far, val)
        o_ref[0, hh * TQ:(hh + 1) * TQ, :] = val + hide


def _toeplitz(rel_bias, n_var, width, base, vstep, stride, hi):
    return pl.pallas_call(
        functools.partial(_toeplitz_kernel, base=base, vstep=vstep, stride=stride, hi=hi),
        grid=(n_var,),
        in_specs=[pl.BlockSpec(memory_space=pltpu.SMEM)],
        out_specs=pl.BlockSpec((1, NH * TQ, width), lambda v: (v, 0, 0)),
        out_shape=jax.ShapeDtypeStruct((n_var, NH * TQ, width), F32),
        compiler_params=_cparams(("arbitrary",)),
        name="t5_tables",
    )(rel_bias)


def _prompt_tables(rel_bias):
    big = 1 << 30
    tab_c = _toeplitz(rel_bias, 33, 256, -(CMP_LEN - 1), TQ, CMP_STRIDE, big)
    tnear = _toeplitz(rel_bias, 2, 2 * TQ, 0, TQ, 1, big)
    twin = _toeplitz(rel_bias, 1, WINDOW + TQ, WINDOW, 0, 1, WINDOW)[0]
    return tab_c, tnear, twin


def _prompt_layer(x, lp, tables, fg, final):
    l = x.shape[0]
    tab_c, tnear, twin = tables
    x = _ffn(x, lp["g"][0], *lp["ffn"][0], fg)
    (qn, nsa_kv, win_kv, misc, u, cin, fq, fox_kv, ksel, kwin, vcat, fk, fv) = _inproj(x, lp["g"][1], lp["w_small"], lp["fb_row"])
    n_rows = l // CMP_STRIDE
    ns = l // SLC_BLOCK
    rk = nsa_kv[:, 0:HD].reshape(n_rows, CMP_STRIDE * HD)
    rv = nsa_kv[:, HD:2 * HD].reshape(n_rows, CMP_STRIDE * HD)
    kc, vc = _compress_prompt(rk, rv, lp["cmp_pos"], lp["cmp_w"])
    oc, selb = _cmp_attn(qn, kc, vc, _overlap(n_rows, ns, ns), tab_c, misc)
    front = jnp.zeros((WINDOW, 128), BF16)
    ya = _nsa_attn(qn, selb, ksel, jnp.concatenate([front, kwin]), jnp.concatenate([front, vcat]), tnear, twin, misc, oc)
    zeros = jnp.zeros((HALO, MIX), F32)
    yb, yc = _convpool(u, zeros, cin, zeros, lp["conv_w"], lp["conv_b"], lp["conv_ln"], lp["pool_w"], lp["pool_scale"], 0)
    logf = misc[:, 12:16]
    yd = _fox_attn(fq, fk, fv)
    x = _merge(x, lp["g"][1], lp["w_merge"], lp["w_branch"], lp["w_out"], ya, yb, yc, yd)
    x = _ffn(x, lp["g"][2], *lp["ffn"][1], fg, final)
    state = (nsa_kv, fox_kv, logf, win_kv[l - min(WINDOW, l):], u[l - CONV_HIST:], cin[l - POOL_HIST:])
    return x, state


def _sample_layer(x, lp, relt8, bt, fg, final, layer, page_table, nsa_pages, fox_pages, lf_pages, win_rows, st_win,
                  st_conv, st_pool):
    bsz = x.shape[0]
    n_pages = page_table.shape[1]
    past = n_pages * PAGE
    x = _ffn(x, lp["g"][0], *lp["ffn"][0], fg)
    (qn, nsa_kv, win_kv, misc, u, cin, fq, fox_kv, _, _, _, _, _) = _inproj(x, lp["g"][1], lp["w_small"], lp["fb_row"])
    pad4 = lambda a: jnp.concatenate([a, jnp.zeros((bsz, 8 - NH) + a.shape[2:], a.dtype)], axis=1)
    q8 = pad4(jnp.transpose(qn, (1, 0, 2)))
    gates = misc[:, 0:3 * NH].reshape(bsz, 3, NH)
    misc8 = jnp.zeros((bsz, 8, 128), F32).at[:, 0:NH, 0:3].set(jnp.transpose(gates, (0, 2, 1)))
    n_rows = past // CMP_STRIDE
    ns = past // SLC_BLOCK + 1
    ns_pad = -(-ns // 128) * 128
    c_end = np.arange(n_rows) * CMP_STRIDE + CMP_LEN - 1
    valid_c = (c_end <= past) & (np.arange(n_rows) < n_rows - 1)
    bias_c = jnp.concatenate([_lookup(bt, past - c_end, valid_c), jnp.full((8 - NH, n_rows), NEG, F32)], axis=0)
    oc, idx = _nsa_paged_cmp(page_table, nsa_pages, layer, q8, lp["cmp_pos"], lp["cmp_w"],
                             _overlap(n_rows, ns, ns_pad), bias_c, misc8, past)
    n_win = st_win.shape[1]
    wbias = jnp.concatenate([_lookup(bt, n_win - np.arange(n_win), np.ones(n_win, bool)),
                             jnp.full((8 - NH, n_win), NEG, F32)], axis=0)
    blk = idx.reshape(bsz, 128)[:, 0:N_SELECT]
    held = jnp.minimum(blk // 2, n_pages - 1)
    pages = jnp.take_along_axis(page_table, held, axis=1)
    ya = _nsa_paged_attn(jnp.concatenate([pages, blk], axis=1), nsa_pages, layer, q8, nsa_kv.reshape(bsz, 1, 256),
                         win_rows, win_kv.reshape(bsz, 1, 128), relt8, wbias, misc8, oc, past)
    ya = jnp.transpose(ya[:, 0:NH], (1, 0, 2))
    yb, yc = _convpool_step(u, jnp.transpose(st_conv, (1, 0, 2)), cin, jnp.transpose(st_pool, (1, 0, 2)),
                            lp["conv_w"], lp["conv_b"], lp["conv_ln"], lp["pool_w"], lp["pool_scale"], past)
    logf = misc[:, 12:16]
    fq8 = pad4(jnp.transpose(fq[:, :, 0:HD], (1, 0, 2)))
    own = (np.arange(MIX)[None, :] // HD == np.arange(8)[:, None])
    qbd = jnp.where(jnp.asarray(own)[None], jnp.tile(fq8, (1, 1, NH)), jnp.zeros((), BF16))
    lf_new = pad4(logf.reshape(bsz, NH, 1))
    yd = _fox_paged(page_table, fox_pages, lf_pages, layer, qbd, fox_kv.reshape(bsz, 1, 2 * MIX), lf_new)
    yd = yd.reshape(bsz, MIX)
    x = _merge(x, lp["g"][1], lp["w_merge"], lp["w_branch"], lp["w_out"], ya, yb, yc, yd)
    x = _ffn(x, lp["g"][2], *lp["ffn"][1], fg, final)
    new_win =jnp.concatenate([st_win.reshape(bsz, n_win, 128)[:, 1:], win_kv[:, None, :]], axis=1)
    new_conv = jnp.concatenate([st_conv[:, 1:], u[:, None, :]], axis=1)
    new_pool = jnp.concatenate([st_pool[:, 1:], cin[:, None, :]], axis=1)
    state = (nsa_kv, fox_kv, logf, new_win, new_conv, new_pool)
    return x, state


def kernel(x_prompt, x_sample, cache_nsa, cache_fox_kv, cache_fox_logf, state_nsa_win, state_conv, state_pool,
           page_table, norm_g, ffn_gate, ffn_up, ffn_down, w_in, fox_f_bias, nsa_cmp_pos, nsa_cmp_w, rel_bias,
           conv_w, conv_b, conv_ln, pool_w, pool_scale, w_branch, w_out, final_norm_g):
    assert x_prompt.shape[0] == 1 and x_sample.shape[1] == 1
    depth = norm_g.shape[0]
    l = x_prompt.shape[1]
    bsz = x_sample.shape[0]
    win_keep = state_nsa_win.shape[2]
    assert l % SUPER == 0 and page_table.shape[1] * PAGE >= max(WINDOW, SUPER) and win_keep == WINDOW
    relt8 = jnp.concatenate([rel_bias.T, jnp.zeros((8 - NH, N_BUCKETS), F32)], axis=0)
    bt = _bias_table(relt8)[0:NH]
    tables = _prompt_tables(rel_bias)
    fg = final_norm_g.reshape(1, D_MODEL)
    xp = x_prompt.reshape(l, D_MODEL)
    xs = x_sample.reshape(bsz, D_MODEL)
    n_phys = cache_nsa.shape[1]
    nsa_pages = jnp.transpose(cache_nsa, (0, 1, 3, 4, 5, 2)).reshape(depth, n_phys, 4, HD, PAGE)
    fox_pages = jnp.transpose(cache_fox_kv, (0, 1, 3, 4, 5, 2)).reshape(depth, n_phys, 2, MIX, PAGE)
    lf_pages = jnp.pad(jnp.transpose(cache_fox_logf, (0, 1, 3, 2)), ((0, 0), (0, 0), (0, 8 - NH), (0, 0)))
    win_rows = jnp.transpose(state_nsa_win, (0, 1, 3, 4, 5, 2)).reshape(depth, bsz, 2, HD, win_keep)
    st_p, st_s = [], []
    for layer in range(depth):
        lp = _layer_params(layer, norm_g, ffn_gate, ffn_up, ffn_down, w_in, fox_f_bias, nsa_cmp_pos, nsa_cmp_w,
                           conv_w, conv_b, conv_ln, pool_w, pool_scale, w_branch, w_out)
        final = layer == depth - 1
        xp, sp = _prompt_layer(xp, lp, tables, fg, final)
        st_p.append(sp)
        xs, ss = _sample_layer(xs, lp, relt8, bt, fg, final, layer, page_table, nsa_pages, fox_pages, lf_pages, win_rows,
                               state_nsa_win[layer], state_conv[layer], state_pool[layer])
        st_s.append(ss)

    def stack(states, k, shape):
        return jnp.stack([s[k] for s in states]).reshape(shape)

    return (xp.reshape(1, l, D_MODEL), xs.reshape(bsz, 1, D_MODEL),
            stack(st_p, 0, (depth, 1, l, 4, 1, HD)), stack(st_s, 0, (depth, bsz, 1, 4, 1, HD)),
            stack(st_p, 1, (depth, 1, l, 2, NH, HD)), stack(st_s, 1, (depth, bsz, 1, 2, NH, HD)),
            stack(st_p, 2, (depth, 1, l, NH)), stack(st_s, 2, (depth, bsz, 1, NH)),
            stack(st_p, 3, (depth, 1, win_keep, 2, 1, HD)), stack(st_s, 3, (depth, bsz, win_keep, 2, 1, HD)),
            stack(st_p, 4, (depth, 1, CONV_HIST, MIX)), stack(st_s, 4, (depth, bsz, CONV_HIST, MIX)),
            stack(st_p, 5, (depth, 1, POOL_HIST, MIX)), stack(st_s, 5, (depth, bsz, POOL_HIST, MIX)))
```

```python
import functools
import math

import numpy as np
import jax
import jax.numpy as jnp
from jax import lax
from jax.experimental import pallas as pl
from jax.experimental.pallas import tpu as pltpu

F32 = jnp.float32
BF16 = jnp.bfloat16

D_MODEL = 1024
N_BRANCH = 4
MIX = D_MODEL // 4
HD = 64
NH = MIX // HD
CMP_STRIDE = 16
CMP_LEN = 32
SLC_BLOCK = 64
N_SELECT = 16
WINDOW = 512
N_BUCKETS = 32
REL_EXACT = 16
REL_MAX_DIST = 128
CONV_WIDTH = 31
CONV_HIST = CONV_WIDTH - 1
POOL_WINDOWS = (2, 4, 8, 16)
POOL_HIST = 15
D_FF = 2816
EPS = 1e-6
NEG = -1e30
BIG = 1e6
PAGE = 128

LANE = 128
TQ = 128
TK = 1024
SUPER = 64 * SLC_BLOCK
TF = 256
TKF = 1024
FOX_CUT = 100.0
TM = 256
FF_CHUNK = 1408
HALO = 32
NDIST = 1024
PG = 8
VMEM_LIMIT = 56 * 1024 * 1024

C_AQ, C_NSA, C_WIN, C_MISC, C_CONV, C_POOL, C_FOX = 0, 256, 512, 640, 768, 1280, 1536
N_SMALL = 2304


def _cparams(sem):
    return pltpu.CompilerParams(dimension_semantics=sem, vmem_limit_bytes=VMEM_LIMIT)


def _full(shape, single=True):
    nd = len(shape)
    kw = dict(pipeline_mode=pl.Buffered(1)) if single else {}
    return pl.BlockSpec(shape, lambda *a, _nd=nd: (0,) * _nd, **kw)


def _rms(x, g):
    ms = jnp.mean(x * x, axis=-1, keepdims=True)
    return x * lax.rsqrt(ms + EPS) * g


def _dot(a, b):
    return jnp.dot(a, b, preferred_element_type=F32)


def _dot_nt(a, b):
    return lax.dot_general(a, b, (((1,), (1,)), ((), ())), preferred_element_type=F32)


def _split3(x):
    hi = x.astype(BF16)
    r1 = x - hi.astype(F32)
    mid = r1.astype(BF16)
    lo = (r1 - mid.astype(F32)).astype(BF16)
    return hi, mid, lo


def _dot01(x, ones_bf16):
    hi, mid, lo = _split3(x)
    return _dot(hi, ones_bf16) + _dot(mid, ones_bf16) + _dot(lo, ones_bf16)


def _dot01_l(ones_bf16, x):
    hi, mid, lo = _split3(x)
    return _dot(ones_bf16, hi) + _dot(ones_bf16, mid) + _dot(ones_bf16, lo)


def _log_sigmoid(x):
    return jnp.minimum(x, 0.0) - jnp.log1p(jnp.exp(-jnp.abs(x)))


def _t5_bucket(dist):
    n = jnp.maximum(dist, 0)
    ratio = jnp.log(jnp.maximum(n, 1).astype(F32) / REL_EXACT) / math.log(REL_MAX_DIST / REL_EXACT)
    large = REL_EXACT + (ratio * (N_BUCKETS - REL_EXACT)).astype(jnp.int32)
    return jnp.where(n < REL_EXACT, n, jnp.minimum(large, N_BUCKETS - 1))


def _t5_bias(relt, dist):
    bucket = _t5_bucket(dist)
    rows = lax.broadcasted_iota(jnp.int32, (N_BUCKETS, dist.shape[1]), 0)
    onehot = jnp.where(rows == bucket, 1.0, 0.0).astype(BF16)
    return _dot01(relt, onehot)


def _ffn_kernel(x_ref, g_ref, wg_ref, wu_ref, wd_ref, fg_ref, o_ref, *, final):
    x = x_ref[...]
    h = _rms(x, g_ref[...]).astype(BF16)
    acc = jnp.zeros_like(x)
    for c in range(D_FF // FF_CHUNK):
        sl = slice(c * FF_CHUNK, (c + 1) * FF_CHUNK)
        a = _dot(h, wg_ref[:, sl])
        b = _dot(h, wu_ref[:, sl])
        t = (a * jax.nn.sigmoid(a) * b).astype(BF16)
        acc = acc + _dot(t, wd_ref[sl, :])
    out = x + 0.5 * acc
    o_ref[...] = _rms(out, fg_ref[...]) if final else out


def _ffn(x, g, wg, wu, wd, fg, final=False):
    m = x.shape[0]
    tm = min(TM, m)
    row = pl.BlockSpec((tm, D_MODEL), lambda i: (i, 0))
    return pl.pallas_call(
        functools.partial(_ffn_kernel, final=final),
        grid=(m // tm,),
        in_specs=[row, _full((1, D_MODEL)), _full((D_MODEL, D_FF)), _full((D_MODEL, D_FF)),
                  _full((D_FF, D_MODEL)), _full((1, D_MODEL))],
        out_specs=row,
        out_shape=jax.ShapeDtypeStruct((m, D_MODEL), F32),
        compiler_params=_cparams(("arbitrary",)),
        name="ffn",
    )(x, g, wg, wu, wd, fg)


def _inproj_kernel(x_ref, g_ref, w_ref, fb_ref,
                   qn_ref, nsa_ref, win_ref, misc_ref, u_ref, cin_ref, fq_ref, fkv_ref,
                   ksel_ref, kwin_ref, vcat_ref, fk_ref, fv_ref, stat_ref, csum_ref):
    tm = x_ref.shape[0]
    h = _rms(x_ref[...], g_ref[...]).astype(BF16)
    z = _dot(h, w_ref[...])
    scale = HD ** -0.5
    zm = z[:, C_MISC:C_MISC + 128]
    lane = lax.broadcasted_iota(jnp.int32, zm.shape, 1)
    logf = _log_sigmoid(zm + fb_ref[...])
    misc_ref[...] = jnp.where(lane < 3 * NH, jax.nn.sigmoid(zm), logf)
    @pl.when(pl.program_id(0) == 0)
    def _():
        csum_ref[...] = jnp.zeros_like(csum_ref)
    tri = jnp.where(lax.broadcasted_iota(jnp.int32, (tm, tm), 1) <= lax.broadcasted_iota(jnp.int32, (tm, tm), 0),
                    1.0, 0.0).astype(BF16)
    lf = jnp.where((lane >= 3 * NH) & (lane < 4 * NH), logf, 0.0)
    csum = _dot01_l(tri, lf) + csum_ref[...]
    csum_ref[...] = csum[tm - 1:tm, :]
    feat = lax.broadcasted_iota(jnp.int32, (tm, HD), 1)
    ones3 = jnp.where(feat < 3, 1.0, 0.0).astype(BF16)
    knorm = jnp.zeros((1, 128), F32)
    for hh in range(NH):
        qn_ref[hh] = (z[:, C_AQ + HD * hh:C_AQ + HD * (hh + 1)] * scale).astype(BF16)
        fq = (z[:, C_FOX + HD * hh:C_FOX + HD * (hh + 1)] * scale).astype(BF16)
        fq_ref[hh] = jnp.concatenate([fq, ones3], axis=1)
        hi, mid, lo = (p.astype(F32) for p in _split3(-csum[:, 3 * NH + hh:3 * NH + hh + 1]))
        extra = jnp.where(feat == 0, hi, jnp.where(feat == 1, mid, jnp.where(feat == 2, lo, 0.0)))
        fk = z[:, C_FOX + MIX + HD * hh:C_FOX + MIX + HD * (hh + 1)].astype(BF16)
        fk_ref[hh] = jnp.concatenate([fk, extra.astype(BF16)], axis=1)
        fkf = fk.astype(F32)
        k2 = jnp.max(jnp.sum(fkf * fkf, axis=1, keepdims=True), axis=0, keepdims=True)
        knorm = jnp.where(lane[0:1] == hh, k2, knorm)
    row8 = lax.broadcasted_iota(jnp.int32, (8, 128), 0)
    stat_ref[0] = jnp.where(row8 == 0, csum[tm - 1:tm, :], jnp.where(row8 == 1, knorm, 0.0))
    fv_ref[...] = z[:, C_FOX + 2 * MIX:C_FOX + 3 * MIX].astype(BF16)
    nsa = z[:, C_NSA:C_NSA + 256]
    nsa_ref[...] = nsa
    win = z[:, C_WIN:C_WIN + 128]
    win_ref[...] = win
    u_ref[...] = z[:, C_CONV:C_CONV + MIX] * jax.nn.sigmoid(z[:, C_CONV + MIX:C_CONV + 2 * MIX])
    cin_ref[...] = z[:, C_POOL:C_POOL + MIX]
    fkv_ref[...] = z[:, C_FOX + MIX:C_FOX + 3 * MIX]
    t = pl.program_id(0) * tm + lax.broadcasted_iota(jnp.int32, (tm, HD), 0)
    feat = lax.broadcasted_iota(jnp.int32, (tm, HD), 1)
    onehot = jnp.where(((t // SLC_BLOCK) % 64) == feat, 1.0, 0.0).astype(BF16)
    ksel_ref[...] = jnp.concatenate([nsa[:, 128:192].astype(BF16), onehot], axis=1)
    kwin_ref[...] = jnp.concatenate([win[:, 0:64].astype(BF16), jnp.zeros((tm, HD), BF16)], axis=1)
    vcat_ref[...] = jnp.concatenate([nsa[:, 192:256].astype(BF16), win[:, 64:128].astype(BF16)], axis=1)


def _inproj(x, g, w_small, fbias_row):
    m = x.shape[0]
    tm = min(TM, m)

    def row(n):
        return pl.BlockSpec((tm, n), lambda i: (i, 0))

    heads = pl.BlockSpec((NH, tm, HD), lambda i: (0, i, 0))
    wide = pl.BlockSpec((NH, tm, 128), lambda i: (0, i, 0))
    shapes = [((NH, m, HD), BF16, heads), ((m, 256), F32, row(256)), ((m, 128), F32, row(128)),
              ((m, 128), F32, row(128)), ((m, MIX), F32, row(MIX)), ((m, MIX), F32, row(MIX)),
              ((NH, m, 128), BF16, wide), ((m, 2 * MIX), F32, row(2 * MIX)),
              ((m, 128), BF16, row(128)), ((m, 128), BF16, row(128)), ((m, 128), BF16, row(128)),
              ((NH, m, 128), BF16, wide), ((m, MIX), BF16, row(MIX)),
              ((m // tm, 8, 128), F32, pl.BlockSpec((1, 8, 128), lambda i: (i, 0, 0)))]
    return pl.pallas_call(
        _inproj_kernel,
        grid=(m // tm,),
        in_specs=[row(D_MODEL), _full((1, D_MODEL)), _full((D_MODEL, N_SMALL)), _full((1, 128))],
        out_specs=[s[2] for s in shapes],
        out_shape=[jax.ShapeDtypeStruct(s[0], s[1]) for s in shapes],
        scratch_shapes=[pltpu.VMEM((1, 128), F32)],
        compiler_params=_cparams(("arbitrary",)),
        name="inproj",
    )(x, g, w_small, fbias_row)


def _merge_kernel(x_ref, g_ref, wm_ref, wb_ref, wo_ref, ya_ref, yb_ref, yc_ref, yd_ref, o_ref):
    x = x_ref[...]
    h = _rms(x, g_ref[...]).astype(BF16)
    mix = jnp.zeros_like(x)
    for n in range(N_BRANCH):
        gate = jax.nn.sigmoid(_dot(h, wm_ref[:, n * D_MODEL:(n + 1) * D_MODEL]))
        if n == 0:
            proj = jnp.zeros_like(x)
            for hh in range(NH):
                proj = proj + _dot(ya_ref[hh], wb_ref[n, hh * HD:(hh + 1) * HD, :])
        else:
            proj = _dot((yb_ref, yc_ref, yd_ref)[n - 1][...], wb_ref[n])
        mix = mix + gate * proj
    o_ref[...] = x + _dot(mix.astype(BF16), wo_ref[...])


def _merge(x, g, w_merge, w_branch, w_out, ya, yb, yc, yd):
    m = x.shape[0]
    tm = min(TM, m)
    row = pl.BlockSpec((tm, D_MODEL), lambda i: (i, 0))
    heads = pl.BlockSpec((NH, tm, HD), lambda i: (0, i, 0))
    mixrow = pl.BlockSpec((tm, MIX), lambda i: (i, 0))
    return pl.pallas_call(
        _merge_kernel,
        grid=(m // tm,),
        in_specs=[row, _full((1, D_MODEL)), _full((D_MODEL, N_BRANCH * D_MODEL)),
                  _full((N_BRANCH, MIX, D_MODEL)), _full((D_MODEL, D_MODEL)),
                  heads, mixrow, mixrow, mixrow],
        out_specs=row,
        out_shape=jax.ShapeDtypeStruct((m, D_MODEL), F32),
        compiler_params=_cparams(("arbitrary",)),
        name="merge",
    )(x, g, w_merge, w_branch, w_out, ya, yb, yc, yd)


def _bias_table_kernel(relt_ref, o_ref):
    dist = lax.broadcasted_iota(jnp.int32, (1, NDIST), 1)
    o_ref[...] = _t5_bias(relt_ref[...], dist)


def _bias_table(relt8):
    return pl.pallas_call(
        _bias_table_kernel,
        out_shape=jax.ShapeDtypeStruct((8, NDIST), F32),
        name="bias_table",
    )(relt8)


def _compress(r, pos_ref, w_ref, t, scr):
    n = r.shape[0]
    a = _dot((r + pos_ref[t, 0]).astype(BF16), w_ref[t, 0])
    b = _dot((r + pos_ref[t, 1]).astype(BF16), w_ref[t, 1])
    scr[pl.ds(0, n), :] = b
    scr[pl.ds(n, 8), :] = jnp.zeros((8, HD), F32)
    return a + scr[pl.ds(1, n), :]


def _compress_kernel(rk_ref, rv_ref, pos_ref, w_ref, kc_ref, vc_ref, scr):
    kc_ref[...] = _compress(rk_ref[...], pos_ref, w_ref, 0, scr).astype(BF16)
    vc_ref[...] = _compress(rv_ref[...], pos_ref, w_ref, 1, scr).astype(BF16)


def _compress_prompt(rk, rv, pos, w):
    r = rk.shape[0]
    return pl.pallas_call(
        _compress_kernel,
        in_specs=[_full((r, 1024), False), _full((r, 1024), False), _full((2, 2, 1, 1024), False),
                  _full((2, 2, 1024, HD), False)],
        out_specs=[_full((r, HD), False)] * 2,
        grid=(1,),
        out_shape=[jax.ShapeDtypeStruct((r, HD), BF16)] * 2,
        scratch_shapes=[pltpu.VMEM((r + 8, HD), F32)],
        compiler_params=_cparams(("arbitrary",)),
        name="nsa_compress",
    )(rk, rv, pos, w)


def _select_topk(val, n_sel):
    j = lax.broadcasted_iota(jnp.int32, val.shape, 1)
    picks = []
    for _ in range(n_sel):
        first = jnp.argmax(val, axis=1, keepdims=True).astype(jnp.int32)
        picks.append(first.astype(F32))
        val = jnp.where(j == first, -jnp.inf, val)
    return picks


def _cmp_attn_kernel(q_ref, kc_ref, vc_ref, ovl_ref, tab_ref, misc_ref, oc_ref, selb_ref,
                     *, n_rows, n_blocks):
    i = pl.program_id(0)
    q = q_ref[...].reshape(NH * TQ, HD)
    w0 = 8 * i - 8
    c0 = jnp.clip((w0 // 128) * 128, 0, n_rows - 256)
    c0 = pl.multiple_of(c0, 128)
    s = _dot_nt(q, kc_ref[...])
    n_idx = lax.broadcasted_iota(jnp.int32, s.shape, 1)
    s_far = s + jnp.where(n_idx < c0, 0.0, NEG)
    s_win = _dot_nt(q, kc_ref[pl.ds(c0, 256), :]) + tab_ref[0]
    m = jnp.maximum(jnp.max(s_far, axis=1, keepdims=True), jnp.max(s_win, axis=1, keepdims=True))
    p_far = jnp.exp(s_far - m)
    p_win = jnp.exp(s_win - m)
    l = jnp.sum(p_far, axis=1, keepdims=True) + jnp.sum(p_win, axis=1, keepdims=True)
    inv = jnp.where(m > 0.5 * NEG, 1.0 / l, 0.0)
    pb_far = (p_far * inv).astype(BF16)
    pb_win = (p_win * inv).astype(BF16)
    o = _dot(pb_far, vc_ref[...]) + _dot(pb_win, vc_ref[pl.ds(c0, 256), :])
    impf = _dot(pb_far, ovl_ref[...]) + _dot(pb_win, ovl_ref[pl.ds(c0, 256), :])
    imp = impf[0:TQ] + impf[TQ:2 * TQ] + impf[2 * TQ:3 * TQ] + impf[3 * TQ:4 * TQ]
    gates = misc_ref[...]
    for hh in range(NH):
        oc_ref[hh] = o[hh * TQ:(hh + 1) * TQ] * gates[:, hh:hh + 1]
    j = lax.broadcasted_iota(jnp.int32, imp.shape, 1)
    qpos = i * TQ + lax.broadcasted_iota(jnp.int32, imp.shape, 0)
    qblk = qpos // SLC_BLOCK
    forced = (j == 0) | (j == qblk) | (j == qblk - 1)
    val = jnp.where(forced, BIG, jnp.where(j * SLC_BLOCK <= qpos, imp, -BIG))
    jf = j.astype(F32)
    selb = jnp.full(imp.shape, NEG, F32)
    for first in _select_topk(val, min(N_SELECT, n_blocks)):
        selb = jnp.where(jf == first, 0.0, selb)
    selb_ref[...] = selb.astype(BF16)


def _cmp_attn(qh, kc, vc, ovl, tab, misc):
    l = qh.shape[1]
    r = kc.shape[0]
    ns = ovl.shape[1]

    def variant(i):
        w0 = 8 * i - 8
        c0 = jnp.clip((w0 // 128) * 128, 0, r - 256)
        return ((w0 - c0 + 8) // 8, 0, 0)

    heads = pl.BlockSpec((NH, TQ, HD), lambda i: (0, i, 0))
    return pl.pallas_call(
        functools.partial(_cmp_attn_kernel, n_rows=r, n_blocks=ns),
        grid=(l // TQ,),
        in_specs=[heads, _full((r, HD)), _full((r, HD)), _full((r, ns)),
                  pl.BlockSpec((1, NH * TQ, 256), variant),
                  pl.BlockSpec((TQ, 128), lambda i: (i, 0))],
        out_specs=[heads, pl.BlockSpec((TQ, ns), lambda i: (i, 0))],
        out_shape=[jax.ShapeDtypeStruct((NH, l, HD), F32), jax.ShapeDtypeStruct((l, ns), BF16)],
        compiler_params=_cparams(("arbitrary",)),
        name="nsa_cmp_attn",
    )(qh, kc, vc, ovl, tab, misc)


def _flash_step(carry, s, v):
    m, l, acc = carry
    m_new = jnp.maximum(m, jnp.max(s, axis=1, keepdims=True))
    alpha = jnp.exp(m - m_new)
    p = jnp.exp(s - m_new)
    l = alpha * l + jnp.sum(p, axis=1, keepdims=True)
    acc = alpha * acc + _dot(p.astype(BF16), v)
    return m_new, l, acc


def _flash_init(rows, width):
    return (jnp.full((rows, 1), NEG, F32), jnp.zeros((rows, 1), F32), jnp.zeros((rows, width), F32))


def _nsa_attn_kernel(q_ref, selb_ref, ksel_ref, kwin_ref, vcat_ref, tnear_ref, twin_ref, misc_ref, oc_ref,
                     ya_ref, qx_ref, *, n_super):
    i = pl.program_id(0)
    q0 = i * TQ
    for jj in range(n_super):
        sb = selb_ref[:, jj * 64:(jj + 1) * 64]
        for hh in range(NH):
            qx_ref[jj, hh * TQ:(hh + 1) * TQ, :] = jnp.concatenate([q_ref[hh], sb], axis=1)

    ks = jnp.maximum(q0 - TQ, 0)
    n_full = ks // TK

    def chunk(j):
        k0 = pl.multiple_of(j * TK, TK)
        s = _dot_nt(qx_ref[j // (SUPER // TK)], ksel_ref[pl.ds(k0, TK), :])
        return s, vcat_ref[pl.ds(WINDOW + k0, TK), :]

    def sel_body(j, carry):
        s, v = chunk(j)
        return _flash_step(carry, s, v)

    carry = lax.fori_loop(0, n_full, sel_body, _flash_init(NH * TQ, 128))
    s, v = chunk(n_full)
    col = n_full * TK + lax.broadcasted_iota(jnp.int32, (1, TK), 1)
    carry = _flash_step(carry, s + jnp.where(col < ks, 0.0, NEG), v)
    parts = []
    for half in range(2):
        kh = pl.multiple_of(ks + half * TQ, TQ)
        parts.append(_dot_nt(qx_ref[kh // SUPER], ksel_ref[pl.ds(kh, TQ), :]))
    s = jnp.concatenate(parts, axis=1) + tnear_ref[jnp.minimum(i, 1)]
    _, l_s, acc_s = _flash_step(carry, s, vcat_ref[pl.ds(pl.multiple_of(WINDOW + ks, TQ), 2 * TQ), :])

    w0 = pl.multiple_of(q0, TQ)
    wpos = q0 - WINDOW + lax.broadcasted_iota(jnp.int32, (1, WINDOW + TQ), 1)
    s_w = _dot_nt(qx_ref[0], kwin_ref[pl.ds(w0, WINDOW + TQ), :]) + twin_ref[...] + jnp.where(wpos < 0, NEG, 0.0)
    p_w = jnp.exp(s_w - jnp.max(s_w, axis=1, keepdims=True))
    l_w = jnp.sum(p_w, axis=1, keepdims=True)
    acc_w = _dot(p_w.astype(BF16), vcat_ref[pl.ds(w0, WINDOW + TQ), :])

    o_s = acc_s[:, 0:HD] / l_s
    o_w = acc_w[:, HD:2 * HD] / l_w
    gates = misc_ref[...]
    for hh in range(NH):
        rows = slice(hh * TQ, (hh + 1) * TQ)
        y = oc_ref[hh] + gates[:, NH + hh:NH + hh + 1] * o_s[rows] + gates[:, 2 * NH + hh:2 * NH + hh + 1] * o_w[rows]
        ya_ref[hh] = y.astype(BF16)


def _nsa_attn(qh, selb, ksel, kwin, vcat, tnear, twin, misc, oc):
    l = qh.shape[1]
    ns = selb.shape[1]
    n_super = ns // 64
    heads = pl.BlockSpec((NH, TQ, HD), lambda i: (0, i, 0))
    return pl.pallas_call(
        functools.partial(_nsa_attn_kernel, n_super=n_super),
        grid=(l // TQ,),
        in_specs=[heads, pl.BlockSpec((TQ, ns), lambda i: (i, 0)),
                  _full((l, 128)), _full((l + WINDOW, 128)), _full((l + WINDOW, 128)),
                  _full((2, NH * TQ, 2 * TQ)), _full((NH * TQ, WINDOW + TQ)),
                  pl.BlockSpec((TQ, 128), lambda i: (i, 0)), heads],
        out_specs=heads,
        out_shape=jax.ShapeDtypeStruct((NH, l, HD), BF16),
        scratch_shapes=[pltpu.VMEM((n_super, NH * TQ, 128), BF16)],
        compiler_params=_cparams(("arbitrary",)),
        name="nsa_attn",
    )(qh, selb, ksel, kwin, vcat, tnear, twin, misc, oc)


def _fox_attn_kernel(bnd_ref, q_ref, k_ref, v_ref, o_ref):
    pair = pl.program_id(0)
    i = pl.program_id(1)
    q0 = i * TF
    n_full = q0 // TKF
    qs = [q_ref[0], q_ref[1]]
    reach = []
    for hh in range(2):
        qf = qs[hh][:, 0:HD].astype(F32)
        reach.append(jnp.sqrt(jnp.sum(qf * qf, axis=1, keepdims=True)) * bnd_ref[2 * pair + hh, 0])

    def tile(j):
        k0 = pl.multiple_of(j * TKF, TKF)
        return [_dot_nt(qs[hh], k_ref[hh, pl.ds(k0, TKF), :]) for hh in range(2)], v_ref[pl.ds(k0, TKF), :]

    ss, v = tile(n_full)
    key = n_full * TKF + lax.broadcasted_iota(jnp.int32, ss[0].shape, 1)
    row = q0 + lax.broadcasted_iota(jnp.int32, ss[0].shape, 0)
    carry = tuple(_flash_step(_flash_init(TF, 128), jnp.where(key <= row, ss[hh], NEG), v) for hh in range(2))

    def reachable(j, carry):
        j = jnp.maximum(j, 0)
        gap = [jnp.max(reach[hh] + bnd_ref[2 * pair + hh, 1 + j] - carry[hh][0]) for hh in range(2)]
        return jnp.maximum(gap[0], gap[1]) > -FOX_CUT

    def body(state):
        j, _, carry = state
        ss, v = tile(j)
        carry = tuple(_flash_step(carry[hh], ss[hh], v) for hh in range(2))
        return j - 1, jnp.logical_and(j >= 1, reachable(j - 1, carry)), carry

    start = (n_full - 1, jnp.logical_and(n_full >= 1, reachable(n_full - 1, carry)), carry)
    _, _, carry = lax.while_loop(lambda state: state[1], body, start)
    outs = [carry[hh][2] / carry[hh][1] for hh in range(2)]
    lane = lax.broadcasted_iota(jnp.int32, outs[0].shape, 1)
    o_ref[...] = jnp.where(lane < HD, outs[0], outs[1]).astype(BF16)


def _fox_attn(bounds, fqx, fkx, fv):
    l = fqx.shape[1]
    return pl.pallas_call(
        _fox_attn_kernel,
        grid=(NH // 2, l // TF),
        in_specs=[pl.BlockSpec(memory_space=pltpu.SMEM),
                  pl.BlockSpec((2, TF, 128), lambda p, i: (p, i, 0)),
                  pl.BlockSpec((2, l, 128), lambda p, i: (p, 0, 0), pipeline_mode=pl.Buffered(1)),
                  pl.BlockSpec((l, 128), lambda p, i: (0, p), pipeline_mode=pl.Buffered(1))],
        out_specs=pl.BlockSpec((TF, 128), lambda p, i: (i, p)),
        out_shape=jax.ShapeDtypeStruct((l, MIX), BF16),
        compiler_params=_cparams(("arbitrary", "arbitrary")),
        name="fox_attn",
    )(bounds, fqx, fkx, fv)


def _layernorm_silu(y, ln_ref):
    mu = jnp.mean(y, axis=-1, keepdims=True)
    d = y - mu
    var = jnp.mean(d * d, axis=-1, keepdims=True)
    z = d * lax.rsqrt(var + EPS) * ln_ref[0] + ln_ref[1]
    return z * jax.nn.sigmoid(z)


def _pool_select(sums, counts, u):
    lane = lax.broadcasted_iota(jnp.int32, u.shape, 1)
    group = MIX // len(POOL_WINDOWS)
    out = sums[-1] / counts[-1]
    for g in range(len(POOL_WINDOWS) - 2, -1, -1):
        out = jnp.where(lane < (g + 1) * group, sums[g] / counts[g], out)
    return out - u


def _convpool_kernel(u_ref, uh_ref, up_ref, c_ref, ch_ref, cp_ref, cw_ref, cb_ref, ln_ref, pw_ref, ps_ref,
                     yb_ref, yc_ref, ext_ref, *, pos0):
    i = pl.program_id(0)
    tm = u_ref.shape[0]
    ext_ref[pl.ds(0, HALO), :] = jnp.where(i == 0, up_ref[...], uh_ref[...])
    ext_ref[pl.ds(HALO, tm), :] = u_ref[...]
    acc = jnp.zeros((tm, MIX), F32) + cb_ref[...]
    for w in range(CONV_WIDTH):
        acc = acc + ext_ref[pl.ds(HALO - CONV_HIST + w, tm), :] * cw_ref[pl.ds(w, 1), :]
    yb_ref[...] = _layernorm_silu(acc, ln_ref).astype(BF16)
    c = c_ref[...]
    ext_ref[pl.ds(0, HALO), :] = jnp.where(i == 0, cp_ref[...], ch_ref[...])
    ext_ref[pl.ds(HALO, tm), :] = c
    pos = pos0 + i * tm + lax.broadcasted_iota(jnp.int32, (tm, 1), 0)
    run = c
    sums, counts = [], []
    for k in range(1, max(POOL_WINDOWS)):
        run = run + ext_ref[pl.ds(HALO - k, tm), :]
        if k + 1 in POOL_WINDOWS:
            sums.append(run)
            counts.append(jnp.minimum(k + 1, pos + 1).astype(F32))
    pooled = _pool_select(sums, counts, c).astype(BF16)
    yc_ref[...] = (_dot(pooled, pw_ref[...]) * ps_ref[...]).astype(BF16)


def _convpool(u, u_past, c, c_past, cw, cb, ln, pw_bd, ps, pos0):
    l = u.shape[0]
    tm = min(TM, l)
    nh = tm // HALO
    row = pl.BlockSpec((tm, MIX), lambda i: (i, 0))
    halo = pl.BlockSpec((HALO, MIX), lambda i: (jnp.maximum(i * nh - 1, 0), 0))
    return pl.pallas_call(
        functools.partial(_convpool_kernel, pos0=pos0),
        grid=(l // tm,),
        in_specs=[row, halo, _full((HALO, MIX)), row, halo, _full((HALO, MIX)),
                  _full((HALO, MIX)), _full((1, MIX)), _full((2, 1, MIX)), _full((MIX, MIX)), _full((1, MIX))],
        out_specs=[row, row],
        out_shape=[jax.ShapeDtypeStruct((l, MIX), BF16)] * 2,
        scratch_shapes=[pltpu.VMEM((tm + HALO, MIX), F32)],
        compiler_params=_cparams(("arbitrary",)),
        name="convpool",
    )(u, u, u_past, c, c, c_past, cw, cb, ln, pw_bd, ps)


def _convpool_step_kernel(u_ref, up_ref, c_ref, cp_ref, cw_ref, cb_ref, ln_ref, pw_ref, ps_ref, yb_ref, yc_ref,
                          *, pos0):
    u = u_ref[...]
    acc = cb_ref[...] + u * cw_ref[pl.ds(CONV_WIDTH - 1, 1), :]
    for w in range(CONV_HIST):
        acc = acc + up_ref[w] * cw_ref[pl.ds(w, 1), :]
    yb_ref[...] = _layernorm_silu(acc, ln_ref).astype(BF16)
    c = c_ref[...]
    run = c
    sums, counts = [], []
    for k in range(1, max(POOL_WINDOWS)):
        run = run + cp_ref[POOL_HIST - k]
        if k + 1 in POOL_WINDOWS:
            sums.append(run)
            counts.append(float(min(k + 1, pos0 + 1)))
    pooled = _pool_select(sums, counts, c).astype(BF16)
    yc_ref[...] = (_dot(pooled, pw_ref[...]) * ps_ref[...]).astype(BF16)


def _convpool_step(u, u_past, c, c_past, cw, cb, ln, pw_bd, ps, pos0):
    b = u.shape[0]
    return pl.pallas_call(
        functools.partial(_convpool_step_kernel, pos0=pos0),
        out_shape=[jax.ShapeDtypeStruct((b, MIX), BF16)] * 2,
        name="convpool_step",
    )(u, u_past, c, c_past, cw, cb, ln, pw_bd, ps)


def _nsa_paged_cmp_kernel(pt_ref, *refs, n_pages, past):
    pages = refs[:PG]
    (q_ref, pos_ref, w_ref, ovl_ref, bias_ref, misc_ref, oc_ref, idx_ref, rk_ref, rv_ref, scr, tok_ref) = refs[PG:]
    p = pl.program_id(1)
    for g in range(PG):
        r0 = pl.multiple_of((p * PG + g) * 8, 8)
        tok_ref[...] = pages[g][0, 0].reshape(2 * HD, PAGE).T
        for t in range(CMP_STRIDE):
            rows = tok_ref[pl.ds(t, 8, stride=CMP_STRIDE), :]
            rk_ref[pl.ds(r0, 8), t * HD:(t + 1) * HD] = rows[:, 0:HD]
            rv_ref[pl.ds(r0, 8), t * HD:(t + 1) * HD] = rows[:, HD:2 * HD]

    @pl.when(p == n_pages // PG - 1)
    def _():
        kc = _compress(rk_ref[...], pos_ref, w_ref, 0, scr).astype(BF16)
        vc = _compress(rv_ref[...], pos_ref, w_ref, 1, scr).astype(BF16)
        q = q_ref[0]
        s = _dot_nt(q, kc) + bias_ref[...]
        m = jnp.max(s, axis=1, keepdims=True)
        e = jnp.exp(s - m)
        pr = (e / jnp.sum(e, axis=1, keepdims=True)).astype(BF16)
        o = _dot(pr, vc)
        gates = misc_ref[0]
        oc_ref[0] = o * gates[:, 0:1]
        impf = _dot(pr, ovl_ref[...])
        imp = jnp.sum(impf[0:NH], axis=0, keepdims=True)
        ns = imp.shape[1]
        j = lax.broadcasted_iota(jnp.int32, imp.shape, 1)
        qblk = past // SLC_BLOCK
        forced = (j == 0) | (j == qblk) | (j == qblk - 1)
        val = jnp.where(forced, BIG, jnp.where(j * SLC_BLOCK <= past, imp, -BIG))
        val = jnp.where(j <= qblk, val, -jnp.inf)
        lane = lax.broadcasted_iota(jnp.int32, (1, 128), 1)
        idx = jnp.zeros((1, 128), F32)
        for t, first in enumerate(_select_topk(val, min(N_SELECT, qblk + 1))):
            idx = jnp.where(lane == t, first, idx)
        idx_ref[0] = idx.astype(jnp.int32)


def _nsa_paged_cmp(page_table, cache, layer, q8, pos, w, ovl, bias, misc8, past):
    bsz, n_pages = page_table.shape
    n_rows = past // CMP_STRIDE
    ns = ovl.shape[1]

    def page_spec(g):
        return pl.BlockSpec((1, 1, 2, HD, PAGE), lambda b, p, pt, _g=g: (layer, pt[b, p * PG + _g], 0, 0, 0))

    def const(shape):
        nd = len(shape)
        return pl.BlockSpec(shape, lambda b, p, pt, _nd=nd: (0,) * _nd)

    def per_b(shape):
        return pl.BlockSpec(shape, lambda b, p, pt: (b, 0, 0))

    grid_spec = pltpu.PrefetchScalarGridSpec(
        num_scalar_prefetch=1,
        grid=(bsz, n_pages // PG),
        in_specs=[page_spec(g) for g in range(PG)] + [
            per_b((1, 8, HD)), const((2, 2, 1, 1024)), const((2, 2, 1024, HD)), const((n_rows, ns)),
            const((8, n_rows)), per_b((1, 8, 128))],
        out_specs=[per_b((1, 8, HD)), per_b((1, 1, 128))],
        scratch_shapes=[pltpu.VMEM((n_rows, 1024), F32), pltpu.VMEM((n_rows, 1024), F32),
                        pltpu.VMEM((n_rows + 8, HD), F32), pltpu.VMEM((PAGE, 2 * HD), F32)],
    )
    return pl.pallas_call(
        functools.partial(_nsa_paged_cmp_kernel, n_pages=n_pages, past=past),
        grid_spec=grid_spec,
        out_shape=[jax.ShapeDtypeStruct((bsz, 8, HD), F32), jax.ShapeDtypeStruct((bsz, 1, 128), jnp.int32)],
        compiler_params=_cparams(("arbitrary", "arbitrary")),
        name="nsa_paged_cmp",
    )(page_table, *([cache] * PG), q8, pos, w, ovl, bias, misc8)


def _nsa_paged_attn_kernel(idx_ref, *refs, past):
    blocks = refs[:N_SELECT]
    (q_ref, new_ref, win_ref, neww_ref, relt_ref, wbias_ref, misc_ref, oc_ref, ya_ref) = refs[N_SELECT:]
    b = pl.program_id(0)
    q = q_ref[0]
    qf = q.astype(F32)

    def own_logit(k_row):
        return jnp.sum(qf * k_row.astype(BF16).astype(F32), axis=1, keepdims=True)

    def attend(s, s_own, values, v_own):
        m = jnp.maximum(jnp.max(s, axis=1, keepdims=True), s_own)
        e = jnp.exp(s - m)
        e_own = jnp.exp(s_own - m)
        inv = 1.0 / (jnp.sum(e, axis=1, keepdims=True) + e_own)
        pr = (e * inv).astype(BF16)
        o = (e_own * inv) * v_own.astype(BF16).astype(F32)
        for c, v in enumerate(values):
            o = o + _dot_nt(pr[:, c * PAGE:(c + 1) * PAGE], v)
        return o

    lane = lax.broadcasted_iota(jnp.int32, (1, PAGE), 1)
    s_parts, pos_parts, hide_parts, v_parts = [], [], [], []
    for t in range(N_SELECT):
        j = idx_ref[b, N_SELECT + t]
        s_parts.append(_dot(q, blocks[t][0, 0, 0].astype(BF16)))
        v_parts.append(blocks[t][0, 0, 1].astype(BF16))
        pos_parts.append(jnp.minimum(j // 2, past // PAGE - 1) * PAGE + lane)
        half = jnp.where(j < past // SLC_BLOCK, j % 2, 2)
        hide_parts.append(jnp.where(lane // SLC_BLOCK == half, 0.0, NEG))
    pos = jnp.concatenate(pos_parts, axis=1)
    s = jnp.concatenate(s_parts, axis=1) + _t5_bias(relt_ref[...], past - pos) + jnp.concatenate(hide_parts, axis=1)
    new = new_ref[0]
    bias0 = _t5_bias(relt_ref[...], jnp.zeros((1, PAGE), jnp.int32))[:, 0:1]
    o_s = attend(s, own_logit(new[:, 2 * HD:3 * HD]) + bias0, v_parts, new[:, 3 * HD:4 * HD])
    n_win = win_ref.shape[4]
    s_w = _dot(q, win_ref[0, 0, 0].astype(BF16)) + wbias_ref[...]
    vw = [win_ref[0, 0, 1, :, c * PAGE:(c + 1) * PAGE].astype(BF16) for c in range(n_win // PAGE)]
    neww = neww_ref[0]
    o_w = attend(s_w, own_logit(neww[:, 0:HD]) + bias0, vw, neww[:, HD:2 * HD])
    gates = misc_ref[0]
    ya_ref[0] = (oc_ref[0] + gates[:, 1:2] * o_s + gates[:, 2:3] * o_w).astype(BF16)


def _nsa_paged_attn(sel, cache, layer, q8, nsa_new, win_state, win_new, relt8, wbias, misc8, oc, past):
    bsz = sel.shape[0]
    n_win = win_state.shape[4]

    def blk_spec(t):
        return pl.BlockSpec((1, 1, 2, HD, PAGE), lambda b, ix, _t=t: (layer, ix[b, _t], 1, 0, 0))

    def const(shape):
        nd = len(shape)
        return pl.BlockSpec(shape, lambda b, ix, _nd=nd: (0,) * _nd)

    def per_b(shape):
        return pl.BlockSpec(shape, lambda b, ix: (b, 0, 0))

    win_spec = pl.BlockSpec((1, 1, 2, HD, n_win), lambda b, ix: (layer, b, 0, 0, 0))
    grid_spec = pltpu.PrefetchScalarGridSpec(
        num_scalar_prefetch=1,
        grid=(bsz,),
        in_specs=[blk_spec(t) for t in range(N_SELECT)] + [
            per_b((1, 8, HD)), per_b((1, 1, 256)), win_spec, per_b((1, 1, 128)),
            const((8, N_BUCKETS)), const((8, n_win)), per_b((1, 8, 128)), per_b((1, 8, HD))],
        out_specs=per_b((1, 8, HD)),
    )
    return pl.pallas_call(
        functools.partial(_nsa_paged_attn_kernel, past=past),
        grid_spec=grid_spec,
        out_shape=jax.ShapeDtypeStruct((bsz, 8, HD), BF16),
        compiler_params=_cparams(("arbitrary",)),
        name="nsa_paged_attn",
    )(sel, *([cache] * N_SELECT), q8, nsa_new, win_state, win_new, relt8, wbias, misc8, oc)


def _fox_paged_kernel(pt_ref, *refs, n_pages):
    kv_pages = refs[:PG]
    lf_pages = refs[PG:2 * PG]
    (qbd_ref, new_ref, lfnew_ref, o_ref, m_ref, l_ref, acc_ref, car_ref) = refs[2 * PG:]
    p = pl.program_id(1)
    qbd = qbd_ref[0]

    @pl.when(p == 0)
    def _():
        new = new_ref[0]
        kn = new[:, 0:MIX].astype(BF16).astype(F32)
        m_ref[...] = jnp.sum(qbd.astype(F32) * kn, axis=1, keepdims=True)
        l_ref[...] = jnp.ones((8, 1), F32)
        acc_ref[...] = jnp.broadcast_to(new[:, MIX:2 * MIX].astype(BF16).astype(F32), (8, MIX))
        car_ref[...] = lfnew_ref[0]

    later = jnp.where(lax.broadcasted_iota(jnp.int32, (PAGE, PAGE), 0) > lax.broadcasted_iota(jnp.int32, (PAGE, PAGE), 1),
                      1.0, 0.0).astype(BF16)
    carry = car_ref[...]
    s_parts = [None] * PG
    for g in range(PG - 1, -1, -1):
        lf = lf_pages[g][0, 0]
        kt = kv_pages[g][0, 0, 0].astype(BF16)
        s_parts[g] = _dot(qbd, kt) + carry + _dot01(lf, later)
        carry = carry + jnp.sum(lf, axis=1, keepdims=True)
    car_ref[...] = carry
    s = jnp.concatenate(s_parts, axis=1)
    m_old = m_ref[...]
    m_new = jnp.maximum(m_old, jnp.max(s, axis=1, keepdims=True))
    alpha = jnp.exp(m_old - m_new)
    pr = jnp.exp(s - m_new)
    l_ref[...] = alpha * l_ref[...] + jnp.sum(pr, axis=1, keepdims=True)
    acc = alpha * acc_ref[...]
    prb = pr.astype(BF16)
    for g in range(PG):
        acc = acc + _dot_nt(prb[:, g * PAGE:(g + 1) * PAGE], kv_pages[g][0, 0, 1].astype(BF16))
    acc_ref[...] = acc
    m_ref[...] = m_new

    @pl.when(p == n_pages // PG - 1)
    def _():
        out = acc / l_ref[...]
        own = lax.broadcasted_iota(jnp.int32, out.shape, 1) // HD == lax.broadcasted_iota(jnp.int32, out.shape, 0)
        o_ref[0] = jnp.sum(jnp.where(own, out, 0.0), axis=0, keepdims=True).astype(BF16)


def _fox_paged(page_table, cache_kv, cache_lf, layer, qbd, kv_new, lf_new):
    bsz, n_pages = page_table.shape
    nchunk = n_pages // PG

    def kv_spec(g):
        return pl.BlockSpec((1, 1, 2, MIX, PAGE),
                            lambda b, p, pt, _g=g: (layer, pt[b, (nchunk - 1 - p) * PG + _g], 0, 0, 0))

    def lf_spec(g):
        return pl.BlockSpec((1, 1, 8, PAGE), lambda b, p, pt, _g=g: (layer, pt[b, (nchunk - 1 - p) * PG + _g], 0, 0))

    def per_b(shape):
        return pl.BlockSpec(shape, lambda b, p, pt: (b, 0, 0))

    grid_spec = pltpu.PrefetchScalarGridSpec(
        num_scalar_prefetch=1,
        grid=(bsz, nchunk),
        in_specs=[kv_spec(g) for g in range(PG)] + [lf_spec(g) for g in range(PG)] + [
            per_b((1, 8, MIX)), per_b((1, 1, 2 * MIX)), per_b((1, 8, 1))],
        out_specs=per_b((1, 1, MIX)),
        scratch_shapes=[pltpu.VMEM((8, 1), F32), pltpu.VMEM((8, 1), F32), pltpu.VMEM((8, MIX), F32),
                        pltpu.VMEM((8, 1), F32)],
    )
    return pl.pallas_call(
        functools.partial(_fox_paged_kernel, n_pages=n_pages),
        grid_spec=grid_spec,
        out_shape=jax.ShapeDtypeStruct((bsz, 1, MIX), BF16),
        compiler_params=_cparams(("arbitrary", "arbitrary")),
        name="fox_paged",
    )(page_table, *([cache_kv] * PG), *([cache_lf] * PG), qbd, kv_new, lf_new)


def _layer_params(l, norm_g, ffn_gate, ffn_up, ffn_down, w_in, fox_f_bias, nsa_cmp_pos, nsa_cmp_w,
                  conv_w, conv_b, conv_ln, pool_w, pool_scale, w_branch, w_out):
    wi = w_in[l]
    o_aq, o_akv, o_ag, o_b, o_c, o_d, o_f, o_m = np.cumsum([0, 256, 384, 12, 512, 256, 768, 4])
    misc = jnp.concatenate([wi[:, o_ag:o_ag + 12], wi[:, o_f:o_f + 4], jnp.zeros((D_MODEL, 112), F32)], axis=1)
    w_small = jnp.concatenate([wi[:, o_aq:o_aq + 256], wi[:, o_akv:o_akv + 384], misc, wi[:, o_b:o_b + 512],
                               wi[:, o_c:o_c + 256], wi[:, o_d:o_d + 768]], axis=1).astype(BF16)
    fb_row = jnp.zeros((1, 128), F32).at[0, 12:16].set(fox_f_bias[l])
    cw = nsa_cmp_w[l].reshape(2, 2, CMP_STRIDE * HD, HD).astype(BF16)
    cpos = nsa_cmp_pos[l].reshape(2, 2, 1, CMP_STRIDE * HD)
    group = MIX // len(POOL_WINDOWS)
    pw_bd = jnp.zeros((MIX, MIX), F32)
    for g in range(len(POOL_WINDOWS)):
        pw_bd = pw_bd.at[g * group:(g + 1) * group, g * group:(g + 1) * group].set(pool_w[l, g])
    cw_conv = jnp.concatenate([conv_w[l], jnp.zeros((HALO - CONV_WIDTH, MIX), F32)], axis=0)
    return dict(
        g=[norm_g[l, k].reshape(1, D_MODEL) for k in range(3)],
        ffn=[(ffn_gate[l, k].astype(BF16), ffn_up[l, k].astype(BF16), ffn_down[l, k].astype(BF16)) for k in range(2)],
        w_small=w_small, w_merge=wi[:, o_m:].astype(BF16), fb_row=fb_row, cmp_w=cw, cmp_pos=cpos,
        conv_w=cw_conv, conv_b=conv_b[l].reshape(1, MIX), conv_ln=conv_ln[l].reshape(2, 1, MIX),
        pool_w=pw_bd.astype(BF16), pool_scale=pool_scale[l].reshape(1, MIX),
        w_branch=w_branch[l].astype(BF16), w_out=w_out[l].astype(BF16))


def _overlap(n_rows, n_blocks, n_pad):
    cs = np.arange(n_rows)[:, None] * CMP_STRIDE
    j0 = np.arange(n_pad)[None, :] * SLC_BLOCK
    ov = (cs < j0 + SLC_BLOCK) & (cs + CMP_LEN > j0) & (np.arange(n_pad)[None, :] < n_blocks)
    return jnp.asarray(ov.astype(np.float32), dtype=BF16)


def _lookup(bt, dist, valid):
    d = np.clip(dist, 0, NDIST - 1)
    return jnp.where(jnp.asarray(valid)[None], bt[:, d], NEG)


def _toeplitz_kernel(rel_ref, o_ref, *, base, vstep, stride, hi):
    width = o_ref.shape[2]
    r = lax.broadcasted_iota(jnp.int32, (TQ, width), 0)
    m = lax.broadcasted_iota(jnp.int32, (TQ, width), 1)
    dist = base + vstep * pl.program_id(0) + r - stride * m
    bucket = _t5_bucket(dist)
    hide = jnp.where(dist < 0, NEG, jnp.where(dist > hi, NEG, 0.0))
    for hh in range(NH):
        far = rel_ref[N_BUCKETS - 1, hh]
        val = jnp.zeros((TQ, width), F32)
        for b in range(N_BUCKETS - 1):
            val = jnp.where(bucket == b, rel_ref[b, hh] - far, val)
        o_ref[0, hh * TQ:(hh + 1) * TQ, :] = val + hide


def _toeplitz(rel_bias, n_var, width, base, vstep, stride, hi):
    return pl.pallas_call(
        functools.partial(_toeplitz_kernel, base=base, vstep=vstep, stride=stride, hi=hi),
        grid=(n_var,),
        in_specs=[pl.BlockSpec(memory_space=pltpu.SMEM)],
        out_specs=pl.BlockSpec((1, NH * TQ, width), lambda v: (v, 0, 0)),
        out_shape=jax.ShapeDtypeStruct((n_var, NH * TQ, width), F32),
        compiler_params=_cparams(("arbitrary",)),
        name="t5_tables",
    )(rel_bias)


def _prompt_tables(rel_bias):
    big = 1 << 30
    tab_c = _toeplitz(rel_bias, 33, 256, -(CMP_LEN - 1), TQ, CMP_STRIDE, big)
    tnear = _toeplitz(rel_bias, 2, 2 * TQ, 0, TQ, 1, big)
    twin = _toeplitz(rel_bias, 1, WINDOW + TQ, WINDOW, 0, 1, WINDOW)[0]
    return tab_c, tnear, twin


def _prompt_layer(x, lp, tables, fg, final):
    l = x.shape[0]
    tab_c, tnear, twin = tables
    x = _ffn(x, lp["g"][0], *lp["ffn"][0], fg)
    (qn, nsa_kv, win_kv, misc, u, cin, fq, fox_kv, ksel, kwin, vcat, fk, fv, stat) = _inproj(x, lp["g"][1], lp["w_small"], lp["fb_row"])
    n_rows = l // CMP_STRIDE
    ns = l // SLC_BLOCK
    rk = nsa_kv[:, 0:HD].reshape(n_rows, CMP_STRIDE * HD)
    rv = nsa_kv[:, HD:2 * HD].reshape(n_rows, CMP_STRIDE * HD)
    kc, vc = _compress_prompt(rk, rv, lp["cmp_pos"], lp["cmp_w"])
    oc, selb = _cmp_attn(qn, kc, vc, _overlap(n_rows, ns, ns), tab_c, misc)
    front = jnp.zeros((WINDOW, 128), BF16)
    ya = _nsa_attn(qn, selb, ksel, jnp.concatenate([front, kwin]), jnp.concatenate([front, vcat]), tnear, twin, misc, oc)
    zeros = jnp.zeros((HALO, MIX), F32)
    yb, yc = _convpool(u, zeros, cin, zeros, lp["conv_w"], lp["conv_b"], lp["conv_ln"], lp["pool_w"], lp["pool_scale"], 0)
    logf = misc[:, 12:16]
    per_chunk = TKF // min(TM, l)
    key_reach = jnp.sqrt(jnp.max(stat[:, 1, 0:NH], axis=0)) * 1.001
    forgot = -stat[per_chunk - 1::per_chunk, 0, 3 * NH:4 * NH]
    yd = _fox_attn(jnp.concatenate([key_reach[:, None], forgot.T], axis=1), fq, fk, fv)
    x = _merge(x, lp["g"][1], lp["w_merge"], lp["w_branch"], lp["w_out"], ya, yb, yc, yd)
    x = _ffn(x, lp["g"][2], *lp["ffn"][1], fg, final)
    state = (nsa_kv, fox_kv, logf, win_kv[l - min(WINDOW, l):], u[l - CONV_HIST:], cin[l - POOL_HIST:])
    return x, state


def _sample_layer(x, lp, relt8, bt, fg, final, layer, page_table, nsa_pages, fox_pages, lf_pages, win_rows, st_win,
                  st_conv, st_pool):
    bsz = x.shape[0]
    n_pages = page_table.shape[1]
    past = n_pages * PAGE
    x = _ffn(x, lp["g"][0], *lp["ffn"][0], fg)
    (qn, nsa_kv, win_kv, misc, u, cin, fq, fox_kv, _, _, _, _, _, _) = _inproj(x, lp["g"][1], lp["w_small"], lp["fb_row"])
    pad4 = lambda a: jnp.concatenate([a, jnp.zeros((bsz, 8 - NH) + a.shape[2:], a.dtype)], axis=1)
    q8 = pad4(jnp.transpose(qn, (1, 0, 2)))
    gates = misc[:, 0:3 * NH].reshape(bsz, 3, NH)
    misc8 = jnp.zeros((bsz, 8, 128), F32).at[:, 0:NH, 0:3].set(jnp.transpose(gates, (0, 2, 1)))
    n_rows = past // CMP_STRIDE
    ns = past // SLC_BLOCK + 1
    ns_pad = -(-ns // 128) * 128
    c_end = np.arange(n_rows) * CMP_STRIDE + CMP_LEN - 1
    valid_c = (c_end <= past) & (np.arange(n_rows) < n_rows - 1)
    bias_c = jnp.concatenate([_lookup(bt, past - c_end, valid_c), jnp.full((8 - NH, n_rows), NEG, F32)], axis=0)
    oc, idx = _nsa_paged_cmp(page_table, nsa_pages, layer, q8, lp["cmp_pos"], lp["cmp_w"],
                             _overlap(n_rows, ns, ns_pad), bias_c, misc8, past)
    n_win = st_win.shape[1]
    wbias = jnp.concatenate([_lookup(bt, n_win - np.arange(n_win), np.ones(n_win, bool)),
                             jnp.full((8 - NH, n_win), NEG, F32)], axis=0)
    blk = idx.reshape(bsz, 128)[:, 0:N_SELECT]
    held = jnp.minimum(blk // 2, n_pages - 1)
    pages = jnp.take_along_axis(page_table, held, axis=1)
    ya = _nsa_paged_attn(jnp.concatenate([pages, blk], axis=1), nsa_pages, layer, q8, nsa_kv.reshape(bsz, 1, 256),
                         win_rows, win_kv.reshape(bsz, 1, 128), relt8, wbias, misc8, oc, past)
    ya = jnp.transpose(ya[:, 0:NH], (1, 0, 2))
    yb, yc = _convpool_step(u, jnp.transpose(st_conv, (1, 0, 2)), cin, jnp.transpose(st_pool, (1, 0, 2)),
                            lp["conv_w"], lp["conv_b"], lp["conv_ln"], lp["pool_w"], lp["pool_scale"], past)
    logf = misc[:, 12:16]
    fq8 = pad4(jnp.transpose(fq[:, :, 0:HD], (1, 0, 2)))
    own = (np.arange(MIX)[None, :] // HD == np.arange(8)[:, None])
    qbd = jnp.where(jnp.asarray(own)[None], jnp.tile(fq8, (1, 1, NH)), jnp.zeros((), BF16))
    lf_new = pad4(logf.reshape(bsz, NH, 1))
    yd = _fox_paged(page_table, fox_pages, lf_pages, layer, qbd, fox_kv.reshape(bsz, 1, 2 * MIX), lf_new)
    yd = yd.reshape(bsz, MIX)
    x = _merge(x, lp["g"][1], lp["w_merge"], lp["w_branch"], lp["w_out"], ya, yb, yc, yd)
    x = _ffn(x, lp["g"][2], *lp["ffn"][1], fg, final)
    new_win =jnp.concatenate([st_win.reshape(bsz, n_win, 128)[:, 1:], win_kv[:, None, :]], axis=1)
    new_conv = jnp.concatenate([st_conv[:, 1:], u[:, None, :]], axis=1)
    new_pool = jnp.concatenate([st_pool[:, 1:], cin[:, None, :]], axis=1)
    state = (nsa_kv, fox_kv, logf, new_win, new_conv, new_pool)
    return x, state


def kernel(x_prompt, x_sample, cache_nsa, cache_fox_kv, cache_fox_logf, state_nsa_win, state_conv, state_pool,
           page_table, norm_g, ffn_gate, ffn_up, ffn_down, w_in, fox_f_bias, nsa_cmp_pos, nsa_cmp_w, rel_bias,
           conv_w, conv_b, conv_ln, pool_w, pool_scale, w_branch, w_out, final_norm_g):
    assert x_prompt.shape[0] == 1 and x_sample.shape[1] == 1
    depth = norm_g.shape[0]
    l = x_prompt.shape[1]
    bsz = x_sample.shape[0]
    win_keep = state_nsa_win.shape[2]
    assert l % SUPER == 0 and page_table.shape[1] * PAGE >= max(WINDOW, SUPER) and win_keep == WINDOW
    relt8 = jnp.concatenate([rel_bias.T, jnp.zeros((8 - NH, N_BUCKETS), F32)], axis=0)
    bt = _bias_table(relt8)[0:NH]
    tables = _prompt_tables(rel_bias)
    fg = final_norm_g.reshape(1, D_MODEL)
    xp = x_prompt.reshape(l, D_MODEL)
    xs = x_sample.reshape(bsz, D_MODEL)
    n_phys = cache_nsa.shape[1]
    nsa_pages = jnp.transpose(cache_nsa, (0, 1, 3, 4, 5, 2)).reshape(depth, n_phys, 4, HD, PAGE)
    fox_pages = jnp.transpose(cache_fox_kv, (0, 1, 3, 4, 5, 2)).reshape(depth, n_phys, 2, MIX, PAGE)
    lf_pages = jnp.pad(jnp.transpose(cache_fox_logf, (0, 1, 3, 2)), ((0, 0), (0, 0), (0, 8 - NH), (0, 0)))
    win_rows = jnp.transpose(state_nsa_win, (0, 1, 3, 4, 5, 2)).reshape(depth, bsz, 2, HD, win_keep)
    st_p, st_s = [], []
    for layer in range(depth):
        lp = _layer_params(layer, norm_g, ffn_gate, ffn_up, ffn_down, w_in, fox_f_bias, nsa_cmp_pos, nsa_cmp_w,
                           conv_w, conv_b, conv_ln, pool_w, pool_scale, w_branch, w_out)
        final = layer == depth - 1
        xp, sp = _prompt_layer(xp, lp, tables, fg, final)
        st_p.append(sp)
        xs, ss = _sample_layer(xs, lp, relt8, bt, fg, final, layer, page_table, nsa_pages, fox_pages, lf_pages, win_rows,
                               state_nsa_win[layer], state_conv[layer], state_pool[layer])
        st_s.append(ss)

    def stack(states, k, shape):
        return jnp.stack([s[k] for s in states]).reshape(shape)

    return (xp.reshape(1, l, D_MODEL), xs.reshape(bsz, 1, D_MODEL),
            stack(st_p, 0, (depth, 1, l, 4, 1, HD)), stack(st_s, 0, (depth, bsz, 1, 4, 1, HD)),
            stack(st_p, 1, (depth, 1, l, 2, NH, HD)), stack(st_s, 1, (depth, bsz, 1, 2, NH, HD)),
            stack(st_p, 2, (depth, 1, l, NH)), stack(st_s, 2, (depth, bsz, 1, NH)),
            stack(st_p, 3, (depth, 1, win_keep, 2, 1, HD)), stack(st_s, 3, (depth, bsz, win_keep, 2, 1, HD)),
            stack(st_p, 4, (depth, 1, CONV_HIST, MIX)), stack(st_s, 4, (depth, bsz, CONV_HIST, MIX)),
            stack(st_p, 5, (depth, 1, POOL_HIST, MIX)), stack(st_s, 5, (depth, bsz, POOL_HIST, MIX)))
```

```python
import functools
import math

import numpy as np
import jax
import jax.numpy as jnp
from jax import lax
from jax.experimental import pallas as pl
from jax.experimental.pallas import tpu as pltpu

F32 = jnp.float32
BF16 = jnp.bfloat16

D_MODEL = 1024
N_BRANCH = 4
MIX = D_MODEL // 4
HD = 64
NH = MIX // HD
CMP_STRIDE = 16
CMP_LEN = 32
SLC_BLOCK = 64
N_SELECT = 16
WINDOW = 512
N_BUCKETS = 32
REL_EXACT = 16
REL_MAX_DIST = 128
CONV_WIDTH = 31
CONV_HIST = CONV_WIDTH - 1
POOL_WINDOWS = (2, 4, 8, 16)
POOL_HIST = 15
D_FF = 2816
EPS = 1e-6
NEG = -1e30
BIG = 1e6
PAGE = 128

LANE = 128
TQ = 128
TOPK_ROWS = 512
TK = 1024
SUPER = 64 * SLC_BLOCK
TF = 256
TKF = 1024
FOX_CUT = 100.0
TM = 256
FF_CHUNK = 1408
HALO = 32
NDIST = 1024
PG = 8
VMEM_LIMIT = 56 * 1024 * 1024

C_AQ, C_NSA, C_WIN, C_MISC, C_CONV, C_POOL, C_FOX = 0, 256, 512, 640, 768, 1280, 1536
N_SMALL = 2304


def _cparams(sem):
    return pltpu.CompilerParams(dimension_semantics=sem, vmem_limit_bytes=VMEM_LIMIT)


def _full(shape, single=True):
    nd = len(shape)
    kw = dict(pipeline_mode=pl.Buffered(1)) if single else {}
    return pl.BlockSpec(shape, lambda *a, _nd=nd: (0,) * _nd, **kw)


def _rms(x, g):
    ms = jnp.mean(x * x, axis=-1, keepdims=True)
    return x * lax.rsqrt(ms + EPS) * g


def _dot(a, b):
    return jnp.dot(a, b, preferred_element_type=F32)


def _dot_nt(a, b):
    return lax.dot_general(a, b, (((1,), (1,)), ((), ())), preferred_element_type=F32)


def _split3(x):
    hi = x.astype(BF16)
    r1 = x - hi.astype(F32)
    mid = r1.astype(BF16)
    lo = (r1 - mid.astype(F32)).astype(BF16)
    return hi, mid, lo


def _dot01(x, ones_bf16):
    hi, mid, lo = _split3(x)
    return _dot(hi, ones_bf16) + _dot(mid, ones_bf16) + _dot(lo, ones_bf16)


def _dot01_l(ones_bf16, x):
    hi, mid, lo = _split3(x)
    return _dot(ones_bf16, hi) + _dot(ones_bf16, mid) + _dot(ones_bf16, lo)


def _log_sigmoid(x):
    return jnp.minimum(x, 0.0) - jnp.log1p(jnp.exp(-jnp.abs(x)))


def _t5_bucket(dist):
    n = jnp.maximum(dist, 0)
    ratio = jnp.log(jnp.maximum(n, 1).astype(F32) / REL_EXACT) / math.log(REL_MAX_DIST / REL_EXACT)
    large = REL_EXACT + (ratio * (N_BUCKETS - REL_EXACT)).astype(jnp.int32)
    return jnp.where(n < REL_EXACT, n, jnp.minimum(large, N_BUCKETS - 1))


def _t5_bias(relt, dist):
    bucket = _t5_bucket(dist)
    rows = lax.broadcasted_iota(jnp.int32, (N_BUCKETS, dist.shape[1]), 0)
    onehot = jnp.where(rows == bucket, 1.0, 0.0).astype(BF16)
    return _dot01(relt, onehot)


def _ffn_kernel(x_ref, g_ref, wg_ref, wu_ref, wd_ref, fg_ref, o_ref, *, final):
    x = x_ref[...]
    h = _rms(x, g_ref[...]).astype(BF16)
    acc = jnp.zeros_like(x)
    for c in range(D_FF // FF_CHUNK):
        sl = slice(c * FF_CHUNK, (c + 1) * FF_CHUNK)
        a = _dot(h, wg_ref[:, sl])
        b = _dot(h, wu_ref[:, sl])
        t = (a * jax.nn.sigmoid(a) * b).astype(BF16)
        acc = acc + _dot(t, wd_ref[sl, :])
    out = x + 0.5 * acc
    o_ref[...] = _rms(out, fg_ref[...]) if final else out


def _ffn(x, g, wg, wu, wd, fg, final=False):
    m = x.shape[0]
    tm = min(TM, m)
    row = pl.BlockSpec((tm, D_MODEL), lambda i: (i, 0))
    return pl.pallas_call(
        functools.partial(_ffn_kernel, final=final),
        grid=(m // tm,),
        in_specs=[row, _full((1, D_MODEL)), _full((D_MODEL, D_FF)), _full((D_MODEL, D_FF)),
                  _full((D_FF, D_MODEL)), _full((1, D_MODEL))],
        out_specs=row,
        out_shape=jax.ShapeDtypeStruct((m, D_MODEL), F32),
        compiler_params=_cparams(("arbitrary",)),
        name="ffn",
    )(x, g, wg, wu, wd, fg)


def _inproj_kernel(x_ref, g_ref, w_ref, fb_ref,
                   qn_ref, nsa_ref, win_ref, misc_ref, u_ref, cin_ref, fq_ref, fkv_ref,
                   ksel_ref, kwin_ref, vcat_ref, fk_ref, fv_ref, stat_ref, csum_ref):
    tm = x_ref.shape[0]
    h = _rms(x_ref[...], g_ref[...]).astype(BF16)
    z = _dot(h, w_ref[...])
    scale = HD ** -0.5
    zm = z[:, C_MISC:C_MISC + 128]
    lane = lax.broadcasted_iota(jnp.int32, zm.shape, 1)
    logf = _log_sigmoid(zm + fb_ref[...])
    misc_ref[...] = jnp.where(lane < 3 * NH, jax.nn.sigmoid(zm), logf)
    @pl.when(pl.program_id(0) == 0)
    def _():
        csum_ref[...] = jnp.zeros_like(csum_ref)
    tri = jnp.where(lax.broadcasted_iota(jnp.int32, (tm, tm), 1) <= lax.broadcasted_iota(jnp.int32, (tm, tm), 0),
                    1.0, 0.0).astype(BF16)
    lf = jnp.where((lane >= 3 * NH) & (lane < 4 * NH), logf, 0.0)
    csum = _dot01_l(tri, lf) + csum_ref[...]
    csum_ref[...] = csum[tm - 1:tm, :]
    feat = lax.broadcasted_iota(jnp.int32, (tm, HD), 1)
    ones3 = jnp.where(feat < 3, 1.0, 0.0).astype(BF16)
    knorm = jnp.zeros((1, 128), F32)
    for hh in range(NH):
        qn_ref[hh] = (z[:, C_AQ + HD * hh:C_AQ + HD * (hh + 1)] * scale).astype(BF16)
        fq = (z[:, C_FOX + HD * hh:C_FOX + HD * (hh + 1)] * scale).astype(BF16)
        fq_ref[hh] = jnp.concatenate([fq, ones3], axis=1)
        hi, mid, lo = (p.astype(F32) for p in _split3(-csum[:, 3 * NH + hh:3 * NH + hh + 1]))
        extra = jnp.where(feat == 0, hi, jnp.where(feat == 1, mid, jnp.where(feat == 2, lo, 0.0)))
        fk = z[:, C_FOX + MIX + HD * hh:C_FOX + MIX + HD * (hh + 1)].astype(BF16)
        fk_ref[hh] = jnp.concatenate([fk, extra.astype(BF16)], axis=1)
        fkf = fk.astype(F32)
        k2 = jnp.max(jnp.sum(fkf * fkf, axis=1, keepdims=True), axis=0, keepdims=True)
        knorm = jnp.where(lane[0:1] == hh, k2, knorm)
    row8 = lax.broadcasted_iota(jnp.int32, (8, 128), 0)
    stat_ref[0] = jnp.where(row8 == 0, csum[tm - 1:tm, :], jnp.where(row8 == 1, knorm, 0.0))
    fv_ref[...] = z[:, C_FOX + 2 * MIX:C_FOX + 3 * MIX].astype(BF16)
    nsa = z[:, C_NSA:C_NSA + 256]
    nsa_ref[...] = nsa
    win = z[:, C_WIN:C_WIN + 128]
    win_ref[...] = win
    u_ref[...] = z[:, C_CONV:C_CONV + MIX] * jax.nn.sigmoid(z[:, C_CONV + MIX:C_CONV + 2 * MIX])
    cin_ref[...] = z[:, C_POOL:C_POOL + MIX]
    fkv_ref[...] = z[:, C_FOX + MIX:C_FOX + 3 * MIX]
    t = pl.program_id(0) * tm + lax.broadcasted_iota(jnp.int32, (tm, HD), 0)
    feat = lax.broadcasted_iota(jnp.int32, (tm, HD), 1)
    onehot = jnp.where(((t // SLC_BLOCK) % 64) == feat, 1.0, 0.0).astype(BF16)
    ksel_ref[...] = jnp.concatenate([nsa[:, 128:192].astype(BF16), onehot], axis=1)
    kwin_ref[...] = jnp.concatenate([win[:, 0:64].astype(BF16), jnp.zeros((tm, HD), BF16)], axis=1)
    vcat_ref[...] = jnp.concatenate([nsa[:, 192:256].astype(BF16), win[:, 64:128].astype(BF16)], axis=1)


def _inproj(x, g, w_small, fbias_row):
    m = x.shape[0]
    tm = min(TM, m)

    def row(n):
        return pl.BlockSpec((tm, n), lambda i: (i, 0))

    heads = pl.BlockSpec((NH, tm, HD), lambda i: (0, i, 0))
    wide = pl.BlockSpec((NH, tm, 128), lambda i: (0, i, 0))
    shapes = [((NH, m, HD), BF16, heads), ((m, 256), F32, row(256)), ((m, 128), F32, row(128)),
              ((m, 128), F32, row(128)), ((m, MIX), F32, row(MIX)), ((m, MIX), F32, row(MIX)),
              ((NH, m, 128), BF16, wide), ((m, 2 * MIX), F32, row(2 * MIX)),
              ((m, 128), BF16, row(128)), ((m, 128), BF16, row(128)), ((m, 128), BF16, row(128)),
              ((NH, m, 128), BF16, wide), ((m, MIX), BF16, row(MIX)),
              ((m // tm, 8, 128), F32, pl.BlockSpec((1, 8, 128), lambda i: (i, 0, 0)))]
    return pl.pallas_call(
        _inproj_kernel,
        grid=(m // tm,),
        in_specs=[row(D_MODEL), _full((1, D_MODEL)), _full((D_MODEL, N_SMALL)), _full((1, 128))],
        out_specs=[s[2] for s in shapes],
        out_shape=[jax.ShapeDtypeStruct(s[0], s[1]) for s in shapes],
        scratch_shapes=[pltpu.VMEM((1, 128), F32)],
        compiler_params=_cparams(("arbitrary",)),
        name="inproj",
    )(x, g, w_small, fbias_row)


def _merge_kernel(x_ref, g_ref, wm_ref, wb_ref, wo_ref, ya_ref, yb_ref, yc_ref, yd_ref, o_ref):
    x = x_ref[...]
    h = _rms(x, g_ref[...]).astype(BF16)
    mix = jnp.zeros_like(x)
    for n in range(N_BRANCH):
        gate = jax.nn.sigmoid(_dot(h, wm_ref[:, n * D_MODEL:(n + 1) * D_MODEL]))
        if n == 0:
            proj = jnp.zeros_like(x)
            for hh in range(NH):
                proj = proj + _dot(ya_ref[hh], wb_ref[n, hh * HD:(hh + 1) * HD, :])
        else:
            proj = _dot((yb_ref, yc_ref, yd_ref)[n - 1][...], wb_ref[n])
        mix = mix + gate * proj
    o_ref[...] = x + _dot(mix.astype(BF16), wo_ref[...])


def _merge(x, g, w_merge, w_branch, w_out, ya, yb, yc, yd):
    m = x.shape[0]
    tm = min(TM, m)
    row = pl.BlockSpec((tm, D_MODEL), lambda i: (i, 0))
    heads = pl.BlockSpec((NH, tm, HD), lambda i: (0, i, 0))
    mixrow = pl.BlockSpec((tm, MIX), lambda i: (i, 0))
    return pl.pallas_call(
        _merge_kernel,
        grid=(m // tm,),
        in_specs=[row, _full((1, D_MODEL)), _full((D_MODEL, N_BRANCH * D_MODEL)),
                  _full((N_BRANCH, MIX, D_MODEL)), _full((D_MODEL, D_MODEL)),
                  heads, mixrow, mixrow, mixrow],
        out_specs=row,
        out_shape=jax.ShapeDtypeStruct((m, D_MODEL), F32),
        compiler_params=_cparams(("arbitrary",)),
        name="merge",
    )(x, g, w_merge, w_branch, w_out, ya, yb, yc, yd)


def _bias_table_kernel(relt_ref, o_ref):
    dist = lax.broadcasted_iota(jnp.int32, (1, NDIST), 1)
    o_ref[...] = _t5_bias(relt_ref[...], dist)


def _bias_table(relt8):
    return pl.pallas_call(
        _bias_table_kernel,
        out_shape=jax.ShapeDtypeStruct((8, NDIST), F32),
        name="bias_table",
    )(relt8)


def _compress(r, pos_ref, w_ref, t, scr):
    n = r.shape[0]
    a = _dot((r + pos_ref[t, 0]).astype(BF16), w_ref[t, 0])
    b = _dot((r + pos_ref[t, 1]).astype(BF16), w_ref[t, 1])
    scr[pl.ds(0, n), :] = b
    scr[pl.ds(n, 8), :] = jnp.zeros((8, HD), F32)
    return a + scr[pl.ds(1, n), :]


def _compress_kernel(rk_ref, rv_ref, pos_ref, w_ref, kc_ref, vc_ref, scr):
    kc_ref[...] = _compress(rk_ref[...], pos_ref, w_ref, 0, scr).astype(BF16)
    vc_ref[...] = _compress(rv_ref[...], pos_ref, w_ref, 1, scr).astype(BF16)


def _compress_prompt(rk, rv, pos, w):
    r = rk.shape[0]
    return pl.pallas_call(
        _compress_kernel,
        in_specs=[_full((r, 1024), False), _full((r, 1024), False), _full((2, 2, 1, 1024), False),
                  _full((2, 2, 1024, HD), False)],
        out_specs=[_full((r, HD), False)] * 2,
        grid=(1,),
        out_shape=[jax.ShapeDtypeStruct((r, HD), BF16)] * 2,
        scratch_shapes=[pltpu.VMEM((r + 8, HD), F32)],
        compiler_params=_cparams(("arbitrary",)),
        name="nsa_compress",
    )(rk, rv, pos, w)


def _select_topk(val, n_sel):
    j = lax.broadcasted_iota(jnp.int32, val.shape, 1)
    picks = []
    for _ in range(n_sel):
        first = jnp.argmax(val, axis=1, keepdims=True).astype(jnp.int32)
        picks.append(first.astype(F32))
        val = jnp.where(j == first, -jnp.inf, val)
    return picks


def _cmp_attn_kernel(q_ref, kc_ref, vc_ref, ovl_ref, tab_ref, misc_ref, oc_ref, val_ref, *, n_rows):
    i = pl.program_id(0)
    q = q_ref[...].reshape(NH * TQ, HD)
    w0 = 8 * i - 8
    c0 = jnp.clip((w0 // 128) * 128, 0, n_rows - 256)
    c0 = pl.multiple_of(c0, 128)
    s = _dot_nt(q, kc_ref[...])
    n_idx = lax.broadcasted_iota(jnp.int32, s.shape, 1)
    s_far = s + jnp.where(n_idx < c0, 0.0, NEG)
    s_win = _dot_nt(q, kc_ref[pl.ds(c0, 256), :]) + tab_ref[0]
    m = jnp.maximum(jnp.max(s_far, axis=1, keepdims=True), jnp.max(s_win, axis=1, keepdims=True))
    p_far = jnp.exp(s_far - m)
    p_win = jnp.exp(s_win - m)
    l = jnp.sum(p_far, axis=1, keepdims=True) + jnp.sum(p_win, axis=1, keepdims=True)
    inv = jnp.where(m > 0.5 * NEG, 1.0 / l, 0.0)
    pb_far = (p_far * inv).astype(BF16)
    pb_win = (p_win * inv).astype(BF16)
    o = _dot(pb_far, vc_ref[...]) + _dot(pb_win, vc_ref[pl.ds(c0, 256), :])
    impf = _dot(pb_far, ovl_ref[...]) + _dot(pb_win, ovl_ref[pl.ds(c0, 256), :])
    imp = impf[0:TQ] + impf[TQ:2 * TQ] + impf[2 * TQ:3 * TQ] + impf[3 * TQ:4 * TQ]
    gates = misc_ref[...]
    for hh in range(NH):
        oc_ref[hh] = o[hh * TQ:(hh + 1) * TQ] * gates[:, hh:hh + 1]
    j = lax.broadcasted_iota(jnp.int32, imp.shape, 1)
    qpos = i * TQ + lax.broadcasted_iota(jnp.int32, imp.shape, 0)
    qblk = qpos // SLC_BLOCK
    forced = (j == 0) | (j == qblk) | (j == qblk - 1)
    val_ref[...] = jnp.where(forced, BIG, jnp.where(j * SLC_BLOCK <= qpos, imp, -BIG))


def _cmp_attn(qh, kc, vc, ovl, tab, misc):
    l = qh.shape[1]
    r = kc.shape[0]
    ns = ovl.shape[1]

    def variant(i):
        w0 = 8 * i - 8
        c0 = jnp.clip((w0 // 128) * 128, 0, r - 256)
        return ((w0 - c0 + 8) // 8, 0, 0)

    heads = pl.BlockSpec((NH, TQ, HD), lambda i: (0, i, 0))
    return pl.pallas_call(
        functools.partial(_cmp_attn_kernel, n_rows=r),
        grid=(l // TQ,),
        in_specs=[heads, _full((r, HD)), _full((r, HD)), _full((r, ns)),
                  pl.BlockSpec((1, NH * TQ, 256), variant),
                  pl.BlockSpec((TQ, 128), lambda i: (i, 0))],
        out_specs=[heads, pl.BlockSpec((TQ, ns), lambda i: (i, 0))],
        out_shape=[jax.ShapeDtypeStruct((NH, l, HD), F32), jax.ShapeDtypeStruct((l, ns), F32)],
        compiler_params=_cparams(("arbitrary",)),
        name="nsa_cmp_attn",
    )(qh, kc, vc, ovl, tab, misc)


def _topk_mask_kernel(val_ref, selb_ref, *, n_sel):
    val = val_ref[...]
    jf = lax.broadcasted_iota(jnp.int32, val.shape, 1).astype(F32)
    selb = jnp.full(val.shape, NEG, F32)
    for first in _select_topk(val, n_sel):
        selb = jnp.where(jf == first, 0.0, selb)
    selb_ref[...] = selb.astype(BF16)


def _topk_mask(val):
    l, ns = val.shape
    tr = min(TOPK_ROWS, l)
    return pl.pallas_call(
        functools.partial(_topk_mask_kernel, n_sel=min(N_SELECT, ns)),
        grid=(l // tr,),
        in_specs=[pl.BlockSpec((tr, ns), lambda i: (i, 0))],
        out_specs=pl.BlockSpec((tr, ns), lambda i: (i, 0)),
        out_shape=jax.ShapeDtypeStruct((l, ns), BF16),
        compiler_params=_cparams(("arbitrary",)),
        name="nsa_topk",
    )(val)


def _topk_index_kernel(val_ref, idx_ref, *, n_sel):
    lane = lax.broadcasted_iota(jnp.int32, idx_ref.shape, 1)
    idx = jnp.zeros(idx_ref.shape, F32)
    for t, first in enumerate(_select_topk(val_ref[...], n_sel)):
        idx = jnp.where(lane == t, first, idx)
    idx_ref[...] = idx.astype(jnp.int32)


def _topk_index(val, n_sel):
    return pl.pallas_call(
        functools.partial(_topk_index_kernel, n_sel=n_sel),
        out_shape=jax.ShapeDtypeStruct((val.shape[0], 128), jnp.int32),
        name="nsa_topk_index",
    )(val)


def _flash_step(carry, s, v):
    m, l, acc = carry
    m_new = jnp.maximum(m, jnp.max(s, axis=1, keepdims=True))
    alpha = jnp.exp(m - m_new)
    p = jnp.exp(s - m_new)
    l = alpha * l + jnp.sum(p, axis=1, keepdims=True)
    acc = alpha * acc + _dot(p.astype(BF16), v)
    return m_new, l, acc


def _flash_init(rows, width):
    return (jnp.full((rows, 1), NEG, F32), jnp.zeros((rows, 1), F32), jnp.zeros((rows, width), F32))


def _nsa_attn_kernel(q_ref, selb_ref, ksel_ref, kwin_ref, vcat_ref, tnear_ref, twin_ref, misc_ref, oc_ref,
                     ya_ref, qx_ref, sa_ref, sb_ref, *, n_super):
    i = pl.program_id(0)
    q0 = i * TQ
    for jj in range(n_super):
        sb = selb_ref[:, jj * 64:(jj + 1) * 64]
        for hh in range(NH):
            qx_ref[jj, hh * TQ:(hh + 1) * TQ, :] = jnp.concatenate([q_ref[hh], sb], axis=1)

    ks = jnp.maximum(q0 - TQ, 0)
    n_full = ks // TK

    def scores(j):
        k0 = pl.multiple_of(j * TK, TK)
        return _dot_nt(qx_ref[j // (SUPER // TK)], ksel_ref[pl.ds(k0, TK), :])

    def values(j):
        return vcat_ref[pl.ds(pl.multiple_of(WINDOW + j * TK, TK), TK), :]

    col = n_full * TK + lax.broadcasted_iota(jnp.int32, (1, TK), 1)
    hide = jnp.where(col < ks, 0.0, NEG)
    sa_ref[...] = scores(0)

    def pair_body(jj, carry):
        j = 2 * jj
        sb_ref[...] = scores(j + 1)
        carry = _flash_step(carry, sa_ref[...], values(j))
        sa_ref[...] = scores(j + 2)
        return _flash_step(carry, sb_ref[...], values(j + 1))

    carry = lax.fori_loop(0, n_full // 2, pair_body, _flash_init(NH * TQ, 128))

    def odd_tail(carry):
        sb_ref[...] = scores(n_full)
        carry = _flash_step(carry, sa_ref[...], values(n_full - 1))
        return _flash_step(carry, sb_ref[...] + hide, values(n_full))

    def even_tail(carry):
        return _flash_step(carry, sa_ref[...] + hide, values(n_full))

    carry = lax.cond(n_full % 2 == 1, odd_tail, even_tail, carry)
    parts = []
    for half in range(2):
        kh = pl.multiple_of(ks + half * TQ, TQ)
        parts.append(_dot_nt(qx_ref[kh // SUPER], ksel_ref[pl.ds(kh, TQ), :]))
    s = jnp.concatenate(parts, axis=1) + tnear_ref[jnp.minimum(i, 1)]
    _, l_s, acc_s = _flash_step(carry, s, vcat_ref[pl.ds(pl.multiple_of(WINDOW + ks, TQ), 2 * TQ), :])

    w0 = pl.multiple_of(q0, TQ)
    wpos = q0 - WINDOW + lax.broadcasted_iota(jnp.int32, (1, WINDOW + TQ), 1)
    s_w = _dot_nt(qx_ref[0], kwin_ref[pl.ds(w0, WINDOW + TQ), :]) + twin_ref[...] + jnp.where(wpos < 0, NEG, 0.0)
    p_w = jnp.exp(s_w - jnp.max(s_w, axis=1, keepdims=True))
    l_w = jnp.sum(p_w, axis=1, keepdims=True)
    acc_w = _dot(p_w.astype(BF16), vcat_ref[pl.ds(w0, WINDOW + TQ), :])

    o_s = acc_s[:, 0:HD] / l_s
    o_w = acc_w[:, HD:2 * HD] / l_w
    gates = misc_ref[...]
    for hh in range(NH):
        rows = slice(hh * TQ, (hh + 1) * TQ)
        y = oc_ref[hh] + gates[:, NH + hh:NH + hh + 1] * o_s[rows] + gates[:, 2 * NH + hh:2 * NH + hh + 1] * o_w[rows]
        ya_ref[hh] = y.astype(BF16)


def _nsa_attn(qh, selb, ksel, kwin, vcat, tnear, twin, misc, oc):
    l = qh.shape[1]
    ns = selb.shape[1]
    n_super = ns // 64
    heads = pl.BlockSpec((NH, TQ, HD), lambda i: (0, i, 0))
    return pl.pallas_call(
        functools.partial(_nsa_attn_kernel, n_super=n_super),
        grid=(l // TQ,),
        in_specs=[heads, pl.BlockSpec((TQ, ns), lambda i: (i, 0)),
                  _full((l, 128)), _full((l + WINDOW, 128)), _full((l + WINDOW, 128)),
                  _full((2, NH * TQ, 2 * TQ)), _full((NH * TQ, WINDOW + TQ)),
                  pl.BlockSpec((TQ, 128), lambda i: (i, 0)), heads],
        out_specs=heads,
        out_shape=jax.ShapeDtypeStruct((NH, l, HD), BF16),
        scratch_shapes=[pltpu.VMEM((n_super, NH * TQ, 128), BF16), pltpu.VMEM((NH * TQ, TK), F32),
                        pltpu.VMEM((NH * TQ, TK), F32)],
        compiler_params=_cparams(("arbitrary",)),
        name="nsa_attn",
    )(qh, selb, ksel, kwin, vcat, tnear, twin, misc, oc)


def _fox_attn_kernel(bnd_ref, q_ref, k_ref, v_ref, o_ref):
    pair = pl.program_id(0)
    i = pl.program_id(1)
    q0 = i * TF
    n_full = q0 // TKF
    qs = [q_ref[0], q_ref[1]]
    reach = []
    for hh in range(2):
        qf = qs[hh][:, 0:HD].astype(F32)
        reach.append(jnp.sqrt(jnp.sum(qf * qf, axis=1, keepdims=True)) * bnd_ref[2 * pair + hh, 0])

    def tile(j):
        k0 = pl.multiple_of(j * TKF, TKF)
        return [_dot_nt(qs[hh], k_ref[hh, pl.ds(k0, TKF), :]) for hh in range(2)], v_ref[pl.ds(k0, TKF), :]

    ss, v = tile(n_full)
    key = n_full * TKF + lax.broadcasted_iota(jnp.int32, ss[0].shape, 1)
    row = q0 + lax.broadcasted_iota(jnp.int32, ss[0].shape, 0)
    carry = tuple(_flash_step(_flash_init(TF, 128), jnp.where(key <= row, ss[hh], NEG), v) for hh in range(2))

    def reachable(j, carry):
        j = jnp.maximum(j, 0)
        gap = [jnp.max(reach[hh] + bnd_ref[2 * pair + hh, 1 + j] - carry[hh][0]) for hh in range(2)]
        return jnp.maximum(gap[0], gap[1]) > -FOX_CUT

    def body(state):
        j, _, carry = state
        ss, v = tile(j)
        carry = tuple(_flash_step(carry[hh], ss[hh], v) for hh in range(2))
        return j - 1, jnp.logical_and(j >= 1, reachable(j - 1, carry)), carry

    start = (n_full - 1, jnp.logical_and(n_full >= 1, reachable(n_full - 1, carry)), carry)
    _, _, carry = lax.while_loop(lambda state: state[1], body, start)
    outs = [carry[hh][2] / carry[hh][1] for hh in range(2)]
    lane = lax.broadcasted_iota(jnp.int32, outs[0].shape, 1)
    o_ref[...] = jnp.where(lane < HD, outs[0], outs[1]).astype(BF16)


def _fox_attn(bounds, fqx, fkx, fv):
    l = fqx.shape[1]
    return pl.pallas_call(
        _fox_attn_kernel,
        grid=(NH // 2, l // TF),
        in_specs=[pl.BlockSpec(memory_space=pltpu.SMEM),
                  pl.BlockSpec((2, TF, 128), lambda p, i: (p, i, 0)),
                  pl.BlockSpec((2, l, 128), lambda p, i: (p, 0, 0), pipeline_mode=pl.Buffered(1)),
                  pl.BlockSpec((l, 128), lambda p, i: (0, p), pipeline_mode=pl.Buffered(1))],
        out_specs=pl.BlockSpec((TF, 128), lambda p, i: (i, p)),
        out_shape=jax.ShapeDtypeStruct((l, MIX), BF16),
        compiler_params=_cparams(("arbitrary", "arbitrary")),
        name="fox_attn",
    )(bounds, fqx, fkx, fv)


def _layernorm_silu(y, ln_ref):
    mu = jnp.mean(y, axis=-1, keepdims=True)
    d = y - mu
    var = jnp.mean(d * d, axis=-1, keepdims=True)
    z = d * lax.rsqrt(var + EPS) * ln_ref[0] + ln_ref[1]
    return z * jax.nn.sigmoid(z)


def _pool_select(sums, counts, u):
    lane = lax.broadcasted_iota(jnp.int32, u.shape, 1)
    group = MIX // len(POOL_WINDOWS)
    out = sums[-1] / counts[-1]
    for g in range(len(POOL_WINDOWS) - 2, -1, -1):
        out = jnp.where(lane < (g + 1) * group, sums[g] / counts[g], out)
    return out - u


def _convpool_kernel(u_ref, uh_ref, up_ref, c_ref, ch_ref, cp_ref, cw_ref, cb_ref, ln_ref, pw_ref, ps_ref,
                     yb_ref, yc_ref, ext_ref, *, pos0):
    i = pl.program_id(0)
    tm = u_ref.shape[0]
    ext_ref[pl.ds(0, HALO), :] = jnp.where(i == 0, up_ref[...], uh_ref[...])
    ext_ref[pl.ds(HALO, tm), :] = u_ref[...]
    acc = jnp.zeros((tm, MIX), F32) + cb_ref[...]
    for w in range(CONV_WIDTH):
        acc = acc + ext_ref[pl.ds(HALO - CONV_HIST + w, tm), :] * cw_ref[pl.ds(w, 1), :]
    yb_ref[...] = _layernorm_silu(acc, ln_ref).astype(BF16)
    c = c_ref[...]
    ext_ref[pl.ds(0, HALO), :] = jnp.where(i == 0, cp_ref[...], ch_ref[...])
    ext_ref[pl.ds(HALO, tm), :] = c
    pos = pos0 + i * tm + lax.broadcasted_iota(jnp.int32, (tm, 1), 0)
    run = c
    sums, counts = [], []
    for k in range(1, max(POOL_WINDOWS)):
        run = run + ext_ref[pl.ds(HALO - k, tm), :]
        if k + 1 in POOL_WINDOWS:
            sums.append(run)
            counts.append(jnp.minimum(k + 1, pos + 1).astype(F32))
    pooled = _pool_select(sums, counts, c).astype(BF16)
    yc_ref[...] = (_dot(pooled, pw_ref[...]) * ps_ref[...]).astype(BF16)


def _convpool(u, u_past, c, c_past, cw, cb, ln, pw_bd, ps, pos0):
    l = u.shape[0]
    tm = min(TM, l)
    nh = tm // HALO
    row = pl.BlockSpec((tm, MIX), lambda i: (i, 0))
    halo = pl.BlockSpec((HALO, MIX), lambda i: (jnp.maximum(i * nh - 1, 0), 0))
    return pl.pallas_call(
        functools.partial(_convpool_kernel, pos0=pos0),
        grid=(l // tm,),
        in_specs=[row, halo, _full((HALO, MIX)), row, halo, _full((HALO, MIX)),
                  _full((HALO, MIX)), _full((1, MIX)), _full((2, 1, MIX)), _full((MIX, MIX)), _full((1, MIX))],
        out_specs=[row, row],
        out_shape=[jax.ShapeDtypeStruct((l, MIX), BF16)] * 2,
        scratch_shapes=[pltpu.VMEM((tm + HALO, MIX), F32)],
        compiler_params=_cparams(("arbitrary",)),
        name="convpool",
    )(u, u, u_past, c, c, c_past, cw, cb, ln, pw_bd, ps)


def _convpool_step_kernel(u_ref, up_ref, c_ref, cp_ref, cw_ref, cb_ref, ln_ref, pw_ref, ps_ref, yb_ref, yc_ref,
                          *, pos0):
    u = u_ref[...]
    acc = cb_ref[...] + u * cw_ref[pl.ds(CONV_WIDTH - 1, 1), :]
    for w in range(CONV_HIST):
        acc = acc + up_ref[w] * cw_ref[pl.ds(w, 1), :]
    yb_ref[...] = _layernorm_silu(acc, ln_ref).astype(BF16)
    c = c_ref[...]
    run = c
    sums, counts = [], []
    for k in range(1, max(POOL_WINDOWS)):
        run = run + cp_ref[POOL_HIST - k]
        if k + 1 in POOL_WINDOWS:
            sums.append(run)
            counts.append(float(min(k + 1, pos0 + 1)))
    pooled = _pool_select(sums, counts, c).astype(BF16)
    yc_ref[...] = (_dot(pooled, pw_ref[...]) * ps_ref[...]).astype(BF16)


def _convpool_step(u, u_past, c, c_past, cw, cb, ln, pw_bd, ps, pos0):
    b = u.shape[0]
    return pl.pallas_call(
        functools.partial(_convpool_step_kernel, pos0=pos0),
        out_shape=[jax.ShapeDtypeStruct((b, MIX), BF16)] * 2,
        name="convpool_step",
    )(u, u_past, c, c_past, cw, cb, ln, pw_bd, ps)


def _nsa_paged_cmp_kernel(pt_ref, *refs, n_pages, past):
    pages = refs[:PG]
    (q_ref, pos_ref, w_ref, ovl_ref, bias_ref, misc_ref, oc_ref, val_ref, rk_ref, rv_ref, scr, tok_ref) = refs[PG:]
    p = pl.program_id(1)
    for g in range(PG):
        r0 = pl.multiple_of((p * PG + g) * 8, 8)
        tok_ref[...] = pages[g][0, 0].reshape(2 * HD, PAGE).T
        for t in range(CMP_STRIDE):
            rows = tok_ref[pl.ds(t, 8, stride=CMP_STRIDE), :]
            rk_ref[pl.ds(r0, 8), t * HD:(t + 1) * HD] = rows[:, 0:HD]
            rv_ref[pl.ds(r0, 8), t * HD:(t + 1) * HD] = rows[:, HD:2 * HD]

    @pl.when(p == n_pages // PG - 1)
    def _():
        kc = _compress(rk_ref[...], pos_ref, w_ref, 0, scr).astype(BF16)
        vc = _compress(rv_ref[...], pos_ref, w_ref, 1, scr).astype(BF16)
        q = q_ref[0]
        s = _dot_nt(q, kc) + bias_ref[...]
        m = jnp.max(s, axis=1, keepdims=True)
        e = jnp.exp(s - m)
        pr = (e / jnp.sum(e, axis=1, keepdims=True)).astype(BF16)
        o = _dot(pr, vc)
        gates = misc_ref[0]
        oc_ref[0] = o * gates[:, 0:1]
        impf = _dot(pr, ovl_ref[...])
        imp = jnp.sum(impf[0:NH], axis=0, keepdims=True)
        ns = imp.shape[1]
        j = lax.broadcasted_iota(jnp.int32, imp.shape, 1)
        qblk = past // SLC_BLOCK
        forced = (j == 0) | (j == qblk) | (j == qblk - 1)
        val = jnp.where(forced, BIG, jnp.where(j * SLC_BLOCK <= past, imp, -BIG))
        val_ref[0] = jnp.where(j <= qblk, val, -jnp.inf)


def _nsa_paged_cmp(page_table, cache, layer, q8, pos, w, ovl, bias, misc8, past):
    bsz, n_pages = page_table.shape
    n_rows = past // CMP_STRIDE
    ns = ovl.shape[1]

    def page_spec(g):
        return pl.BlockSpec((1, 1, 2, HD, PAGE), lambda b, p, pt, _g=g: (layer, pt[b, p * PG + _g], 0, 0, 0))

    def const(shape):
        nd = len(shape)
        return pl.BlockSpec(shape, lambda b, p, pt, _nd=nd: (0,) * _nd)

    def per_b(shape):
        return pl.BlockSpec(shape, lambda b, p, pt: (b, 0, 0))

    grid_spec = pltpu.PrefetchScalarGridSpec(
        num_scalar_prefetch=1,
        grid=(bsz, n_pages // PG),
        in_specs=[page_spec(g) for g in range(PG)] + [
            per_b((1, 8, HD)), const((2, 2, 1, 1024)), const((2, 2, 1024, HD)), const((n_rows, ns)),
            const((8, n_rows)), per_b((1, 8, 128))],
        out_specs=[per_b((1, 8, HD)), per_b((1, 1, ns))],
        scratch_shapes=[pltpu.VMEM((n_rows, 1024), F32), pltpu.VMEM((n_rows, 1024), F32),
                        pltpu.VMEM((n_rows + 8, HD), F32), pltpu.VMEM((PAGE, 2 * HD), F32)],
    )
    return pl.pallas_call(
        functools.partial(_nsa_paged_cmp_kernel, n_pages=n_pages, past=past),
        grid_spec=grid_spec,
        out_shape=[jax.ShapeDtypeStruct((bsz, 8, HD), F32), jax.ShapeDtypeStruct((bsz, 1, ns), F32)],
        compiler_params=_cparams(("arbitrary", "arbitrary")),
        name="nsa_paged_cmp",
    )(page_table, *([cache] * PG), q8, pos, w, ovl, bias, misc8)


def _nsa_paged_attn_kernel(idx_ref, *refs, past):
    blocks = refs[:N_SELECT]
    (q_ref, new_ref, win_ref, neww_ref, relt_ref, wbias_ref, misc_ref, oc_ref, ya_ref) = refs[N_SELECT:]
    b = pl.program_id(0)
    q = q_ref[0]
    qf = q.astype(F32)

    def own_logit(k_row):
        return jnp.sum(qf * k_row.astype(BF16).astype(F32), axis=1, keepdims=True)

    def attend(s, s_own, values, v_own):
        m = jnp.maximum(jnp.max(s, axis=1, keepdims=True), s_own)
        e = jnp.exp(s - m)
        e_own = jnp.exp(s_own - m)
        inv = 1.0 / (jnp.sum(e, axis=1, keepdims=True) + e_own)
        pr = (e * inv).astype(BF16)
        o = (e_own * inv) * v_own.astype(BF16).astype(F32)
        for c, v in enumerate(values):
            o = o + _dot_nt(pr[:, c * PAGE:(c + 1) * PAGE], v)
        return o

    lane = lax.broadcasted_iota(jnp.int32, (1, PAGE), 1)
    s_parts, pos_parts, hide_parts, v_parts = [], [], [], []
    for t in range(N_SELECT):
        j = idx_ref[b, N_SELECT + t]
        s_parts.append(_dot(q, blocks[t][0, 0, 0].astype(BF16)))
        v_parts.append(blocks[t][0, 0, 1].astype(BF16))
        pos_parts.append(jnp.minimum(j // 2, past // PAGE - 1) * PAGE + lane)
        half = jnp.where(j < past // SLC_BLOCK, j % 2, 2)
        hide_parts.append(jnp.where(lane // SLC_BLOCK == half, 0.0, NEG))
    pos = jnp.concatenate(pos_parts, axis=1)
    s = jnp.concatenate(s_parts, axis=1) + _t5_bias(relt_ref[...], past - pos) + jnp.concatenate(hide_parts, axis=1)
    new = new_ref[0]
    bias0 = _t5_bias(relt_ref[...], jnp.zeros((1, PAGE), jnp.int32))[:, 0:1]
    o_s = attend(s, own_logit(new[:, 2 * HD:3 * HD]) + bias0, v_parts, new[:, 3 * HD:4 * HD])
    n_win = win_ref.shape[4]
    s_w = _dot(q, win_ref[0, 0, 0].astype(BF16)) + wbias_ref[...]
    vw = [win_ref[0, 0, 1, :, c * PAGE:(c + 1) * PAGE].astype(BF16) for c in range(n_win // PAGE)]
    neww = neww_ref[0]
    o_w = attend(s_w, own_logit(neww[:, 0:HD]) + bias0, vw, neww[:, HD:2 * HD])
    gates = misc_ref[0]
    ya_ref[0] = (oc_ref[0] + gates[:, 1:2] * o_s + gates[:, 2:3] * o_w).astype(BF16)


def _nsa_paged_attn(sel, cache, layer, q8, nsa_new, win_state, win_new, relt8, wbias, misc8, oc, past):
    bsz = sel.shape[0]
    n_win = win_state.shape[4]

    def blk_spec(t):
        return pl.BlockSpec((1, 1, 2, HD, PAGE), lambda b, ix, _t=t: (layer, ix[b, _t], 1, 0, 0))

    def const(shape):
        nd = len(shape)
        return pl.BlockSpec(shape, lambda b, ix, _nd=nd: (0,) * _nd)

    def per_b(shape):
        return pl.BlockSpec(shape, lambda b, ix: (b, 0, 0))

    win_spec = pl.BlockSpec((1, 1, 2, HD, n_win), lambda b, ix: (layer, b, 0, 0, 0))
    grid_spec = pltpu.PrefetchScalarGridSpec(
        num_scalar_prefetch=1,
        grid=(bsz,),
        in_specs=[blk_spec(t) for t in range(N_SELECT)] + [
            per_b((1, 8, HD)), per_b((1, 1, 256)), win_spec, per_b((1, 1, 128)),
            const((8, N_BUCKETS)), const((8, n_win)), per_b((1, 8, 128)), per_b((1, 8, HD))],
        out_specs=per_b((1, 8, HD)),
    )
    return pl.pallas_call(
        functools.partial(_nsa_paged_attn_kernel, past=past),
        grid_spec=grid_spec,
        out_shape=jax.ShapeDtypeStruct((bsz, 8, HD), BF16),
        compiler_params=_cparams(("arbitrary",)),
        name="nsa_paged_attn",
    )(sel, *([cache] * N_SELECT), q8, nsa_new, win_state, win_new, relt8, wbias, misc8, oc)


def _fox_paged_kernel(pt_ref, *refs, n_pages):
    kv_pages = refs[:PG]
    lf_pages = refs[PG:2 * PG]
    (qbd_ref, new_ref, lfnew_ref, o_ref, m_ref, l_ref, acc_ref, car_ref) = refs[2 * PG:]
    p = pl.program_id(1)
    qbd = qbd_ref[0]

    @pl.when(p == 0)
    def _():
        new = new_ref[0]
        kn = new[:, 0:MIX].astype(BF16).astype(F32)
        m_ref[...] = jnp.sum(qbd.astype(F32) * kn, axis=1, keepdims=True)
        l_ref[...] = jnp.ones((8, 1), F32)
        acc_ref[...] = jnp.broadcast_to(new[:, MIX:2 * MIX].astype(BF16).astype(F32), (8, MIX))
        car_ref[...] = lfnew_ref[0]

    later = jnp.where(lax.broadcasted_iota(jnp.int32, (PAGE, PAGE), 0) > lax.broadcasted_iota(jnp.int32, (PAGE, PAGE), 1),
                      1.0, 0.0).astype(BF16)
    carry = car_ref[...]
    s_parts = [None] * PG
    for g in range(PG - 1, -1, -1):
        lf = lf_pages[g][0, 0]
        kt = kv_pages[g][0, 0, 0].astype(BF16)
        s_parts[g] = _dot(qbd, kt) + carry + _dot01(lf, later)
        carry = carry + jnp.sum(lf, axis=1, keepdims=True)
    car_ref[...] = carry
    s = jnp.concatenate(s_parts, axis=1)
    m_old = m_ref[...]
    m_new = jnp.maximum(m_old, jnp.max(s, axis=1, keepdims=True))
    alpha = jnp.exp(m_old - m_new)
    pr = jnp.exp(s - m_new)
    l_ref[...] = alpha * l_ref[...] + jnp.sum(pr, axis=1, keepdims=True)
    acc = alpha * acc_ref[...]
    prb = pr.astype(BF16)
    for g in range(PG):
        acc = acc + _dot_nt(prb[:, g * PAGE:(g + 1) * PAGE], kv_pages[g][0, 0, 1].astype(BF16))
    acc_ref[...] = acc
    m_ref[...] = m_new

    @pl.when(p == n_pages // PG - 1)
    def _():
        out = acc / l_ref[...]
        own = lax.broadcasted_iota(jnp.int32, out.shape, 1) // HD == lax.broadcasted_iota(jnp.int32, out.shape, 0)
        o_ref[0] = jnp.sum(jnp.where(own, out, 0.0), axis=0, keepdims=True).astype(BF16)


def _fox_paged(page_table, cache_kv, cache_lf, layer, qbd, kv_new, lf_new):
    bsz, n_pages = page_table.shape
    nchunk = n_pages // PG

    def kv_spec(g):
        return pl.BlockSpec((1, 1, 2, MIX, PAGE),
                            lambda b, p, pt, _g=g: (layer, pt[b, (nchunk - 1 - p) * PG + _g], 0, 0, 0))

    def lf_spec(g):
        return pl.BlockSpec((1, 1, 8, PAGE), lambda b, p, pt, _g=g: (layer, pt[b, (nchunk - 1 - p) * PG + _g], 0, 0))

    def per_b(shape):
        return pl.BlockSpec(shape, lambda b, p, pt: (b, 0, 0))

    grid_spec = pltpu.PrefetchScalarGridSpec(
        num_scalar_prefetch=1,
        grid=(bsz, nchunk),
        in_specs=[kv_spec(g) for g in range(PG)] + [lf_spec(g) for g in range(PG)] + [
            per_b((1, 8, MIX)), per_b((1, 1, 2 * MIX)), per_b((1, 8, 1))],
        out_specs=per_b((1, 1, MIX)),
        scratch_shapes=[pltpu.VMEM((8, 1), F32), pltpu.VMEM((8, 1), F32), pltpu.VMEM((8, MIX), F32),
                        pltpu.VMEM((8, 1), F32)],
    )
    return pl.pallas_call(
        functools.partial(_fox_paged_kernel, n_pages=n_pages),
        grid_spec=grid_spec,
        out_shape=jax.ShapeDtypeStruct((bsz, 1, MIX), BF16),
        compiler_params=_cparams(("arbitrary", "arbitrary")),
        name="fox_paged",
    )(page_table, *([cache_kv] * PG), *([cache_lf] * PG), qbd, kv_new, lf_new)


def _layer_params(l, norm_g, ffn_gate, ffn_up, ffn_down, w_in, fox_f_bias, nsa_cmp_pos, nsa_cmp_w,
                  conv_w, conv_b, conv_ln, pool_w, pool_scale, w_branch, w_out):
    wi = w_in[l]
    o_aq, o_akv, o_ag, o_b, o_c, o_d, o_f, o_m = np.cumsum([0, 256, 384, 12, 512, 256, 768, 4])
    misc = jnp.concatenate([wi[:, o_ag:o_ag + 12], wi[:, o_f:o_f + 4], jnp.zeros((D_MODEL, 112), F32)], axis=1)
    w_small = jnp.concatenate([wi[:, o_aq:o_aq + 256], wi[:, o_akv:o_akv + 384], misc, wi[:, o_b:o_b + 512],
                               wi[:, o_c:o_c + 256], wi[:, o_d:o_d + 768]], axis=1).astype(BF16)
    fb_row = jnp.zeros((1, 128), F32).at[0, 12:16].set(fox_f_bias[l])
    cw = nsa_cmp_w[l].reshape(2, 2, CMP_STRIDE * HD, HD).astype(BF16)
    cpos = nsa_cmp_pos[l].reshape(2, 2, 1, CMP_STRIDE * HD)
    group = MIX // len(POOL_WINDOWS)
    pw_bd = jnp.zeros((MIX, MIX), F32)
    for g in range(len(POOL_WINDOWS)):
        pw_bd = pw_bd.at[g * group:(g + 1) * group, g * group:(g + 1) * group].set(pool_w[l, g])
    cw_conv = jnp.concatenate([conv_w[l], jnp.zeros((HALO - CONV_WIDTH, MIX), F32)], axis=0)
    return dict(
        g=[norm_g[l, k].reshape(1, D_MODEL) for k in range(3)],
        ffn=[(ffn_gate[l, k].astype(BF16), ffn_up[l, k].astype(BF16), ffn_down[l, k].astype(BF16)) for k in range(2)],
        w_small=w_small, w_merge=wi[:, o_m:].astype(BF16), fb_row=fb_row, cmp_w=cw, cmp_pos=cpos,
        conv_w=cw_conv, conv_b=conv_b[l].reshape(1, MIX), conv_ln=conv_ln[l].reshape(2, 1, MIX),
        pool_w=pw_bd.astype(BF16), pool_scale=pool_scale[l].reshape(1, MIX),
        w_branch=w_branch[l].astype(BF16), w_out=w_out[l].astype(BF16))


def _overlap(n_rows, n_blocks, n_pad):
    cs = np.arange(n_rows)[:, None] * CMP_STRIDE
    j0 = np.arange(n_pad)[None, :] * SLC_BLOCK
    ov = (cs < j0 + SLC_BLOCK) & (cs + CMP_LEN > j0) & (np.arange(n_pad)[None, :] < n_blocks)
    return jnp.asarray(ov.astype(np.float32), dtype=BF16)


def _lookup(bt, dist, valid):
    d = np.clip(dist, 0, NDIST - 1)
    return jnp.where(jnp.asarray(valid)[None], bt[:, d], NEG)


def _toeplitz_kernel(rel_ref, o_ref, *, base, vstep, stride, hi):
    width = o_ref.shape[2]
    r = lax.broadcasted_iota(jnp.int32, (TQ, width), 0)
    m = lax.broadcasted_iota(jnp.int32, (TQ, width), 1)
    dist = base + vstep * pl.program_id(0) + r - stride * m
    bucket = _t5_bucket(dist)
    hide = jnp.where(dist < 0, NEG, jnp.where(dist > hi, NEG, 0.0))
    for hh in range(NH):
        far = rel_ref[N_BUCKETS - 1, hh]
        val = jnp.zeros((TQ, width), F32)
        for b in range(N_BUCKETS - 1):
            val = jnp.where(bucket == b, rel_ref[b, hh] - far, val)
        o_ref[0, hh * TQ:(hh + 1) * TQ, :] = val + hide


def _toeplitz(rel_bias, n_var, width, base, vstep, stride, hi):
    return pl.pallas_call(
        functools.partial(_toeplitz_kernel, base=base, vstep=vstep, stride=stride, hi=hi),
        grid=(n_var,),
        in_specs=[pl.BlockSpec(memory_space=pltpu.SMEM)],
        out_specs=pl.BlockSpec((1, NH * TQ, width), lambda v: (v, 0, 0)),
        out_shape=jax.ShapeDtypeStruct((n_var, NH * TQ, width), F32),
        compiler_params=_cparams(("arbitrary",)),
        name="t5_tables",
    )(rel_bias)


def _prompt_tables(rel_bias):
    big = 1 << 30
    tab_c = _toeplitz(rel_bias, 33, 256, -(CMP_LEN - 1), TQ, CMP_STRIDE, big)
    tnear = _toeplitz(rel_bias, 2, 2 * TQ, 0, TQ, 1, big)
    twin = _toeplitz(rel_bias, 1, WINDOW + TQ, WINDOW, 0, 1, WINDOW)[0]
    return tab_c, tnear, twin


def _prompt_layer(x, lp, tables, fg, final):
    l = x.shape[0]
    tab_c, tnear, twin = tables
    x = _ffn(x, lp["g"][0], *lp["ffn"][0], fg)
    (qn, nsa_kv, win_kv, misc, u, cin, fq, fox_kv, ksel, kwin, vcat, fk, fv, stat) = _inproj(x, lp["g"][1], lp["w_small"], lp["fb_row"])
    n_rows = l // CMP_STRIDE
    ns = l // SLC_BLOCK
    rk = nsa_kv[:, 0:HD].reshape(n_rows, CMP_STRIDE * HD)
    rv = nsa_kv[:, HD:2 * HD].reshape(n_rows, CMP_STRIDE * HD)
    kc, vc = _compress_prompt(rk, rv, lp["cmp_pos"], lp["cmp_w"])
    oc, val = _cmp_attn(qn, kc, vc, _overlap(n_rows, ns, ns), tab_c, misc)
    selb = _topk_mask(val)
    front = jnp.zeros((WINDOW, 128), BF16)
    ya = _nsa_attn(qn, selb, ksel, jnp.concatenate([front, kwin]), jnp.concatenate([front, vcat]), tnear, twin, misc, oc)
    zeros = jnp.zeros((HALO, MIX), F32)
    yb, yc = _convpool(u, zeros, cin, zeros, lp["conv_w"], lp["conv_b"], lp["conv_ln"], lp["pool_w"], lp["pool_scale"], 0)
    logf = misc[:, 12:16]
    per_chunk = TKF // min(TM, l)
    key_reach = jnp.sqrt(jnp.max(stat[:, 1, 0:NH], axis=0)) * 1.001
    forgot = -stat[per_chunk - 1::per_chunk, 0, 3 * NH:4 * NH]
    yd = _fox_attn(jnp.concatenate([key_reach[:, None], forgot.T], axis=1), fq, fk, fv)
    x = _merge(x, lp["g"][1], lp["w_merge"], lp["w_branch"], lp["w_out"], ya, yb, yc, yd)
    x = _ffn(x, lp["g"][2], *lp["ffn"][1], fg, final)
    state = (nsa_kv, fox_kv, logf, win_kv[l - min(WINDOW, l):], u[l - CONV_HIST:], cin[l - POOL_HIST:])
    return x, state


def _sample_layer(x, lp, relt8, bt, fg, final, layer, page_table, nsa_pages, fox_pages, lf_pages, win_rows, st_win,
                  st_conv, st_pool):
    bsz = x.shape[0]
    n_pages = page_table.shape[1]
    past = n_pages * PAGE
    x = _ffn(x, lp["g"][0], *lp["ffn"][0], fg)
    (qn, nsa_kv, win_kv, misc, u, cin, fq, fox_kv, _, _, _, _, _, _) = _inproj(x, lp["g"][1], lp["w_small"], lp["fb_row"])
    pad4 = lambda a: jnp.concatenate([a, jnp.zeros((bsz, 8 - NH) + a.shape[2:], a.dtype)], axis=1)
    q8 = pad4(jnp.transpose(qn, (1, 0, 2)))
    gates = misc[:, 0:3 * NH].reshape(bsz, 3, NH)
    misc8 = jnp.zeros((bsz, 8, 128), F32).at[:, 0:NH, 0:3].set(jnp.transpose(gates, (0, 2, 1)))
    n_rows = past // CMP_STRIDE
    ns = past // SLC_BLOCK + 1
    ns_pad = -(-ns // 128) * 128
    c_end = np.arange(n_rows) * CMP_STRIDE + CMP_LEN - 1
    valid_c = (c_end <= past) & (np.arange(n_rows) < n_rows - 1)
    bias_c = jnp.concatenate([_lookup(bt, past - c_end, valid_c), jnp.full((8 - NH, n_rows), NEG, F32)], axis=0)
    oc, val = _nsa_paged_cmp(page_table, nsa_pages, layer, q8, lp["cmp_pos"], lp["cmp_w"],
                             _overlap(n_rows, ns, ns_pad), bias_c, misc8, past)
    idx = _topk_index(val.reshape(bsz, ns_pad), min(N_SELECT, ns))
    n_win = st_win.shape[1]
    wbias = jnp.concatenate([_lookup(bt, n_win - np.arange(n_win), np.ones(n_win, bool)),
                             jnp.full((8 - NH, n_win), NEG, F32)], axis=0)
    blk = idx.reshape(bsz, 128)[:, 0:N_SELECT]
    held = jnp.minimum(blk // 2, n_pages - 1)
    pages = jnp.take_along_axis(page_table, held, axis=1)
    ya = _nsa_paged_attn(jnp.concatenate([pages, blk], axis=1), nsa_pages, layer, q8, nsa_kv.reshape(bsz, 1, 256),
                         win_rows, win_kv.reshape(bsz, 1, 128), relt8, wbias, misc8, oc, past)
    ya = jnp.transpose(ya[:, 0:NH], (1, 0, 2))
    yb, yc = _convpool_step(u, jnp.transpose(st_conv, (1, 0, 2)), cin, jnp.transpose(st_pool, (1, 0, 2)),
                            lp["conv_w"], lp["conv_b"], lp["conv_ln"], lp["pool_w"], lp["pool_scale"], past)
    logf = misc[:, 12:16]
    fq8 = pad4(jnp.transpose(fq[:, :, 0:HD], (1, 0, 2)))
    own = (np.arange(MIX)[None, :] // HD == np.arange(8)[:, None])
    qbd = jnp.where(jnp.asarray(own)[None], jnp.tile(fq8, (1, 1, NH)), jnp.zeros((), BF16))
    lf_new = pad4(logf.reshape(bsz, NH, 1))
    yd = _fox_paged(page_table, fox_pages, lf_pages, layer, qbd, fox_kv.reshape(bsz, 1, 2 * MIX), lf_new)
    yd = yd.reshape(bsz, MIX)
    x = _merge(x, lp["g"][1], lp["w_merge"], lp["w_branch"], lp["w_out"], ya, yb, yc, yd)
    x = _ffn(x, lp["g"][2], *lp["ffn"][1], fg, final)
    new_win =jnp.concatenate([st_win.reshape(bsz, n_win, 128)[:, 1:], win_kv[:, None, :]], axis=1)
    new_conv = jnp.concatenate([st_conv[:, 1:], u[:, None, :]], axis=1)
    new_pool = jnp.concatenate([st_pool[:, 1:], cin[:, None, :]], axis=1)
    state = (nsa_kv, fox_kv, logf, new_win, new_conv, new_pool)
    return x, state


def kernel(x_prompt, x_sample, cache_nsa, cache_fox_kv, cache_fox_logf, state_nsa_win, state_conv, state_pool,
           page_table, norm_g, ffn_gate, ffn_up, ffn_down, w_in, fox_f_bias, nsa_cmp_pos, nsa_cmp_w, rel_bias,
           conv_w, conv_b, conv_ln, pool_w, pool_scale, w_branch, w_out, final_norm_g):
    assert x_prompt.shape[0] == 1 and x_sample.shape[1] == 1
    depth = norm_g.shape[0]
    l = x_prompt.shape[1]
    bsz = x_sample.shape[0]
    win_keep = state_nsa_win.shape[2]
    assert l % SUPER == 0 and page_table.shape[1] * PAGE >= max(WINDOW, SUPER) and win_keep == WINDOW
    relt8 = jnp.concatenate([rel_bias.T, jnp.zeros((8 - NH, N_BUCKETS), F32)], axis=0)
    bt = _bias_table(relt8)[0:NH]
    tables = _prompt_tables(rel_bias)
    fg = final_norm_g.reshape(1, D_MODEL)
    xp = x_prompt.reshape(l, D_MODEL)
    xs = x_sample.reshape(bsz, D_MODEL)
    n_phys = cache_nsa.shape[1]
    nsa_pages = jnp.transpose(cache_nsa, (0, 1, 3, 4, 5, 2)).reshape(depth, n_phys, 4, HD, PAGE)
    fox_pages = jnp.transpose(cache_fox_kv, (0, 1, 3, 4, 5, 2)).reshape(depth, n_phys, 2, MIX, PAGE)
    lf_pages = jnp.pad(jnp.transpose(cache_fox_logf, (0, 1, 3, 2)), ((0, 0), (0, 0), (0, 8 - NH), (0, 0)))
    win_rows = jnp.transpose(state_nsa_win, (0, 1, 3, 4, 5, 2)).reshape(depth, bsz, 2, HD, win_keep)
    st_p, st_s = [], []
    for layer in range(depth):
        lp = _layer_params(layer, norm_g, ffn_gate, ffn_up, ffn_down, w_in, fox_f_bias, nsa_cmp_pos, nsa_cmp_w,
                           conv_w, conv_b, conv_ln, pool_w, pool_scale, w_branch, w_out)
        final = layer == depth - 1
        xp, sp = _prompt_layer(xp, lp, tables, fg, final)
        st_p.append(sp)
        xs, ss = _sample_layer(xs, lp, relt8, bt, fg, final, layer, page_table, nsa_pages, fox_pages, lf_pages, win_rows,
                               state_nsa_win[layer], state_conv[layer], state_pool[layer])
        st_s.append(ss)

    def stack(states, k, shape):
        return jnp.stack([s[k] for s in states]).reshape(shape)

    return (xp.reshape(1, l, D_MODEL), xs.reshape(bsz, 1, D_MODEL),
            stack(st_p, 0, (depth, 1, l, 4, 1, HD)), stack(st_s, 0, (depth, bsz, 1, 4, 1, HD)),
            stack(st_p, 1, (depth, 1, l, 2, NH, HD)), stack(st_s, 1, (depth, bsz, 1, 2, NH, HD)),
            stack(st_p, 2, (depth, 1, l, NH)), stack(st_s, 2, (depth, bsz, 1, NH)),
            stack(st_p, 3, (depth, 1, win_keep, 2, 1, HD)), stack(st_s, 3, (depth, bsz, win_keep, 2, 1, HD)),
            stack(st_p, 4, (depth, 1, CONV_HIST, MIX)), stack(st_s, 4, (depth, bsz, CONV_HIST, MIX)),
            stack(st_p, 5, (depth, 1, POOL_HIST, MIX)), stack(st_s, 5, (depth, bsz, POOL_HIST, MIX)))
```

```python
import functools
import math

import numpy as np
import jax
import jax.numpy as jnp
from jax import lax
from jax.experimental import pallas as pl
from jax.experimental.pallas import tpu as pltpu

F32 = jnp.float32
BF16 = jnp.bfloat16

D_MODEL = 1024
N_BRANCH = 4
MIX = D_MODEL // 4
HD = 64
NH = MIX // HD
CMP_STRIDE = 16
CMP_LEN = 32
SLC_BLOCK = 64
N_SELECT = 16
WINDOW = 512
N_BUCKETS = 32
REL_EXACT = 16
REL_MAX_DIST = 128
CONV_WIDTH = 31
CONV_HIST = CONV_WIDTH - 1
POOL_WINDOWS = (2, 4, 8, 16)
POOL_HIST = 15
D_FF = 2816
EPS = 1e-6
NEG = -1e30
BIG = 1e6
PAGE = 128

LANE = 128
TQ = 128
TOPK_ROWS = 512
TK = 1024
SUPER = 64 * SLC_BLOCK
TF = 256
TKF = 1024
FOX_CUT = 100.0
TM = 256
TM_WIDE = 512
FF_CHUNK = 1408
HALO = 32
NDIST = 1024
PG = 8
VMEM_LIMIT = 56 * 1024 * 1024

C_AQ, C_NSA, C_WIN, C_MISC, C_CONV, C_POOL, C_FOX = 0, 256, 512, 640, 768, 1280, 1536
N_SMALL = 2304


def _cparams(sem):
    return pltpu.CompilerParams(dimension_semantics=sem, vmem_limit_bytes=VMEM_LIMIT)


def _full(shape, single=True):
    nd = len(shape)
    kw = dict(pipeline_mode=pl.Buffered(1)) if single else {}
    return pl.BlockSpec(shape, lambda *a, _nd=nd: (0,) * _nd, **kw)


def _rms(x, g):
    ms = jnp.mean(x * x, axis=-1, keepdims=True)
    return x * lax.rsqrt(ms + EPS) * g


def _dot(a, b):
    return jnp.dot(a, b, preferred_element_type=F32)


def _dot_nt(a, b):
    return lax.dot_general(a, b, (((1,), (1,)), ((), ())), preferred_element_type=F32)


def _split3(x):
    hi = x.astype(BF16)
    r1 = x - hi.astype(F32)
    mid = r1.astype(BF16)
    lo = (r1 - mid.astype(F32)).astype(BF16)
    return hi, mid, lo


def _dot01(x, ones_bf16):
    hi, mid, lo = _split3(x)
    return _dot(hi, ones_bf16) + _dot(mid, ones_bf16) + _dot(lo, ones_bf16)


def _dot01_l(ones_bf16, x):
    hi, mid, lo = _split3(x)
    return _dot(ones_bf16, hi) + _dot(ones_bf16, mid) + _dot(ones_bf16, lo)


def _log_sigmoid(x):
    return jnp.minimum(x, 0.0) - jnp.log1p(jnp.exp(-jnp.abs(x)))


def _t5_bucket(dist):
    n = jnp.maximum(dist, 0)
    ratio = jnp.log(jnp.maximum(n, 1).astype(F32) / REL_EXACT) / math.log(REL_MAX_DIST / REL_EXACT)
    large = REL_EXACT + (ratio * (N_BUCKETS - REL_EXACT)).astype(jnp.int32)
    return jnp.where(n < REL_EXACT, n, jnp.minimum(large, N_BUCKETS - 1))


def _t5_bias(relt, dist):
    bucket = _t5_bucket(dist)
    rows = lax.broadcasted_iota(jnp.int32, (N_BUCKETS, dist.shape[1]), 0)
    onehot = jnp.where(rows == bucket, 1.0, 0.0).astype(BF16)
    return _dot01(relt, onehot)


def _ffn_kernel(x_ref, g_ref, wg_ref, wu_ref, wd_ref, fg_ref, o_ref, *, final):
    x = x_ref[...]
    h = _rms(x, g_ref[...]).astype(BF16)
    acc = jnp.zeros_like(x)
    for c in range(D_FF // FF_CHUNK):
        sl = slice(c * FF_CHUNK, (c + 1) * FF_CHUNK)
        a = _dot(h, wg_ref[:, sl])
        b = _dot(h, wu_ref[:, sl])
        t = (a * jax.nn.sigmoid(a) * b).astype(BF16)
        acc = acc + _dot(t, wd_ref[sl, :])
    out = x + 0.5 * acc
    o_ref[...] = _rms(out, fg_ref[...]) if final else out


def _ffn(x, g, wg, wu, wd, fg, final=False):
    m = x.shape[0]
    tm = min(TM_WIDE, m)
    row = pl.BlockSpec((tm, D_MODEL), lambda i: (i, 0))
    return pl.pallas_call(
        functools.partial(_ffn_kernel, final=final),
        grid=(m // tm,),
        in_specs=[row, _full((1, D_MODEL)), _full((D_MODEL, D_FF)), _full((D_MODEL, D_FF)),
                  _full((D_FF, D_MODEL)), _full((1, D_MODEL))],
        out_specs=row,
        out_shape=jax.ShapeDtypeStruct((m, D_MODEL), F32),
        compiler_params=_cparams(("arbitrary",)),
        name="ffn",
    )(x, g, wg, wu, wd, fg)


def _inproj_kernel(x_ref, g_ref, w_ref, fb_ref,
                   qn_ref, nsa_ref, win_ref, misc_ref, u_ref, cin_ref, fq_ref, fkv_ref,
                   ksel_ref, kwin_ref, vcat_ref, fk_ref, fv_ref, stat_ref, csum_ref):
    tm = x_ref.shape[0]
    h = _rms(x_ref[...], g_ref[...]).astype(BF16)
    z = _dot(h, w_ref[...])
    scale = HD ** -0.5
    zm = z[:, C_MISC:C_MISC + 128]
    lane = lax.broadcasted_iota(jnp.int32, zm.shape, 1)
    logf = _log_sigmoid(zm + fb_ref[...])
    misc_ref[...] = jnp.where(lane < 3 * NH, jax.nn.sigmoid(zm), logf)
    @pl.when(pl.program_id(0) == 0)
    def _():
        csum_ref[...] = jnp.zeros_like(csum_ref)
    tri = jnp.where(lax.broadcasted_iota(jnp.int32, (tm, tm), 1) <= lax.broadcasted_iota(jnp.int32, (tm, tm), 0),
                    1.0, 0.0).astype(BF16)
    lf = jnp.where((lane >= 3 * NH) & (lane < 4 * NH), logf, 0.0)
    csum = _dot01_l(tri, lf) + csum_ref[...]
    csum_ref[...] = csum[tm - 1:tm, :]
    feat = lax.broadcasted_iota(jnp.int32, (tm, HD), 1)
    ones3 = jnp.where(feat < 3, 1.0, 0.0).astype(BF16)
    knorm = jnp.zeros((1, 128), F32)
    for hh in range(NH):
        qn_ref[hh] = (z[:, C_AQ + HD * hh:C_AQ + HD * (hh + 1)] * scale).astype(BF16)
        fq = (z[:, C_FOX + HD * hh:C_FOX + HD * (hh + 1)] * scale).astype(BF16)
        fq_ref[hh] = jnp.concatenate([fq, ones3], axis=1)
        hi, mid, lo = (p.astype(F32) for p in _split3(-csum[:, 3 * NH + hh:3 * NH + hh + 1]))
        extra = jnp.where(feat == 0, hi, jnp.where(feat == 1, mid, jnp.where(feat == 2, lo, 0.0)))
        fk = z[:, C_FOX + MIX + HD * hh:C_FOX + MIX + HD * (hh + 1)].astype(BF16)
        fk_ref[hh] = jnp.concatenate([fk, extra.astype(BF16)], axis=1)
        fkf = fk.astype(F32)
        k2 = jnp.max(jnp.sum(fkf * fkf, axis=1, keepdims=True), axis=0, keepdims=True)
        knorm = jnp.where(lane[0:1] == hh, k2, knorm)
    row8 = lax.broadcasted_iota(jnp.int32, (8, 128), 0)
    stat_ref[0] = jnp.where(row8 == 0, csum[tm - 1:tm, :], jnp.where(row8 == 1, knorm, 0.0))
    fv_ref[...] = z[:, C_FOX + 2 * MIX:C_FOX + 3 * MIX].astype(BF16)
    nsa = z[:, C_NSA:C_NSA + 256]
    nsa_ref[...] = nsa
    win = z[:, C_WIN:C_WIN + 128]
    win_ref[...] = win
    u_ref[...] = z[:, C_CONV:C_CONV + MIX] * jax.nn.sigmoid(z[:, C_CONV + MIX:C_CONV + 2 * MIX])
    cin_ref[...] = z[:, C_POOL:C_POOL + MIX]
    fkv_ref[...] = z[:, C_FOX + MIX:C_FOX + 3 * MIX]
    t = pl.program_id(0) * tm + lax.broadcasted_iota(jnp.int32, (tm, HD), 0)
    feat = lax.broadcasted_iota(jnp.int32, (tm, HD), 1)
    onehot = jnp.where(((t // SLC_BLOCK) % 64) == feat, 1.0, 0.0).astype(BF16)
    ksel_ref[...] = jnp.concatenate([nsa[:, 128:192].astype(BF16), onehot], axis=1)
    kwin_ref[...] = jnp.concatenate([win[:, 0:64].astype(BF16), jnp.zeros((tm, HD), BF16)], axis=1)
    vcat_ref[...] = jnp.concatenate([nsa[:, 192:256].astype(BF16), win[:, 64:128].astype(BF16)], axis=1)


def _inproj(x, g, w_small, fbias_row):
    m = x.shape[0]
    tm = min(TM, m)

    def row(n):
        return pl.BlockSpec((tm, n), lambda i: (i, 0))

    heads = pl.BlockSpec((NH, tm, HD), lambda i: (0, i, 0))
    wide = pl.BlockSpec((NH, tm, 128), lambda i: (0, i, 0))
    shapes = [((NH, m, HD), BF16, heads), ((m, 256), F32, row(256)), ((m, 128), F32, row(128)),
              ((m, 128), F32, row(128)), ((m, MIX), F32, row(MIX)), ((m, MIX), F32, row(MIX)),
              ((NH, m, 128), BF16, wide), ((m, 2 * MIX), F32, row(2 * MIX)),
              ((m, 128), BF16, row(128)), ((m, 128), BF16, row(128)), ((m, 128), BF16, row(128)),
              ((NH, m, 128), BF16, wide), ((m, MIX), BF16, row(MIX)),
              ((m // tm, 8, 128), F32, pl.BlockSpec((1, 8, 128), lambda i: (i, 0, 0)))]
    return pl.pallas_call(
        _inproj_kernel,
        grid=(m // tm,),
        in_specs=[row(D_MODEL), _full((1, D_MODEL)), _full((D_MODEL, N_SMALL)), _full((1, 128))],
        out_specs=[s[2] for s in shapes],
        out_shape=[jax.ShapeDtypeStruct(s[0], s[1]) for s in shapes],
        scratch_shapes=[pltpu.VMEM((1, 128), F32)],
        compiler_params=_cparams(("arbitrary",)),
        name="inproj",
    )(x, g, w_small, fbias_row)


def _merge_kernel(x_ref, g_ref, wm_ref, wb_ref, wo_ref, ya_ref, yb_ref, yc_ref, yd_ref, o_ref):
    x = x_ref[...]
    h = _rms(x, g_ref[...]).astype(BF16)
    mix = jnp.zeros_like(x)
    for n in range(N_BRANCH):
        gate = jax.nn.sigmoid(_dot(h, wm_ref[:, n * D_MODEL:(n + 1) * D_MODEL]))
        if n == 0:
            proj = jnp.zeros_like(x)
            for hh in range(NH):
                proj = proj + _dot(ya_ref[hh], wb_ref[n, hh * HD:(hh + 1) * HD, :])
        else:
            proj = _dot((yb_ref, yc_ref, yd_ref)[n - 1][...], wb_ref[n])
        mix = mix + gate * proj
    o_ref[...] = x + _dot(mix.astype(BF16), wo_ref[...])


def _merge(x, g, w_merge, w_branch, w_out, ya, yb, yc, yd):
    m = x.shape[0]
    tm = min(TM_WIDE, m)
    row = pl.BlockSpec((tm, D_MODEL), lambda i: (i, 0))
    heads = pl.BlockSpec((NH, tm, HD), lambda i: (0, i, 0))
    mixrow = pl.BlockSpec((tm, MIX), lambda i: (i, 0))
    return pl.pallas_call(
        _merge_kernel,
        grid=(m // tm,),
        in_specs=[row, _full((1, D_MODEL)), _full((D_MODEL, N_BRANCH * D_MODEL)),
                  _full((N_BRANCH, MIX, D_MODEL)), _full((D_MODEL, D_MODEL)),
                  heads, mixrow, mixrow, mixrow],
        out_specs=row,
        out_shape=jax.ShapeDtypeStruct((m, D_MODEL), F32),
        compiler_params=_cparams(("arbitrary",)),
        name="merge",
    )(x, g, w_merge, w_branch, w_out, ya, yb, yc, yd)


def _bias_table_kernel(relt_ref, o_ref):
    dist = lax.broadcasted_iota(jnp.int32, (1, NDIST), 1)
    o_ref[...] = _t5_bias(relt_ref[...], dist)


def _bias_table(relt8):
    return pl.pallas_call(
        _bias_table_kernel,
        out_shape=jax.ShapeDtypeStruct((8, NDIST), F32),
        name="bias_table",
    )(relt8)


def _compress(r, pos_ref, w_ref, t, scr):
    n = r.shape[0]
    a = _dot((r + pos_ref[t, 0]).astype(BF16), w_ref[t, 0])
    b = _dot((r + pos_ref[t, 1]).astype(BF16), w_ref[t, 1])
    scr[pl.ds(0, n), :] = b
    scr[pl.ds(n, 8), :] = jnp.zeros((8, HD), F32)
    return a + scr[pl.ds(1, n), :]


def _compress_kernel(rk_ref, rv_ref, pos_ref, w_ref, kc_ref, vc_ref, scr):
    kc_ref[...] = _compress(rk_ref[...], pos_ref, w_ref, 0, scr).astype(BF16)
    vc_ref[...] = _compress(rv_ref[...], pos_ref, w_ref, 1, scr).astype(BF16)


def _compress_prompt(rk, rv, pos, w):
    r = rk.shape[0]
    return pl.pallas_call(
        _compress_kernel,
        in_specs=[_full((r, 1024), False), _full((r, 1024), False), _full((2, 2, 1, 1024), False),
                  _full((2, 2, 1024, HD), False)],
        out_specs=[_full((r, HD), False)] * 2,
        grid=(1,),
        out_shape=[jax.ShapeDtypeStruct((r, HD), BF16)] * 2,
        scratch_shapes=[pltpu.VMEM((r + 8, HD), F32)],
        compiler_params=_cparams(("arbitrary",)),
        name="nsa_compress",
    )(rk, rv, pos, w)


def _select_topk(val, n_sel):
    ns = val.shape[1]
    j = lax.broadcasted_iota(jnp.int32, val.shape, 1).astype(F32)
    picks = []
    for _ in range(n_sel):
        mx = jnp.max(val, axis=1, keepdims=True)
        first = jnp.min(jnp.where(val == mx, j, float(ns)), axis=1, keepdims=True)
        picks.append(first)
        val = jnp.where(j == first, -jnp.inf, val)
    return picks


def _cmp_attn_kernel(q_ref, kc_ref, vc_ref, ovl_ref, tab_ref, misc_ref, oc_ref, val_ref, *, n_rows):
    i = pl.program_id(0)
    q = q_ref[...].reshape(NH * TQ, HD)
    w0 = 8 * i - 8
    c0 = jnp.clip((w0 // 128) * 128, 0, n_rows - 256)
    c0 = pl.multiple_of(c0, 128)
    s = _dot_nt(q, kc_ref[...])
    n_idx = lax.broadcasted_iota(jnp.int32, s.shape, 1)
    s_far = s + jnp.where(n_idx < c0, 0.0, NEG)
    s_win = _dot_nt(q, kc_ref[pl.ds(c0, 256), :]) + tab_ref[0]
    m = jnp.maximum(jnp.max(s_far, axis=1, keepdims=True), jnp.max(s_win, axis=1, keepdims=True))
    p_far = jnp.exp(s_far - m)
    p_win = jnp.exp(s_win - m)
    l = jnp.sum(p_far, axis=1, keepdims=True) + jnp.sum(p_win, axis=1, keepdims=True)
    inv = jnp.where(m > 0.5 * NEG, 1.0 / l, 0.0)
    pb_far = (p_far * inv).astype(BF16)
    pb_win = (p_win * inv).astype(BF16)
    o = _dot(pb_far, vc_ref[...]) + _dot(pb_win, vc_ref[pl.ds(c0, 256), :])
    impf = _dot(pb_far, ovl_ref[...]) + _dot(pb_win, ovl_ref[pl.ds(c0, 256), :])
    imp = impf[0:TQ] + impf[TQ:2 * TQ] + impf[2 * TQ:3 * TQ] + impf[3 * TQ:4 * TQ]
    gates = misc_ref[...]
    for hh in range(NH):
        oc_ref[hh] = o[hh * TQ:(hh + 1) * TQ] * gates[:, hh:hh + 1]
    j = lax.broadcasted_iota(jnp.int32, imp.shape, 1)
    qpos = i * TQ + lax.broadcasted_iota(jnp.int32, imp.shape, 0)
    qblk = qpos // SLC_BLOCK
    forced = (j == 0) | (j == qblk) | (j == qblk - 1)
    val_ref[...] = jnp.where(forced, BIG, jnp.where(j * SLC_BLOCK <= qpos, imp, -BIG))


def _cmp_attn(qh, kc, vc, ovl, tab, misc):
    l = qh.shape[1]
    r = kc.shape[0]
    ns = ovl.shape[1]

    def variant(i):
        w0 = 8 * i - 8
        c0 = jnp.clip((w0 // 128) * 128, 0, r - 256)
        return ((w0 - c0 + 8) // 8, 0, 0)

    heads = pl.BlockSpec((NH, TQ, HD), lambda i: (0, i, 0))
    return pl.pallas_call(
        functools.partial(_cmp_attn_kernel, n_rows=r),
        grid=(l // TQ,),
        in_specs=[heads, _full((r, HD)), _full((r, HD)), _full((r, ns)),
                  pl.BlockSpec((1, NH * TQ, 256), variant),
                  pl.BlockSpec((TQ, 128), lambda i: (i, 0))],
        out_specs=[heads, pl.BlockSpec((TQ, ns), lambda i: (i, 0))],
        out_shape=[jax.ShapeDtypeStruct((NH, l, HD), F32), jax.ShapeDtypeStruct((l, ns), F32)],
        compiler_params=_cparams(("arbitrary",)),
        name="nsa_cmp_attn",
    )(qh, kc, vc, ovl, tab, misc)


def _topk_mask_kernel(val_ref, selb_ref, *, n_sel):
    val = val_ref[...]
    jf = lax.broadcasted_iota(jnp.int32, val.shape, 1).astype(F32)
    selb = jnp.full(val.shape, NEG, F32)
    for first in _select_topk(val, n_sel):
        selb = jnp.where(jf == first, 0.0, selb)
    selb_ref[...] = selb.astype(BF16)


def _topk_mask(val):
    l, ns = val.shape
    tr = min(TOPK_ROWS, l)
    return pl.pallas_call(
        functools.partial(_topk_mask_kernel, n_sel=min(N_SELECT, ns)),
        grid=(l // tr,),
        in_specs=[pl.BlockSpec((tr, ns), lambda i: (i, 0))],
        out_specs=pl.BlockSpec((tr, ns), lambda i: (i, 0)),
        out_shape=jax.ShapeDtypeStruct((l, ns), BF16),
        compiler_params=_cparams(("arbitrary",)),
        name="nsa_topk",
    )(val)


def _topk_index_kernel(val_ref, idx_ref, *, n_sel):
    lane = lax.broadcasted_iota(jnp.int32, idx_ref.shape, 1)
    idx = jnp.zeros(idx_ref.shape, F32)
    for t, first in enumerate(_select_topk(val_ref[...], n_sel)):
        idx = jnp.where(lane == t, first, idx)
    idx_ref[...] = idx.astype(jnp.int32)


def _topk_index(val, n_sel):
    return pl.pallas_call(
        functools.partial(_topk_index_kernel, n_sel=n_sel),
        out_shape=jax.ShapeDtypeStruct((val.shape[0], 128), jnp.int32),
        name="nsa_topk_index",
    )(val)


def _flash_step(carry, s, v):
    m, l, acc = carry
    m_new = jnp.maximum(m, jnp.max(s, axis=1, keepdims=True))
    alpha = jnp.exp(m - m_new)
    p = jnp.exp(s - m_new)
    l = alpha * l + jnp.sum(p, axis=1, keepdims=True)
    acc = alpha * acc + _dot(p.astype(BF16), v)
    return m_new, l, acc


def _flash_init(rows, width):
    return (jnp.full((rows, 1), NEG, F32), jnp.zeros((rows, 1), F32), jnp.zeros((rows, width), F32))


def _nsa_attn_kernel(q_ref, selb_ref, ksel_ref, kwin_ref, vcat_ref, tnear_ref, twin_ref, misc_ref, oc_ref,
                     ya_ref, qx_ref, sa_ref, sb_ref, *, n_super):
    i = pl.program_id(0)
    q0 = i * TQ
    for jj in range(n_super):
        sb = selb_ref[:, jj * 64:(jj + 1) * 64]
        for hh in range(NH):
            qx_ref[jj, hh * TQ:(hh + 1) * TQ, :] = jnp.concatenate([q_ref[hh], sb], axis=1)

    ks = jnp.maximum(q0 - TQ, 0)
    n_full = ks // TK

    def scores(j):
        k0 = pl.multiple_of(j * TK, TK)
        return _dot_nt(qx_ref[j // (SUPER // TK)], ksel_ref[pl.ds(k0, TK), :])

    def values(j):
        return vcat_ref[pl.ds(pl.multiple_of(WINDOW + j * TK, TK), TK), :]

    col = n_full * TK + lax.broadcasted_iota(jnp.int32, (1, TK), 1)
    hide = jnp.where(col < ks, 0.0, NEG)
    sa_ref[...] = scores(0)

    def pair_body(jj, carry):
        j = 2 * jj
        sb_ref[...] = scores(j + 1)
        carry = _flash_step(carry, sa_ref[...], values(j))
        sa_ref[...] = scores(j + 2)
        return _flash_step(carry, sb_ref[...], values(j + 1))

    carry = lax.fori_loop(0, n_full // 2, pair_body, _flash_init(NH * TQ, 128))

    def odd_tail(carry):
        sb_ref[...] = scores(n_full)
        carry = _flash_step(carry, sa_ref[...], values(n_full - 1))
        return _flash_step(carry, sb_ref[...] + hide, values(n_full))

    def even_tail(carry):
        return _flash_step(carry, sa_ref[...] + hide, values(n_full))

    carry = lax.cond(n_full % 2 == 1, odd_tail, even_tail, carry)
    parts = []
    for half in range(2):
        kh = pl.multiple_of(ks + half * TQ, TQ)
        parts.append(_dot_nt(qx_ref[kh // SUPER], ksel_ref[pl.ds(kh, TQ), :]))
    s = jnp.concatenate(parts, axis=1) + tnear_ref[jnp.minimum(i, 1)]
    _, l_s, acc_s = _flash_step(carry, s, vcat_ref[pl.ds(pl.multiple_of(WINDOW + ks, TQ), 2 * TQ), :])

    w0 = pl.multiple_of(q0, TQ)
    wpos = q0 - WINDOW + lax.broadcasted_iota(jnp.int32, (1, WINDOW + TQ), 1)
    s_w = _dot_nt(qx_ref[0], kwin_ref[pl.ds(w0, WINDOW + TQ), :]) + twin_ref[...] + jnp.where(wpos < 0, NEG, 0.0)
    p_w = jnp.exp(s_w - jnp.max(s_w, axis=1, keepdims=True))
    l_w = jnp.sum(p_w, axis=1, keepdims=True)
    acc_w = _dot(p_w.astype(BF16), vcat_ref[pl.ds(w0, WINDOW + TQ), :])

    o_s = acc_s[:, 0:HD] / l_s
    o_w = acc_w[:, HD:2 * HD] / l_w
    gates = misc_ref[...]
    for hh in range(NH):
        rows = slice(hh * TQ, (hh + 1) * TQ)
        y = oc_ref[hh] + gates[:, NH + hh:NH + hh + 1] * o_s[rows] + gates[:, 2 * NH + hh:2 * NH + hh + 1] * o_w[rows]
        ya_ref[hh] = y.astype(BF16)


def _nsa_attn(qh, selb, ksel, kwin, vcat, tnear, twin, misc, oc):
    l = qh.shape[1]
    ns = selb.shape[1]
    n_super = ns // 64
    heads = pl.BlockSpec((NH, TQ, HD), lambda i: (0, i, 0))
    return pl.pallas_call(
        functools.partial(_nsa_attn_kernel, n_super=n_super),
        grid=(l // TQ,),
        in_specs=[heads, pl.BlockSpec((TQ, ns), lambda i: (i, 0)),
                  _full((l, 128)), _full((l + WINDOW, 128)), _full((l + WINDOW, 128)),
                  _full((2, NH * TQ, 2 * TQ)), _full((NH * TQ, WINDOW + TQ)),
                  pl.BlockSpec((TQ, 128), lambda i: (i, 0)), heads],
        out_specs=heads,
        out_shape=jax.ShapeDtypeStruct((NH, l, HD), BF16),
        scratch_shapes=[pltpu.VMEM((n_super, NH * TQ, 128), BF16), pltpu.VMEM((NH * TQ, TK), F32),
                        pltpu.VMEM((NH * TQ, TK), F32)],
        compiler_params=_cparams(("arbitrary",)),
        name="nsa_attn",
    )(qh, selb, ksel, kwin, vcat, tnear, twin, misc, oc)


def _fox_attn_kernel(bnd_ref, q_ref, k_ref, v_ref, o_ref):
    pair = pl.program_id(0)
    i = pl.program_id(1)
    q0 = i * TF
    n_full = q0 // TKF
    qs = [q_ref[0], q_ref[1]]
    reach = []
    for hh in range(2):
        qf = qs[hh][:, 0:HD].astype(F32)
        reach.append(jnp.sqrt(jnp.sum(qf * qf, axis=1, keepdims=True)) * bnd_ref[2 * pair + hh, 0])

    def tile(j):
        k0 = pl.multiple_of(j * TKF, TKF)
        return [_dot_nt(qs[hh], k_ref[hh, pl.ds(k0, TKF), :]) for hh in range(2)], v_ref[pl.ds(k0, TKF), :]

    ss, v = tile(n_full)
    key = n_full * TKF + lax.broadcasted_iota(jnp.int32, ss[0].shape, 1)
    row = q0 + lax.broadcasted_iota(jnp.int32, ss[0].shape, 0)
    carry = tuple(_flash_step(_flash_init(TF, 128), jnp.where(key <= row, ss[hh], NEG), v) for hh in range(2))

    def reachable(j, carry):
        j = jnp.maximum(j, 0)
        gap = [jnp.max(reach[hh] + bnd_ref[2 * pair + hh, 1 + j] - carry[hh][0]) for hh in range(2)]
        return jnp.maximum(gap[0], gap[1]) > -FOX_CUT

    def body(state):
        j, _, carry = state
        ss, v = tile(j)
        carry = tuple(_flash_step(carry[hh], ss[hh], v) for hh in range(2))
        return j - 1, jnp.logical_and(j >= 1, reachable(j - 1, carry)), carry

    start = (n_full - 1, jnp.logical_and(n_full >= 1, reachable(n_full - 1, carry)), carry)
    _, _, carry = lax.while_loop(lambda state: state[1], body, start)
    outs = [carry[hh][2] / carry[hh][1] for hh in range(2)]
    lane = lax.broadcasted_iota(jnp.int32, outs[0].shape, 1)
    o_ref[...] = jnp.where(lane < HD, outs[0], outs[1]).astype(BF16)


def _fox_attn(bounds, fqx, fkx, fv):
    l = fqx.shape[1]
    return pl.pallas_call(
        _fox_attn_kernel,
        grid=(NH // 2, l // TF),
        in_specs=[pl.BlockSpec(memory_space=pltpu.SMEM),
                  pl.BlockSpec((2, TF, 128), lambda p, i: (p, i, 0)),
                  pl.BlockSpec((2, l, 128), lambda p, i: (p, 0, 0), pipeline_mode=pl.Buffered(1)),
                  pl.BlockSpec((l, 128), lambda p, i: (0, p), pipeline_mode=pl.Buffered(1))],
        out_specs=pl.BlockSpec((TF, 128), lambda p, i: (i, p)),
        out_shape=jax.ShapeDtypeStruct((l, MIX), BF16),
        compiler_params=_cparams(("arbitrary", "arbitrary")),
        name="fox_attn",
    )(bounds, fqx, fkx, fv)


def _layernorm_silu(y, ln_ref):
    mu = jnp.mean(y, axis=-1, keepdims=True)
    d = y - mu
    var = jnp.mean(d * d, axis=-1, keepdims=True)
    z = d * lax.rsqrt(var + EPS) * ln_ref[0] + ln_ref[1]
    return z * jax.nn.sigmoid(z)


def _pool_select(sums, counts, u):
    lane = lax.broadcasted_iota(jnp.int32, u.shape, 1)
    group = MIX // len(POOL_WINDOWS)
    out = sums[-1] / counts[-1]
    for g in range(len(POOL_WINDOWS) - 2, -1, -1):
        out = jnp.where(lane < (g + 1) * group, sums[g] / counts[g], out)
    return out - u


def _convpool_kernel(u_ref, uh_ref, up_ref, c_ref, ch_ref, cp_ref, cw_ref, cb_ref, ln_ref, pw_ref, ps_ref,
                     yb_ref, yc_ref, ext_ref, *, pos0):
    i = pl.program_id(0)
    tm = u_ref.shape[0]
    ext_ref[pl.ds(0, HALO), :] = jnp.where(i == 0, up_ref[...], uh_ref[...])
    ext_ref[pl.ds(HALO, tm), :] = u_ref[...]
    acc = jnp.zeros((tm, MIX), F32) + cb_ref[...]
    for w in range(CONV_WIDTH):
        acc = acc + ext_ref[pl.ds(HALO - CONV_HIST + w, tm), :] * cw_ref[pl.ds(w, 1), :]
    yb_ref[...] = _layernorm_silu(acc, ln_ref).astype(BF16)
    c = c_ref[...]
    ext_ref[pl.ds(0, HALO), :] = jnp.where(i == 0, cp_ref[...], ch_ref[...])
    ext_ref[pl.ds(HALO, tm), :] = c
    pos = pos0 + i * tm + lax.broadcasted_iota(jnp.int32, (tm, 1), 0)
    run = c
    sums, counts = [], []
    for k in range(1, max(POOL_WINDOWS)):
        run = run + ext_ref[pl.ds(HALO - k, tm), :]
        if k + 1 in POOL_WINDOWS:
            sums.append(run)
            counts.append(jnp.minimum(k + 1, pos + 1).astype(F32))
    pooled = _pool_select(sums, counts, c).astype(BF16)
    yc_ref[...] = (_dot(pooled, pw_ref[...]) * ps_ref[...]).astype(BF16)


def _convpool(u, u_past, c, c_past, cw, cb, ln, pw_bd, ps, pos0):
    l = u.shape[0]
    tm = min(TM, l)
    nh = tm // HALO
    row = pl.BlockSpec((tm, MIX), lambda i: (i, 0))
    halo = pl.BlockSpec((HALO, MIX), lambda i: (jnp.maximum(i * nh - 1, 0), 0))
    return pl.pallas_call(
        functools.partial(_convpool_kernel, pos0=pos0),
        grid=(l // tm,),
        in_specs=[row, halo, _full((HALO, MIX)), row, halo, _full((HALO, MIX)),
                  _full((HALO, MIX)), _full((1, MIX)), _full((2, 1, MIX)), _full((MIX, MIX)), _full((1, MIX))],
        out_specs=[row, row],
        out_shape=[jax.ShapeDtypeStruct((l, MIX), BF16)] * 2,
        scratch_shapes=[pltpu.VMEM((tm + HALO, MIX), F32)],
        compiler_params=_cparams(("arbitrary",)),
        name="convpool",
    )(u, u, u_past, c, c, c_past, cw, cb, ln, pw_bd, ps)


def _convpool_step_kernel(u_ref, up_ref, c_ref, cp_ref, cw_ref, cb_ref, ln_ref, pw_ref, ps_ref, yb_ref, yc_ref,
                          *, pos0):
    u = u_ref[...]
    acc = cb_ref[...] + u * cw_ref[pl.ds(CONV_WIDTH - 1, 1), :]
    for w in range(CONV_HIST):
        acc = acc + up_ref[w] * cw_ref[pl.ds(w, 1), :]
    yb_ref[...] = _layernorm_silu(acc, ln_ref).astype(BF16)
    c = c_ref[...]
    run = c
    sums, counts = [], []
    for k in range(1, max(POOL_WINDOWS)):
        run = run + cp_ref[POOL_HIST - k]
        if k + 1 in POOL_WINDOWS:
            sums.append(run)
            counts.append(float(min(k + 1, pos0 + 1)))
    pooled = _pool_select(sums, counts, c).astype(BF16)
    yc_ref[...] = (_dot(pooled, pw_ref[...]) * ps_ref[...]).astype(BF16)


def _convpool_step(u, u_past, c, c_past, cw, cb, ln, pw_bd, ps, pos0):
    b = u.shape[0]
    return pl.pallas_call(
        functools.partial(_convpool_step_kernel, pos0=pos0),
        out_shape=[jax.ShapeDtypeStruct((b, MIX), BF16)] * 2,
        name="convpool_step",
    )(u, u_past, c, c_past, cw, cb, ln, pw_bd, ps)


def _nsa_paged_cmp_kernel(pt_ref, *refs, n_pages, past):
    pages = refs[:PG]
    (q_ref, pos_ref, w_ref, ovl_ref, bias_ref, misc_ref, oc_ref, val_ref, r_ref, scr, tok_ref) = refs[PG:]
    p = pl.program_id(1)
    for g in range(PG):
        r0 = pl.multiple_of((p * PG + g) * 8, 8)
        tok_ref[...] = pages[g][0, 0].reshape(2 * HD, PAGE).T
        for t in range(CMP_STRIDE):
            r_ref[pl.ds(r0, 8), t * 2 * HD:(t + 1) * 2 * HD] = tok_ref[pl.ds(t, 8, stride=CMP_STRIDE), :]

    @pl.when(p == n_pages // PG - 1)
    def _():
        r = r_ref[...]
        n = r.shape[0]
        lo = _dot((r + pos_ref[0]).astype(BF16), w_ref[0])
        hi = _dot((r + pos_ref[1]).astype(BF16), w_ref[1])
        scr[pl.ds(0, n), :] = hi
        scr[pl.ds(n, 8), :] = jnp.zeros((8, 2 * HD), F32)
        kvc = (lo + scr[pl.ds(1, n), :]).astype(BF16)
        q = q_ref[0]
        s = _dot_nt(q, kvc) + bias_ref[...]
        m = jnp.max(s, axis=1, keepdims=True)
        e = jnp.exp(s - m)
        pr = (e / jnp.sum(e, axis=1, keepdims=True)).astype(BF16)
        o = _dot(pr, kvc)[:, HD:2 * HD]
        gates = misc_ref[0]
        oc_ref[0] = o * gates[:, 0:1]
        impf = _dot(pr, ovl_ref[...])
        imp = jnp.sum(impf[0:NH], axis=0, keepdims=True)
        ns = imp.shape[1]
        j = lax.broadcasted_iota(jnp.int32, imp.shape, 1)
        qblk = past // SLC_BLOCK
        forced = (j == 0) | (j == qblk) | (j == qblk - 1)
        val = jnp.where(forced, BIG, jnp.where(j * SLC_BLOCK <= past, imp, -BIG))
        val_ref[0] = jnp.where(j <= qblk, val, -jnp.inf)


def _nsa_paged_cmp(page_table, cache, layer, qx8, pos, w, ovl, bias, misc8, past):
    bsz, n_pages = page_table.shape
    n_rows = past // CMP_STRIDE
    ns = ovl.shape[1]
    wide = CMP_STRIDE * 2 * HD

    def page_spec(g):
        return pl.BlockSpec((1, 1, 2, HD, PAGE), lambda b, p, pt, _g=g: (layer, pt[b, p * PG + _g], 0, 0, 0))

    def const(shape):
        nd = len(shape)
        return pl.BlockSpec(shape, lambda b, p, pt, _nd=nd: (0,) * _nd)

    def per_b(shape):
        return pl.BlockSpec(shape, lambda b, p, pt: (b, 0, 0))

    grid_spec = pltpu.PrefetchScalarGridSpec(
        num_scalar_prefetch=1,
        grid=(bsz, n_pages // PG),
        in_specs=[page_spec(g) for g in range(PG)] + [
            per_b((1, 8, 2 * HD)), const((2, 1, wide)), const((2, wide, 2 * HD)), const((n_rows, ns)),
            const((8, n_rows)), per_b((1, 8, 128))],
        out_specs=[per_b((1, 8, HD)), per_b((1, 1, ns))],
        scratch_shapes=[pltpu.VMEM((n_rows, wide), F32), pltpu.VMEM((n_rows + 8, 2 * HD), F32),
                        pltpu.VMEM((PAGE, 2 * HD), F32)],
    )
    return pl.pallas_call(
        functools.partial(_nsa_paged_cmp_kernel, n_pages=n_pages, past=past),
        grid_spec=grid_spec,
        out_shape=[jax.ShapeDtypeStruct((bsz, 8, HD), F32), jax.ShapeDtypeStruct((bsz, 1, ns), F32)],
        compiler_params=_cparams(("arbitrary", "arbitrary")),
        name="nsa_paged_cmp",
    )(page_table, *([cache] * PG), qx8, pos, w, ovl, bias, misc8)


def _nsa_paged_attn_kernel(idx_ref, *refs, past):
    blocks = refs[:N_SELECT]
    (q_ref, new_ref, win_ref, neww_ref, relt_ref, wbias_ref, misc_ref, oc_ref, ya_ref) = refs[N_SELECT:]
    b = pl.program_id(0)
    q = q_ref[0]
    qf = q.astype(F32)

    def own_logit(k_row):
        return jnp.sum(qf * k_row.astype(BF16).astype(F32), axis=1, keepdims=True)

    def attend(s, s_own, values, v_own):
        m = jnp.maximum(jnp.max(s, axis=1, keepdims=True), s_own)
        e = jnp.exp(s - m)
        e_own = jnp.exp(s_own - m)
        inv = 1.0 / (jnp.sum(e, axis=1, keepdims=True) + e_own)
        pr = (e * inv).astype(BF16)
        o = (e_own * inv) * v_own.astype(BF16).astype(F32)
        for c, v in enumerate(values):
            o = o + _dot_nt(pr[:, c * PAGE:(c + 1) * PAGE], v)
        return o

    lane = lax.broadcasted_iota(jnp.int32, (1, PAGE), 1)
    s_parts, pos_parts, hide_parts, v_parts = [], [], [], []
    for t in range(N_SELECT):
        j = idx_ref[b, N_SELECT + t]
        s_parts.append(_dot(q, blocks[t][0, 0, 0].astype(BF16)))
        v_parts.append(blocks[t][0, 0, 1].astype(BF16))
        pos_parts.append(jnp.minimum(j // 2, past // PAGE - 1) * PAGE + lane)
        half = jnp.where(j < past // SLC_BLOCK, j % 2, 2)
        hide_parts.append(jnp.where(lane // SLC_BLOCK == half, 0.0, NEG))
    pos = jnp.concatenate(pos_parts, axis=1)
    s = jnp.concatenate(s_parts, axis=1) + _t5_bias(relt_ref[...], past - pos) + jnp.concatenate(hide_parts, axis=1)
    new = new_ref[0]
    bias0 = _t5_bias(relt_ref[...], jnp.zeros((1, PAGE), jnp.int32))[:, 0:1]
    o_s = attend(s, own_logit(new[:, 2 * HD:3 * HD]) + bias0, v_parts, new[:, 3 * HD:4 * HD])
    n_win = win_ref.shape[4]
    s_w = _dot(q, win_ref[0, 0, 0].astype(BF16)) + wbias_ref[...]
    vw = [win_ref[0, 0, 1, :, c * PAGE:(c + 1) * PAGE].astype(BF16) for c in range(n_win // PAGE)]
    neww = neww_ref[0]
    o_w = attend(s_w, own_logit(neww[:, 0:HD]) + bias0, vw, neww[:, HD:2 * HD])
    gates = misc_ref[0]
    ya_ref[0] = (oc_ref[0] + gates[:, 1:2] * o_s + gates[:, 2:3] * o_w).astype(BF16)


def _nsa_paged_attn(sel, cache, layer, q8, nsa_new, win_state, win_new, relt8, wbias, misc8, oc, past):
    bsz = sel.shape[0]
    n_win = win_state.shape[4]

    def blk_spec(t):
        return pl.BlockSpec((1, 1, 2, HD, PAGE), lambda b, ix, _t=t: (layer, ix[b, _t], 1, 0, 0))

    def const(shape):
        nd = len(shape)
        return pl.BlockSpec(shape, lambda b, ix, _nd=nd: (0,) * _nd)

    def per_b(shape):
        return pl.BlockSpec(shape, lambda b, ix: (b, 0, 0))

    win_spec = pl.BlockSpec((1, 1, 2, HD, n_win), lambda b, ix: (layer, b, 0, 0, 0))
    grid_spec = pltpu.PrefetchScalarGridSpec(
        num_scalar_prefetch=1,
        grid=(bsz,),
        in_specs=[blk_spec(t) for t in range(N_SELECT)] + [
            per_b((1, 8, HD)), per_b((1, 1, 256)), win_spec, per_b((1, 1, 128)),
            const((8, N_BUCKETS)), const((8, n_win)), per_b((1, 8, 128)), per_b((1, 8, HD))],
        out_specs=per_b((1, 8, HD)),
    )
    return pl.pallas_call(
        functools.partial(_nsa_paged_attn_kernel, past=past),
        grid_spec=grid_spec,
        out_shape=jax.ShapeDtypeStruct((bsz, 8, HD), BF16),
        compiler_params=_cparams(("arbitrary",)),
        name="nsa_paged_attn",
    )(sel, *([cache] * N_SELECT), q8, nsa_new, win_state, win_new, relt8, wbias, misc8, oc)


def _fox_paged_kernel(pt_ref, kv_hbm, lf_hbm, qcol_ref, new_ref, lfnew_ref, o_ref, kv_buf, lf_buf, sem, *,
                      layer, n_pages):
    b = pl.program_id(0)
    n_seq = pl.num_programs(0)
    n_chunk = n_pages // PG

    def copies(seq, chunk, slot):
        out = []
        for g in range(PG):
            page = pt_ref[seq, (n_chunk - 1 - chunk) * PG + g]
            out.append(pltpu.make_async_copy(kv_hbm.at[layer, page], kv_buf.at[slot, g], sem.at[slot]))
            out.append(pltpu.make_async_copy(lf_hbm.at[layer, page], lf_buf.at[slot, g], sem.at[slot]))
        return out

    @pl.when(b == 0)
    def _():
        for cp in copies(0, 0, 0):
            cp.start()

    q_col = qcol_ref[0]
    q_wide = jnp.broadcast_to(q_col, (MIX, PAGE))
    new = new_ref[0]
    row8 = lax.broadcasted_iota(jnp.int32, (8, 1), 0)

    def per_head(x):
        out = jnp.zeros((8, x.shape[1]), F32)
        for hh in range(NH):
            out = jnp.where(row8 == hh, jnp.sum(x[hh * HD:(hh + 1) * HD], axis=0, keepdims=True), out)
        return out

    def per_feature(x, width):
        return jnp.concatenate([jnp.broadcast_to(x[hh:hh + 1], (HD, width)) for hh in range(NH)], axis=0)

    m0 = per_head(q_col * new[0:MIX])
    acc0 = jnp.where(lax.broadcasted_iota(jnp.int32, (MIX, PAGE), 1) == 0, new[MIX:2 * MIX], 0.0)
    later = jnp.where(lax.broadcasted_iota(jnp.int32, (PAGE, PAGE), 0) > lax.broadcasted_iota(jnp.int32, (PAGE, PAGE), 1),
                      1.0, 0.0).astype(BF16)

    def body(c, state):
        m_old, l_old, acc, carry = state
        slot = c % 2

        @pl.when(c + 1 < n_chunk)
        def _():
            for cp in copies(b, c + 1, 1 - slot):
                cp.start()

        @pl.when(jnp.logical_and(c + 1 == n_chunk, b + 1 < n_seq))
        def _():
            for cp in copies(b + 1, 0, 1 - slot):
                cp.start()

        for cp in copies(b, c, slot):
            cp.wait()
        totals = [jnp.sum(lf_buf[slot, g], axis=1, keepdims=True) for g in range(PG)]
        newer = [None] * PG
        for g in range(PG - 1, -1, -1):
            newer[g] = carry
            carry = carry + totals[g]
        within = _dot01(lf_buf[slot].reshape(PG * 8, PAGE), later)
        s_parts = []
        for g in range(PG):
            s_parts.append(per_head(kv_buf[slot, g, 0] * q_wide) + newer[g] + within[g * 8:(g + 1) * 8])
        s = jnp.concatenate(s_parts, axis=1)
        m_new = jnp.maximum(m_old, jnp.max(s, axis=1, keepdims=True))
        alpha = jnp.exp(m_old - m_new)
        pr = jnp.exp(s - m_new)
        l_new = alpha * l_old + jnp.sum(pr, axis=1, keepdims=True)
        acc = acc * per_feature(alpha, 1)
        for g in range(PG):
            acc = acc + kv_buf[slot, g, 1] * per_feature(pr[:, g * PAGE:(g + 1) * PAGE], PAGE)
        return m_new, l_new, acc, carry

    _, l, acc, _ = lax.fori_loop(0, n_chunk, body, (m0, jnp.ones((8, 1), F32), acc0, lfnew_ref[0]))
    o_ref[0] = jnp.sum(acc, axis=1, keepdims=True) / per_feature(l, 1)


def _fox_paged(page_table, cache_kv, cache_lf, layer, q_col, kv_new, lf_new):
    bsz, n_pages = page_table.shape
    assert (n_pages // PG) % 2 == 0

    def per_b(shape):
        return pl.BlockSpec(shape, lambda b, pt: (b, 0, 0))

    grid_spec = pltpu.PrefetchScalarGridSpec(
        num_scalar_prefetch=1,
        grid=(bsz,),
        in_specs=[pl.BlockSpec(memory_space=pl.ANY), pl.BlockSpec(memory_space=pl.ANY),
                  per_b((1, MIX, 1)), per_b((1, 2 * MIX, 1)), per_b((1, 8, 1))],
        out_specs=per_b((1, MIX, 1)),
        scratch_shapes=[pltpu.VMEM((2, PG, 2, MIX, PAGE), F32), pltpu.VMEM((2, PG, 8, PAGE), F32),
                        pltpu.SemaphoreType.DMA((2,))],
    )
    return pl.pallas_call(
        functools.partial(_fox_paged_kernel, layer=layer, n_pages=n_pages),
        grid_spec=grid_spec,
        out_shape=jax.ShapeDtypeStruct((bsz, MIX, 1), F32),
        compiler_params=_cparams(("arbitrary",)),
        name="fox_paged",
    )(page_table, cache_kv, cache_lf, q_col, kv_new, lf_new)


def _layer_params(l, norm_g, ffn_gate, ffn_up, ffn_down, w_in, fox_f_bias, nsa_cmp_pos, nsa_cmp_w,
                  conv_w, conv_b, conv_ln, pool_w, pool_scale, w_branch, w_out):
    wi = w_in[l]
    o_aq, o_akv, o_ag, o_b, o_c, o_d, o_f, o_m = np.cumsum([0, 256, 384, 12, 512, 256, 768, 4])
    misc = jnp.concatenate([wi[:, o_ag:o_ag + 12], wi[:, o_f:o_f + 4], jnp.zeros((D_MODEL, 112), F32)], axis=1)
    w_small = jnp.concatenate([wi[:, o_aq:o_aq + 256], wi[:, o_akv:o_akv + 384], misc, wi[:, o_b:o_b + 512],
                               wi[:, o_c:o_c + 256], wi[:, o_d:o_d + 768]], axis=1).astype(BF16)
    fb_row = jnp.zeros((1, 128), F32).at[0, 12:16].set(fox_f_bias[l])
    cw = nsa_cmp_w[l].reshape(2, 2, CMP_STRIDE * HD, HD).astype(BF16)
    cpos = nsa_cmp_pos[l].reshape(2, 2, 1, CMP_STRIDE * HD)
    w4 = nsa_cmp_w[l].reshape(2, 2, CMP_STRIDE, HD, HD)
    none = jnp.zeros_like(w4[0])
    w_pair = jnp.concatenate([jnp.concatenate([w4[0], none], axis=-1), jnp.concatenate([none, w4[1]], axis=-1)], axis=2)
    p4 = nsa_cmp_pos[l].reshape(2, 2, CMP_STRIDE, HD)
    pos_pair = jnp.concatenate([p4[0], p4[1]], axis=-1).reshape(2, 1, CMP_STRIDE * 2 * HD)
    group = MIX // len(POOL_WINDOWS)
    pw_bd = jnp.zeros((MIX, MIX), F32)
    for g in range(len(POOL_WINDOWS)):
        pw_bd = pw_bd.at[g * group:(g + 1) * group, g * group:(g + 1) * group].set(pool_w[l, g])
    cw_conv = jnp.concatenate([conv_w[l], jnp.zeros((HALO - CONV_WIDTH, MIX), F32)], axis=0)
    return dict(
        g=[norm_g[l, k].reshape(1, D_MODEL) for k in range(3)],
        ffn=[(ffn_gate[l, k].astype(BF16), ffn_up[l, k].astype(BF16), ffn_down[l, k].astype(BF16)) for k in range(2)],
        w_small=w_small, w_merge=wi[:, o_m:].astype(BF16), fb_row=fb_row, cmp_w=cw, cmp_pos=cpos,
        cmp_w_pair=w_pair.reshape(2, CMP_STRIDE * 2 * HD, 2 * HD).astype(BF16), cmp_pos_pair=pos_pair,
        conv_w=cw_conv, conv_b=conv_b[l].reshape(1, MIX), conv_ln=conv_ln[l].reshape(2, 1, MIX),
        pool_w=pw_bd.astype(BF16), pool_scale=pool_scale[l].reshape(1, MIX),
        w_branch=w_branch[l].astype(BF16), w_out=w_out[l].astype(BF16))


def _overlap(n_rows, n_blocks, n_pad):
    cs = np.arange(n_rows)[:, None] * CMP_STRIDE
    j0 = np.arange(n_pad)[None, :] * SLC_BLOCK
    ov = (cs < j0 + SLC_BLOCK) & (cs + CMP_LEN > j0) & (np.arange(n_pad)[None, :] < n_blocks)
    return jnp.asarray(ov.astype(np.float32), dtype=BF16)


def _lookup(bt, dist, valid):
    d = np.clip(dist, 0, NDIST - 1)
    return jnp.where(jnp.asarray(valid)[None], bt[:, d], NEG)


def _toeplitz_kernel(rel_ref, o_ref, *, base, vstep, stride, hi):
    width = o_ref.shape[2]
    r = lax.broadcasted_iota(jnp.int32, (TQ, width), 0)
    m = lax.broadcasted_iota(jnp.int32, (TQ, width), 1)
    dist = base + vstep * pl.program_id(0) + r - stride * m
    bucket = _t5_bucket(dist)
    hide = jnp.where(dist < 0, NEG, jnp.where(dist > hi, NEG, 0.0))
    for hh in range(NH):
        far = rel_ref[N_BUCKETS - 1, hh]
        val = jnp.zeros((TQ, width), F32)
        for b in range(N_BUCKETS - 1):
            val = jnp.where(bucket == b, rel_ref[b, hh] - far, val)
        o_ref[0, hh * TQ:(hh + 1) * TQ, :] = val + hide


def _toeplitz(rel_bias, n_var, width, base, vstep, stride, hi):
    return pl.pallas_call(
        functools.partial(_toeplitz_kernel, base=base, vstep=vstep, stride=stride, hi=hi),
        grid=(n_var,),
        in_specs=[pl.BlockSpec(memory_space=pltpu.SMEM)],
        out_specs=pl.BlockSpec((1, NH * TQ, width), lambda v: (v, 0, 0)),
        out_shape=jax.ShapeDtypeStruct((n_var, NH * TQ, width), F32),
        compiler_params=_cparams(("arbitrary",)),
        name="t5_tables",
    )(rel_bias)


def _prompt_tables(rel_bias):
    big = 1 << 30
    tab_c = _toeplitz(rel_bias, 33, 256, -(CMP_LEN - 1), TQ, CMP_STRIDE, big)
    tnear = _toeplitz(rel_bias, 2, 2 * TQ, 0, TQ, 1, big)
    twin = _toeplitz(rel_bias, 1, WINDOW + TQ, WINDOW, 0, 1, WINDOW)[0]
    return tab_c, tnear, twin


def _prompt_layer(x, lp, tables, fg, final):
    l = x.shape[0]
    tab_c, tnear, twin = tables
    x = _ffn(x, lp["g"][0], *lp["ffn"][0], fg)
    (qn, nsa_kv, win_kv, misc, u, cin, fq, fox_kv, ksel, kwin, vcat, fk, fv, stat) = _inproj(x, lp["g"][1], lp["w_small"], lp["fb_row"])
    n_rows = l // CMP_STRIDE
    ns = l // SLC_BLOCK
    rk = nsa_kv[:, 0:HD].reshape(n_rows, CMP_STRIDE * HD)
    rv = nsa_kv[:, HD:2 * HD].reshape(n_rows, CMP_STRIDE * HD)
    kc, vc = _compress_prompt(rk, rv, lp["cmp_pos"], lp["cmp_w"])
    oc, val = _cmp_attn(qn, kc, vc, _overlap(n_rows, ns, ns), tab_c, misc)
    selb = _topk_mask(val)
    front = jnp.zeros((WINDOW, 128), BF16)
    ya = _nsa_attn(qn, selb, ksel, jnp.concatenate([front, kwin]), jnp.concatenate([front, vcat]), tnear, twin, misc, oc)
    zeros = jnp.zeros((HALO, MIX), F32)
    yb, yc = _convpool(u, zeros, cin, zeros, lp["conv_w"], lp["conv_b"], lp["conv_ln"], lp["pool_w"], lp["pool_scale"], 0)
    logf = misc[:, 12:16]
    per_chunk = TKF // min(TM, l)
    key_reach = jnp.sqrt(jnp.max(stat[:, 1, 0:NH], axis=0)) * 1.001
    forgot = -stat[per_chunk - 1::per_chunk, 0, 3 * NH:4 * NH]
    yd = _fox_attn(jnp.concatenate([key_reach[:, None], forgot.T], axis=1), fq, fk, fv)
    x = _merge(x, lp["g"][1], lp["w_merge"], lp["w_branch"], lp["w_out"], ya, yb, yc, yd)
    x = _ffn(x, lp["g"][2], *lp["ffn"][1], fg, final)
    state = (nsa_kv, fox_kv, logf, win_kv[l - min(WINDOW, l):], u[l - CONV_HIST:], cin[l - POOL_HIST:])
    return x, state


def _sample_layer(x, lp, relt8, bt, fg, final, layer, page_table, nsa_pages, fox_pages, lf_pages, win_rows, st_win,
                  st_conv, st_pool):
    bsz = x.shape[0]
    n_pages = page_table.shape[1]
    past = n_pages * PAGE
    x = _ffn(x, lp["g"][0], *lp["ffn"][0], fg)
    (qn, nsa_kv, win_kv, misc, u, cin, fq, fox_kv, _, _, _, _, _, _) = _inproj(x, lp["g"][1], lp["w_small"], lp["fb_row"])
    pad4 = lambda a: jnp.concatenate([a, jnp.zeros((bsz, 8 - NH) + a.shape[2:], a.dtype)], axis=1)
    q8 = pad4(jnp.transpose(qn, (1, 0, 2)))
    gates = misc[:, 0:3 * NH].reshape(bsz, 3, NH)
    misc8 = jnp.zeros((bsz, 8, 128), F32).at[:, 0:NH, 0:3].set(jnp.transpose(gates, (0, 2, 1)))
    n_rows = past // CMP_STRIDE
    ns = past // SLC_BLOCK + 1
    ns_pad = -(-ns // 128) * 128
    c_end = np.arange(n_rows) * CMP_STRIDE + CMP_LEN - 1
    valid_c = (c_end <= past) & (np.arange(n_rows) < n_rows - 1)
    bias_c = jnp.concatenate([_lookup(bt, past - c_end, valid_c), jnp.full((8 - NH, n_rows), NEG, F32)], axis=0)
    oc, val = _nsa_paged_cmp(page_table, nsa_pages, layer, jnp.concatenate([q8, jnp.zeros_like(q8)], axis=2),
                             lp["cmp_pos_pair"], lp["cmp_w_pair"], _overlap(n_rows, ns, ns_pad), bias_c, misc8, past)
    idx = _topk_index(val.reshape(bsz, ns_pad), min(N_SELECT, ns))
    n_win = st_win.shape[1]
    wbias = jnp.concatenate([_lookup(bt, n_win - np.arange(n_win), np.ones(n_win, bool)),
                             jnp.full((8 - NH, n_win), NEG, F32)], axis=0)
    blk = idx.reshape(bsz, 128)[:, 0:N_SELECT]
    held = jnp.minimum(blk // 2, n_pages - 1)
    pages = jnp.take_along_axis(page_table, held, axis=1)
    ya = _nsa_paged_attn(jnp.concatenate([pages, blk], axis=1), nsa_pages, layer, q8, nsa_kv.reshape(bsz, 1, 256),
                         win_rows, win_kv.reshape(bsz, 1, 128), relt8, wbias, misc8, oc, past)
    ya = jnp.transpose(ya[:, 0:NH], (1, 0, 2))
    yb, yc = _convpool_step(u, jnp.transpose(st_conv, (1, 0, 2)), cin, jnp.transpose(st_pool, (1, 0, 2)),
                            lp["conv_w"], lp["conv_b"], lp["conv_ln"], lp["pool_w"], lp["pool_scale"], past)
    logf = misc[:, 12:16]
    q_col = jnp.transpose(fq[:, :, 0:HD], (1, 0, 2)).reshape(bsz, MIX, 1).astype(F32)
    lf_new = pad4(logf.reshape(bsz, NH, 1))
    yd = _fox_paged(page_table, fox_pages, lf_pages, layer, q_col, fox_kv.reshape(bsz, 2 * MIX, 1), lf_new)
    yd = yd.reshape(bsz, MIX).astype(BF16)
    x = _merge(x, lp["g"][1], lp["w_merge"], lp["w_branch"], lp["w_out"], ya, yb, yc, yd)
    x = _ffn(x, lp["g"][2], *lp["ffn"][1], fg, final)
    new_win =jnp.concatenate([st_win.reshape(bsz, n_win, 128)[:, 1:], win_kv[:, None, :]], axis=1)
    new_conv = jnp.concatenate([st_conv[:, 1:], u[:, None, :]], axis=1)
    new_pool = jnp.concatenate([st_pool[:, 1:], cin[:, None, :]], axis=1)
    state = (nsa_kv, fox_kv, logf, new_win, new_conv, new_pool)
    return x, state


def kernel(x_prompt, x_sample, cache_nsa, cache_fox_kv, cache_fox_logf, state_nsa_win, state_conv, state_pool,
           page_table, norm_g, ffn_gate, ffn_up, ffn_down, w_in, fox_f_bias, nsa_cmp_pos, nsa_cmp_w, rel_bias,
           conv_w, conv_b, conv_ln, pool_w, pool_scale, w_branch, w_out, final_norm_g):
    assert x_prompt.shape[0] == 1 and x_sample.shape[1] == 1
    depth = norm_g.shape[0]
    l = x_prompt.shape[1]
    bsz = x_sample.shape[0]
    win_keep = state_nsa_win.shape[2]
    assert l % SUPER == 0 and page_table.shape[1] * PAGE >= max(WINDOW, SUPER) and win_keep == WINDOW
    relt8 = jnp.concatenate([rel_bias.T, jnp.zeros((8 - NH, N_BUCKETS), F32)], axis=0)
    bt = _bias_table(relt8)[0:NH]
    tables = _prompt_tables(rel_bias)
    fg = final_norm_g.reshape(1, D_MODEL)
    xp = x_prompt.reshape(l, D_MODEL)
    xs = x_sample.reshape(bsz, D_MODEL)
    n_phys = cache_nsa.shape[1]
    nsa_pages = jnp.transpose(cache_nsa, (0, 1, 3, 4, 5, 2)).reshape(depth, n_phys, 4, HD, PAGE)
    fox_pages = jnp.transpose(cache_fox_kv, (0, 1, 3, 4, 5, 2)).reshape(depth, n_phys, 2, MIX, PAGE)
    lf_pages = jnp.pad(jnp.transpose(cache_fox_logf, (0, 1, 3, 2)), ((0, 0), (0, 0), (0, 8 - NH), (0, 0)))
    win_rows = jnp.transpose(state_nsa_win, (0, 1, 3, 4, 5, 2)).reshape(depth, bsz, 2, HD, win_keep)
    st_p, st_s = [], []
    for layer in range(depth):
        lp = _layer_params(layer, norm_g, ffn_gate, ffn_up, ffn_down, w_in, fox_f_bias, nsa_cmp_pos, nsa_cmp_w,
                           conv_w, conv_b, conv_ln, pool_w, pool_scale, w_branch, w_out)
        final = layer == depth - 1
        xp, sp = _prompt_layer(xp, lp, tables, fg, final)
        st_p.append(sp)
        xs, ss = _sample_layer(xs, lp, relt8, bt, fg, final, layer, page_table, nsa_pages, fox_pages, lf_pages, win_rows,
                               state_nsa_win[layer], state_conv[layer], state_pool[layer])
        st_s.append(ss)

    def stack(states, k, shape):
        return jnp.stack([s[k] for s in states]).reshape(shape)

    return (xp.reshape(1, l, D_MODEL), xs.reshape(bsz, 1, D_MODEL),
            stack(st_p, 0, (depth, 1, l, 4, 1, HD)), stack(st_s, 0, (depth, bsz, 1, 4, 1, HD)),
            stack(st_p, 1, (depth, 1, l, 2, NH, HD)), stack(st_s, 1, (depth, bsz, 1, 2, NH, HD)),
            stack(st_p, 2, (depth, 1, l, NH)), stack(st_s, 2, (depth, bsz, 1, NH)),
            stack(st_p, 3, (depth, 1, win_keep, 2, 1, HD)), stack(st_s, 3, (depth, bsz, win_keep, 2, 1, HD)),
            stack(st_p, 4, (depth, 1, CONV_HIST, MIX)), stack(st_s, 4, (depth, bsz, CONV_HIST, MIX)),
            stack(st_p, 5, (depth, 1, POOL_HIST, MIX)), stack(st_s, 5, (depth, bsz, POOL_HIST, MIX)))
```

```python
import functools
import math

import numpy as np
import jax
import jax.numpy as jnp
from jax import lax
from jax.experimental import pallas as pl
from jax.experimental.pallas import tpu as pltpu

F32 = jnp.float32
BF16 = jnp.bfloat16

D_MODEL = 1024
N_BRANCH = 4
MIX = D_MODEL // 4
HD = 64
NH = MIX // HD
CMP_STRIDE = 16
CMP_LEN = 32
SLC_BLOCK = 64
N_SELECT = 16
WINDOW = 512
N_BUCKETS = 32
REL_EXACT = 16
REL_MAX_DIST = 128
CONV_WIDTH = 31
CONV_HIST = CONV_WIDTH - 1
POOL_WINDOWS = (2, 4, 8, 16)
POOL_HIST = 15
D_FF = 2816
EPS = 1e-6
NEG = -1e30
BIG = 1e6
PAGE = 128

LANE = 128
TQ = 128
TOPK_ROWS = 512
TK = 1024
SUPER = 64 * SLC_BLOCK
TF = 256
TKF = 1024
FOX_CUT = 100.0
TM = 256
TM_WIDE = 512
FF_CHUNK = 1408
HALO = 32
NDIST = 1024
PG = 16
VMEM_LIMIT = 56 * 1024 * 1024

C_AQ, C_NSA, C_WIN, C_MISC, C_CONV, C_POOL, C_FOX = 0, 256, 512, 640, 768, 1280, 1536
N_SMALL = 2304


def _cparams(sem):
    return pltpu.CompilerParams(dimension_semantics=sem, vmem_limit_bytes=VMEM_LIMIT)


def _full(shape, single=True):
    nd = len(shape)
    kw = dict(pipeline_mode=pl.Buffered(1)) if single else {}
    return pl.BlockSpec(shape, lambda *a, _nd=nd: (0,) * _nd, **kw)


def _rms(x, g):
    ms = jnp.mean(x * x, axis=-1, keepdims=True)
    return x * lax.rsqrt(ms + EPS) * g


def _dot(a, b):
    return jnp.dot(a, b, preferred_element_type=F32)


def _dot_nt(a, b):
    return lax.dot_general(a, b, (((1,), (1,)), ((), ())), preferred_element_type=F32)


def _split3(x):
    hi = x.astype(BF16)
    r1 = x - hi.astype(F32)
    mid = r1.astype(BF16)
    lo = (r1 - mid.astype(F32)).astype(BF16)
    return hi, mid, lo


def _dot01(x, ones_bf16):
    hi, mid, lo = _split3(x)
    return _dot(hi, ones_bf16) + _dot(mid, ones_bf16) + _dot(lo, ones_bf16)


def _dot01_l(ones_bf16, x):
    hi, mid, lo = _split3(x)
    return _dot(ones_bf16, hi) + _dot(ones_bf16, mid) + _dot(ones_bf16, lo)


def _log_sigmoid(x):
    return jnp.minimum(x, 0.0) - jnp.log1p(jnp.exp(-jnp.abs(x)))


def _t5_bucket(dist):
    n = jnp.maximum(dist, 0)
    ratio = jnp.log(jnp.maximum(n, 1).astype(F32) / REL_EXACT) / math.log(REL_MAX_DIST / REL_EXACT)
    large = REL_EXACT + (ratio * (N_BUCKETS - REL_EXACT)).astype(jnp.int32)
    return jnp.where(n < REL_EXACT, n, jnp.minimum(large, N_BUCKETS - 1))


def _t5_bias(relt, dist):
    bucket = _t5_bucket(dist)
    rows = lax.broadcasted_iota(jnp.int32, (N_BUCKETS, dist.shape[1]), 0)
    onehot = jnp.where(rows == bucket, 1.0, 0.0).astype(BF16)
    return _dot01(relt, onehot)


def _ffn_kernel(x_ref, g_ref, wg_ref, wu_ref, wd_ref, fg_ref, o_ref, *, final):
    x = x_ref[...]
    h = _rms(x, g_ref[...]).astype(BF16)
    acc = jnp.zeros_like(x)
    for c in range(D_FF // FF_CHUNK):
        sl = slice(c * FF_CHUNK, (c + 1) * FF_CHUNK)
        a = _dot(h, wg_ref[:, sl])
        b = _dot(h, wu_ref[:, sl])
        t = (a * jax.nn.sigmoid(a) * b).astype(BF16)
        acc = acc + _dot(t, wd_ref[sl, :])
    out = x + 0.5 * acc
    o_ref[...] = _rms(out, fg_ref[...]) if final else out


def _ffn(x, g, wg, wu, wd, fg, final=False):
    m = x.shape[0]
    tm = min(TM_WIDE, m)
    row = pl.BlockSpec((tm, D_MODEL), lambda i: (i, 0))
    return pl.pallas_call(
        functools.partial(_ffn_kernel, final=final),
        grid=(m // tm,),
        in_specs=[row, _full((1, D_MODEL)), _full((D_MODEL, D_FF)), _full((D_MODEL, D_FF)),
                  _full((D_FF, D_MODEL)), _full((1, D_MODEL))],
        out_specs=row,
        out_shape=jax.ShapeDtypeStruct((m, D_MODEL), F32),
        compiler_params=_cparams(("arbitrary",)),
        name="ffn",
    )(x, g, wg, wu, wd, fg)


def _inproj_kernel(x_ref, g_ref, w_ref, fb_ref,
                   qn_ref, nsa_ref, win_ref, misc_ref, u_ref, cin_ref, fq_ref, fkv_ref,
                   ksel_ref, kwin_ref, vcat_ref, fk_ref, fv_ref, stat_ref, csum_ref):
    tm = x_ref.shape[0]
    h = _rms(x_ref[...], g_ref[...]).astype(BF16)
    z = _dot(h, w_ref[...])
    scale = HD ** -0.5
    zm = z[:, C_MISC:C_MISC + 128]
    lane = lax.broadcasted_iota(jnp.int32, zm.shape, 1)
    logf = _log_sigmoid(zm + fb_ref[...])
    misc_ref[...] = jnp.where(lane < 3 * NH, jax.nn.sigmoid(zm), logf)
    @pl.when(pl.program_id(0) == 0)
    def _():
        csum_ref[...] = jnp.zeros_like(csum_ref)
    tri = jnp.where(lax.broadcasted_iota(jnp.int32, (tm, tm), 1) <= lax.broadcasted_iota(jnp.int32, (tm, tm), 0),
                    1.0, 0.0).astype(BF16)
    lf = jnp.where((lane >= 3 * NH) & (lane < 4 * NH), logf, 0.0)
    csum = _dot01_l(tri, lf) + csum_ref[...]
    csum_ref[...] = csum[tm - 1:tm, :]
    feat = lax.broadcasted_iota(jnp.int32, (tm, HD), 1)
    ones3 = jnp.where(feat < 3, 1.0, 0.0).astype(BF16)
    knorm = jnp.zeros((1, 128), F32)
    for hh in range(NH):
        qn_ref[hh] = (z[:, C_AQ + HD * hh:C_AQ + HD * (hh + 1)] * scale).astype(BF16)
        fq = (z[:, C_FOX + HD * hh:C_FOX + HD * (hh + 1)] * scale).astype(BF16)
        fq_ref[hh] = jnp.concatenate([fq, ones3], axis=1)
        hi, mid, lo = (p.astype(F32) for p in _split3(-csum[:, 3 * NH + hh:3 * NH + hh + 1]))
        extra = jnp.where(feat == 0, hi, jnp.where(feat == 1, mid, jnp.where(feat == 2, lo, 0.0)))
        fk = z[:, C_FOX + MIX + HD * hh:C_FOX + MIX + HD * (hh + 1)].astype(BF16)
        fk_ref[hh] = jnp.concatenate([fk, extra.astype(BF16)], axis=1)
        fkf = fk.astype(F32)
        k2 = jnp.max(jnp.sum(fkf * fkf, axis=1, keepdims=True), axis=0, keepdims=True)
        knorm = jnp.where(lane[0:1] == hh, k2, knorm)
    row8 = lax.broadcasted_iota(jnp.int32, (8, 128), 0)
    stat_ref[0] = jnp.where(row8 == 0, csum[tm - 1:tm, :], jnp.where(row8 == 1, knorm, 0.0))
    fv_ref[...] = z[:, C_FOX + 2 * MIX:C_FOX + 3 * MIX].astype(BF16)
    nsa = z[:, C_NSA:C_NSA + 256]
    nsa_ref[...] = nsa
    win = z[:, C_WIN:C_WIN + 128]
    win_ref[...] = win
    u_ref[...] = z[:, C_CONV:C_CONV + MIX] * jax.nn.sigmoid(z[:, C_CONV + MIX:C_CONV + 2 * MIX])
    cin_ref[...] = z[:, C_POOL:C_POOL + MIX]
    fkv_ref[...] = z[:, C_FOX + MIX:C_FOX + 3 * MIX]
    t = pl.program_id(0) * tm + lax.broadcasted_iota(jnp.int32, (tm, HD), 0)
    feat = lax.broadcasted_iota(jnp.int32, (tm, HD), 1)
    onehot = jnp.where(((t // SLC_BLOCK) % 64) == feat, 1.0, 0.0).astype(BF16)
    ksel_ref[...] = jnp.concatenate([nsa[:, 128:192].astype(BF16), onehot], axis=1)
    kwin_ref[...] = jnp.concatenate([win[:, 0:64].astype(BF16), jnp.zeros((tm, HD), BF16)], axis=1)
    vcat_ref[...] = jnp.concatenate([nsa[:, 192:256].astype(BF16), win[:, 64:128].astype(BF16)], axis=1)


def _inproj(x, g, w_small, fbias_row):
    m = x.shape[0]
    tm = min(TM, m)

    def row(n):
        return pl.BlockSpec((tm, n), lambda i: (i, 0))

    heads = pl.BlockSpec((NH, tm, HD), lambda i: (0, i, 0))
    wide = pl.BlockSpec((NH, tm, 128), lambda i: (0, i, 0))
    shapes = [((NH, m, HD), BF16, heads), ((m, 256), F32, row(256)), ((m, 128), F32, row(128)),
              ((m, 128), F32, row(128)), ((m, MIX), F32, row(MIX)), ((m, MIX), F32, row(MIX)),
              ((NH, m, 128), BF16, wide), ((m, 2 * MIX), F32, row(2 * MIX)),
              ((m, 128), BF16, row(128)), ((m, 128), BF16, row(128)), ((m, 128), BF16, row(128)),
              ((NH, m, 128), BF16, wide), ((m, MIX), BF16, row(MIX)),
              ((m // tm, 8, 128), F32, pl.BlockSpec((1, 8, 128), lambda i: (i, 0, 0)))]
    return pl.pallas_call(
        _inproj_kernel,
        grid=(m // tm,),
        in_specs=[row(D_MODEL), _full((1, D_MODEL)), _full((D_MODEL, N_SMALL)), _full((1, 128))],
        out_specs=[s[2] for s in shapes],
        out_shape=[jax.ShapeDtypeStruct(s[0], s[1]) for s in shapes],
        scratch_shapes=[pltpu.VMEM((1, 128), F32)],
        compiler_params=_cparams(("arbitrary",)),
        name="inproj",
    )(x, g, w_small, fbias_row)


def _merge_kernel(x_ref, g_ref, wm_ref, wb_ref, wo_ref, ya_ref, yb_ref, yc_ref, yd_ref, o_ref):
    x = x_ref[...]
    h = _rms(x, g_ref[...]).astype(BF16)
    mix = jnp.zeros_like(x)
    for n in range(N_BRANCH):
        gate = jax.nn.sigmoid(_dot(h, wm_ref[:, n * D_MODEL:(n + 1) * D_MODEL]))
        if n == 0:
            proj = jnp.zeros_like(x)
            for hh in range(NH):
                proj = proj + _dot(ya_ref[hh], wb_ref[n, hh * HD:(hh + 1) * HD, :])
        else:
            proj = _dot((yb_ref, yc_ref, yd_ref)[n - 1][...], wb_ref[n])
        mix = mix + gate * proj
    o_ref[...] = x + _dot(mix.astype(BF16), wo_ref[...])


def _merge(x, g, w_merge, w_branch, w_out, ya, yb, yc, yd):
    m = x.shape[0]
    tm = min(TM_WIDE, m)
    row = pl.BlockSpec((tm, D_MODEL), lambda i: (i, 0))
    heads = pl.BlockSpec((NH, tm, HD), lambda i: (0, i, 0))
    mixrow = pl.BlockSpec((tm, MIX), lambda i: (i, 0))
    return pl.pallas_call(
        _merge_kernel,
        grid=(m // tm,),
        in_specs=[row, _full((1, D_MODEL)), _full((D_MODEL, N_BRANCH * D_MODEL)),
                  _full((N_BRANCH, MIX, D_MODEL)), _full((D_MODEL, D_MODEL)),
                  heads, mixrow, mixrow, mixrow],
        out_specs=row,
        out_shape=jax.ShapeDtypeStruct((m, D_MODEL), F32),
        compiler_params=_cparams(("arbitrary",)),
        name="merge",
    )(x, g, w_merge, w_branch, w_out, ya, yb, yc, yd)


def _bias_table_kernel(relt_ref, o_ref):
    dist = lax.broadcasted_iota(jnp.int32, (1, NDIST), 1)
    o_ref[...] = _t5_bias(relt_ref[...], dist)


def _bias_table(relt8):
    return pl.pallas_call(
        _bias_table_kernel,
        out_shape=jax.ShapeDtypeStruct((8, NDIST), F32),
        name="bias_table",
    )(relt8)


def _compress(r, pos_ref, w_ref, t, scr):
    n = r.shape[0]
    a = _dot((r + pos_ref[t, 0]).astype(BF16), w_ref[t, 0])
    b = _dot((r + pos_ref[t, 1]).astype(BF16), w_ref[t, 1])
    scr[pl.ds(0, n), :] = b
    scr[pl.ds(n, 8), :] = jnp.zeros((8, HD), F32)
    return a + scr[pl.ds(1, n), :]


def _compress_kernel(rk_ref, rv_ref, pos_ref, w_ref, kc_ref, vc_ref, scr):
    kc_ref[...] = _compress(rk_ref[...], pos_ref, w_ref, 0, scr).astype(BF16)
    vc_ref[...] = _compress(rv_ref[...], pos_ref, w_ref, 1, scr).astype(BF16)


def _compress_prompt(rk, rv, pos, w):
    r = rk.shape[0]
    return pl.pallas_call(
        _compress_kernel,
        in_specs=[_full((r, 1024), False), _full((r, 1024), False), _full((2, 2, 1, 1024), False),
                  _full((2, 2, 1024, HD), False)],
        out_specs=[_full((r, HD), False)] * 2,
        grid=(1,),
        out_shape=[jax.ShapeDtypeStruct((r, HD), BF16)] * 2,
        scratch_shapes=[pltpu.VMEM((r + 8, HD), F32)],
        compiler_params=_cparams(("arbitrary",)),
        name="nsa_compress",
    )(rk, rv, pos, w)


def _select_topk(val, n_sel):
    ns = val.shape[1]
    j = lax.broadcasted_iota(jnp.int32, val.shape, 1).astype(F32)
    picks = []
    for _ in range(n_sel):
        mx = jnp.max(val, axis=1, keepdims=True)
        first = jnp.min(jnp.where(val == mx, j, float(ns)), axis=1, keepdims=True)
        picks.append(first)
        val = jnp.where(j == first, -jnp.inf, val)
    return picks


def _cmp_attn_kernel(q_ref, kc_ref, vc_ref, ovl_ref, tab_ref, misc_ref, oc_ref, val_ref, *, n_rows):
    i = pl.program_id(0)
    q = q_ref[...].reshape(NH * TQ, HD)
    w0 = 8 * i - 8
    c0 = jnp.clip((w0 // 128) * 128, 0, n_rows - 256)
    c0 = pl.multiple_of(c0, 128)
    s = _dot_nt(q, kc_ref[...])
    n_idx = lax.broadcasted_iota(jnp.int32, s.shape, 1)
    s_far = s + jnp.where(n_idx < c0, 0.0, NEG)
    s_win = _dot_nt(q, kc_ref[pl.ds(c0, 256), :]) + tab_ref[0]
    m = jnp.maximum(jnp.max(s_far, axis=1, keepdims=True), jnp.max(s_win, axis=1, keepdims=True))
    p_far = jnp.exp(s_far - m)
    p_win = jnp.exp(s_win - m)
    l = jnp.sum(p_far, axis=1, keepdims=True) + jnp.sum(p_win, axis=1, keepdims=True)
    inv = jnp.where(m > 0.5 * NEG, 1.0 / l, 0.0)
    pb_far = (p_far * inv).astype(BF16)
    pb_win = (p_win * inv).astype(BF16)
    o = _dot(pb_far, vc_ref[...]) + _dot(pb_win, vc_ref[pl.ds(c0, 256), :])
    impf = _dot(pb_far, ovl_ref[...]) + _dot(pb_win, ovl_ref[pl.ds(c0, 256), :])
    imp = impf[0:TQ] + impf[TQ:2 * TQ] + impf[2 * TQ:3 * TQ] + impf[3 * TQ:4 * TQ]
    gates = misc_ref[...]
    for hh in range(NH):
        oc_ref[hh] = o[hh * TQ:(hh + 1) * TQ] * gates[:, hh:hh + 1]
    j = lax.broadcasted_iota(jnp.int32, imp.shape, 1)
    qpos = i * TQ + lax.broadcasted_iota(jnp.int32, imp.shape, 0)
    qblk = qpos // SLC_BLOCK
    forced = (j == 0) | (j == qblk) | (j == qblk - 1)
    val_ref[...] = jnp.where(forced, BIG, jnp.where(j * SLC_BLOCK <= qpos, imp, -BIG))


def _cmp_attn(qh, kc, vc, ovl, tab, misc):
    l = qh.shape[1]
    r = kc.shape[0]
    ns = ovl.shape[1]

    def variant(i):
        w0 = 8 * i - 8
        c0 = jnp.clip((w0 // 128) * 128, 0, r - 256)
        return ((w0 - c0 + 8) // 8, 0, 0)

    heads = pl.BlockSpec((NH, TQ, HD), lambda i: (0, i, 0))
    return pl.pallas_call(
        functools.partial(_cmp_attn_kernel, n_rows=r),
        grid=(l // TQ,),
        in_specs=[heads, _full((r, HD)), _full((r, HD)), _full((r, ns)),
                  pl.BlockSpec((1, NH * TQ, 256), variant),
                  pl.BlockSpec((TQ, 128), lambda i: (i, 0))],
        out_specs=[heads, pl.BlockSpec((TQ, ns), lambda i: (i, 0))],
        out_shape=[jax.ShapeDtypeStruct((NH, l, HD), F32), jax.ShapeDtypeStruct((l, ns), F32)],
        compiler_params=_cparams(("arbitrary",)),
        name="nsa_cmp_attn",
    )(qh, kc, vc, ovl, tab, misc)


def _topk_mask_kernel(val_ref, selb_ref, *, n_sel):
    val = val_ref[...]
    jf = lax.broadcasted_iota(jnp.int32, val.shape, 1).astype(F32)
    selb = jnp.full(val.shape, NEG, F32)
    for first in _select_topk(val, n_sel):
        selb = jnp.where(jf == first, 0.0, selb)
    selb_ref[...] = selb.astype(BF16)


def _topk_mask(val):
    l, ns = val.shape
    tr = min(TOPK_ROWS, l)
    return pl.pallas_call(
        functools.partial(_topk_mask_kernel, n_sel=min(N_SELECT, ns)),
        grid=(l // tr,),
        in_specs=[pl.BlockSpec((tr, ns), lambda i: (i, 0))],
        out_specs=pl.BlockSpec((tr, ns), lambda i: (i, 0)),
        out_shape=jax.ShapeDtypeStruct((l, ns), BF16),
        compiler_params=_cparams(("arbitrary",)),
        name="nsa_topk",
    )(val)


def _topk_index_kernel(val_ref, idx_ref, *, n_sel):
    lane = lax.broadcasted_iota(jnp.int32, idx_ref.shape, 1)
    idx = jnp.zeros(idx_ref.shape, F32)
    for t, first in enumerate(_select_topk(val_ref[...], n_sel)):
        idx = jnp.where(lane == t, first, idx)
    idx_ref[...] = idx.astype(jnp.int32)


def _topk_index(val, n_sel):
    return pl.pallas_call(
        functools.partial(_topk_index_kernel, n_sel=n_sel),
        out_shape=jax.ShapeDtypeStruct((val.shape[0], 128), jnp.int32),
        name="nsa_topk_index",
    )(val)


def _flash_step(carry, s, v):
    m, l, acc = carry
    m_new = jnp.maximum(m, jnp.max(s, axis=1, keepdims=True))
    alpha = jnp.exp(m - m_new)
    p = jnp.exp(s - m_new)
    l = alpha * l + jnp.sum(p, axis=1, keepdims=True)
    acc = alpha * acc + _dot(p.astype(BF16), v)
    return m_new, l, acc


def _flash_init(rows, width):
    return (jnp.full((rows, 1), NEG, F32), jnp.zeros((rows, 1), F32), jnp.zeros((rows, width), F32))


def _nsa_attn_kernel(q_ref, selb_ref, ksel_ref, kwin_ref, vcat_ref, tnear_ref, twin_ref, misc_ref, oc_ref,
                     ya_ref, qx_ref, sa_ref, sb_ref, *, n_super):
    i = pl.program_id(0)
    q0 = i * TQ
    for jj in range(n_super):
        sb = selb_ref[:, jj * 64:(jj + 1) * 64]
        for hh in range(NH):
            qx_ref[jj, hh * TQ:(hh + 1) * TQ, :] = jnp.concatenate([q_ref[hh], sb], axis=1)

    ks = jnp.maximum(q0 - TQ, 0)
    n_full = ks // TK

    def scores(j):
        k0 = pl.multiple_of(j * TK, TK)
        return _dot_nt(qx_ref[j // (SUPER // TK)], ksel_ref[pl.ds(k0, TK), :])

    def values(j):
        return vcat_ref[pl.ds(pl.multiple_of(WINDOW + j * TK, TK), TK), :]

    col = n_full * TK + lax.broadcasted_iota(jnp.int32, (1, TK), 1)
    hide = jnp.where(col < ks, 0.0, NEG)
    sa_ref[...] = scores(0)

    def pair_body(jj, carry):
        j = 2 * jj
        sb_ref[...] = scores(j + 1)
        carry = _flash_step(carry, sa_ref[...], values(j))
        sa_ref[...] = scores(j + 2)
        return _flash_step(carry, sb_ref[...], values(j + 1))

    carry = lax.fori_loop(0, n_full // 2, pair_body, _flash_init(NH * TQ, 128))

    def odd_tail(carry):
        sb_ref[...] = scores(n_full)
        carry = _flash_step(carry, sa_ref[...], values(n_full - 1))
        return _flash_step(carry, sb_ref[...] + hide, values(n_full))

    def even_tail(carry):
        return _flash_step(carry, sa_ref[...] + hide, values(n_full))

    carry = lax.cond(n_full % 2 == 1, odd_tail, even_tail, carry)
    parts = []
    for half in range(2):
        kh = pl.multiple_of(ks + half * TQ, TQ)
        parts.append(_dot_nt(qx_ref[kh // SUPER], ksel_ref[pl.ds(kh, TQ), :]))
    s = jnp.concatenate(parts, axis=1) + tnear_ref[jnp.minimum(i, 1)]
    _, l_s, acc_s = _flash_step(carry, s, vcat_ref[pl.ds(pl.multiple_of(WINDOW + ks, TQ), 2 * TQ), :])

    w0 = pl.multiple_of(q0, TQ)
    wpos = q0 - WINDOW + lax.broadcasted_iota(jnp.int32, (1, WINDOW + TQ), 1)
    s_w = _dot_nt(qx_ref[0], kwin_ref[pl.ds(w0, WINDOW + TQ), :]) + twin_ref[...] + jnp.where(wpos < 0, NEG, 0.0)
    p_w = jnp.exp(s_w - jnp.max(s_w, axis=1, keepdims=True))
    l_w = jnp.sum(p_w, axis=1, keepdims=True)
    acc_w = _dot(p_w.astype(BF16), vcat_ref[pl.ds(w0, WINDOW + TQ), :])

    o_s = acc_s[:, 0:HD] / l_s
    o_w = acc_w[:, HD:2 * HD] / l_w
    gates = misc_ref[...]
    for hh in range(NH):
        rows = slice(hh * TQ, (hh + 1) * TQ)
        y = oc_ref[hh] + gates[:, NH + hh:NH + hh + 1] * o_s[rows] + gates[:, 2 * NH + hh:2 * NH + hh + 1] * o_w[rows]
        ya_ref[hh] = y.astype(BF16)


def _nsa_attn(qh, selb, ksel, kwin, vcat, tnear, twin, misc, oc):
    l = qh.shape[1]
    ns = selb.shape[1]
    n_super = ns // 64
    heads = pl.BlockSpec((NH, TQ, HD), lambda i: (0, i, 0))
    return pl.pallas_call(
        functools.partial(_nsa_attn_kernel, n_super=n_super),
        grid=(l // TQ,),
        in_specs=[heads, pl.BlockSpec((TQ, ns), lambda i: (i, 0)),
                  _full((l, 128)), _full((l + WINDOW, 128)), _full((l + WINDOW, 128)),
                  _full((2, NH * TQ, 2 * TQ)), _full((NH * TQ, WINDOW + TQ)),
                  pl.BlockSpec((TQ, 128), lambda i: (i, 0)), heads],
        out_specs=heads,
        out_shape=jax.ShapeDtypeStruct((NH, l, HD), BF16),
        scratch_shapes=[pltpu.VMEM((n_super, NH * TQ, 128), BF16), pltpu.VMEM((NH * TQ, TK), F32),
                        pltpu.VMEM((NH * TQ, TK), F32)],
        compiler_params=_cparams(("arbitrary",)),
        name="nsa_attn",
    )(qh, selb, ksel, kwin, vcat, tnear, twin, misc, oc)


def _fox_attn_kernel(bnd_ref, q_ref, k_ref, v_ref, o_ref):
    pair = pl.program_id(0)
    i = pl.program_id(1)
    q0 = i * TF
    n_full = q0 // TKF
    qs = [q_ref[0], q_ref[1]]
    reach = []
    for hh in range(2):
        qf = qs[hh][:, 0:HD].astype(F32)
        reach.append(jnp.sqrt(jnp.sum(qf * qf, axis=1, keepdims=True)) * bnd_ref[2 * pair + hh, 0])

    def tile(j):
        k0 = pl.multiple_of(j * TKF, TKF)
        return [_dot_nt(qs[hh], k_ref[hh, pl.ds(k0, TKF), :]) for hh in range(2)], v_ref[pl.ds(k0, TKF), :]

    ss, v = tile(n_full)
    key = n_full * TKF + lax.broadcasted_iota(jnp.int32, ss[0].shape, 1)
    row = q0 + lax.broadcasted_iota(jnp.int32, ss[0].shape, 0)
    carry = tuple(_flash_step(_flash_init(TF, 128), jnp.where(key <= row, ss[hh], NEG), v) for hh in range(2))

    def reachable(j, carry):
        j = jnp.maximum(j, 0)
        gap = [jnp.max(reach[hh] + bnd_ref[2 * pair + hh, 1 + j] - carry[hh][0]) for hh in range(2)]
        return jnp.maximum(gap[0], gap[1]) > -FOX_CUT

    def body(state):
        j, _, carry = state
        ss, v = tile(j)
        carry = tuple(_flash_step(carry[hh], ss[hh], v) for hh in range(2))
        return j - 1, jnp.logical_and(j >= 1, reachable(j - 1, carry)), carry

    start = (n_full - 1, jnp.logical_and(n_full >= 1, reachable(n_full - 1, carry)), carry)
    _, _, carry = lax.while_loop(lambda state: state[1], body, start)
    outs = [carry[hh][2] / carry[hh][1] for hh in range(2)]
    lane = lax.broadcasted_iota(jnp.int32, outs[0].shape, 1)
    o_ref[...] = jnp.where(lane < HD, outs[0], outs[1]).astype(BF16)


def _fox_attn(bounds, fqx, fkx, fv):
    l = fqx.shape[1]
    return pl.pallas_call(
        _fox_attn_kernel,
        grid=(NH // 2, l // TF),
        in_specs=[pl.BlockSpec(memory_space=pltpu.SMEM),
                  pl.BlockSpec((2, TF, 128), lambda p, i: (p, i, 0)),
                  pl.BlockSpec((2, l, 128), lambda p, i: (p, 0, 0), pipeline_mode=pl.Buffered(1)),
                  pl.BlockSpec((l, 128), lambda p, i: (0, p), pipeline_mode=pl.Buffered(1))],
        out_specs=pl.BlockSpec((TF, 128), lambda p, i: (i, p)),
        out_shape=jax.ShapeDtypeStruct((l, MIX), BF16),
        compiler_params=_cparams(("arbitrary", "arbitrary")),
        name="fox_attn",
    )(bounds, fqx, fkx, fv)


def _layernorm_silu(y, ln_ref):
    mu = jnp.mean(y, axis=-1, keepdims=True)
    d = y - mu
    var = jnp.mean(d * d, axis=-1, keepdims=True)
    z = d * lax.rsqrt(var + EPS) * ln_ref[0] + ln_ref[1]
    return z * jax.nn.sigmoid(z)


def _pool_select(sums, counts, u):
    lane = lax.broadcasted_iota(jnp.int32, u.shape, 1)
    group = MIX // len(POOL_WINDOWS)
    out = sums[-1] / counts[-1]
    for g in range(len(POOL_WINDOWS) - 2, -1, -1):
        out = jnp.where(lane < (g + 1) * group, sums[g] / counts[g], out)
    return out - u


def _convpool_kernel(u_ref, uh_ref, up_ref, c_ref, ch_ref, cp_ref, cw_ref, cb_ref, ln_ref, pw_ref, ps_ref,
                     yb_ref, yc_ref, ext_ref, *, pos0):
    i = pl.program_id(0)
    tm = u_ref.shape[0]
    ext_ref[pl.ds(0, HALO), :] = jnp.where(i == 0, up_ref[...], uh_ref[...])
    ext_ref[pl.ds(HALO, tm), :] = u_ref[...]
    acc = jnp.zeros((tm, MIX), F32) + cb_ref[...]
    for w in range(CONV_WIDTH):
        acc = acc + ext_ref[pl.ds(HALO - CONV_HIST + w, tm), :] * cw_ref[pl.ds(w, 1), :]
    yb_ref[...] = _layernorm_silu(acc, ln_ref).astype(BF16)
    c = c_ref[...]
    ext_ref[pl.ds(0, HALO), :] = jnp.where(i == 0, cp_ref[...], ch_ref[...])
    ext_ref[pl.ds(HALO, tm), :] = c
    pos = pos0 + i * tm + lax.broadcasted_iota(jnp.int32, (tm, 1), 0)
    run = c
    sums, counts = [], []
    for k in range(1, max(POOL_WINDOWS)):
        run = run + ext_ref[pl.ds(HALO - k, tm), :]
        if k + 1 in POOL_WINDOWS:
            sums.append(run)
            counts.append(jnp.minimum(k + 1, pos + 1).astype(F32))
    pooled = _pool_select(sums, counts, c).astype(BF16)
    yc_ref[...] = (_dot(pooled, pw_ref[...]) * ps_ref[...]).astype(BF16)


def _convpool(u, u_past, c, c_past, cw, cb, ln, pw_bd, ps, pos0):
    l = u.shape[0]
    tm = min(TM, l)
    nh = tm // HALO
    row = pl.BlockSpec((tm, MIX), lambda i: (i, 0))
    halo = pl.BlockSpec((HALO, MIX), lambda i: (jnp.maximum(i * nh - 1, 0), 0))
    return pl.pallas_call(
        functools.partial(_convpool_kernel, pos0=pos0),
        grid=(l // tm,),
        in_specs=[row, halo, _full((HALO, MIX)), row, halo, _full((HALO, MIX)),
                  _full((HALO, MIX)), _full((1, MIX)), _full((2, 1, MIX)), _full((MIX, MIX)), _full((1, MIX))],
        out_specs=[row, row],
        out_shape=[jax.ShapeDtypeStruct((l, MIX), BF16)] * 2,
        scratch_shapes=[pltpu.VMEM((tm + HALO, MIX), F32)],
        compiler_params=_cparams(("arbitrary",)),
        name="convpool",
    )(u, u, u_past, c, c, c_past, cw, cb, ln, pw_bd, ps)


def _convpool_step_kernel(u_ref, up_ref, c_ref, cp_ref, cw_ref, cb_ref, ln_ref, pw_ref, ps_ref, yb_ref, yc_ref,
                          *, pos0):
    u = u_ref[...]
    acc = cb_ref[...] + u * cw_ref[pl.ds(CONV_WIDTH - 1, 1), :]
    for w in range(CONV_HIST):
        acc = acc + up_ref[w] * cw_ref[pl.ds(w, 1), :]
    yb_ref[...] = _layernorm_silu(acc, ln_ref).astype(BF16)
    c = c_ref[...]
    run = c
    sums, counts = [], []
    for k in range(1, max(POOL_WINDOWS)):
        run = run + cp_ref[POOL_HIST - k]
        if k + 1 in POOL_WINDOWS:
            sums.append(run)
            counts.append(float(min(k + 1, pos0 + 1)))
    pooled = _pool_select(sums, counts, c).astype(BF16)
    yc_ref[...] = (_dot(pooled, pw_ref[...]) * ps_ref[...]).astype(BF16)


def _convpool_step(u, u_past, c, c_past, cw, cb, ln, pw_bd, ps, pos0):
    b = u.shape[0]
    return pl.pallas_call(
        functools.partial(_convpool_step_kernel, pos0=pos0),
        out_shape=[jax.ShapeDtypeStruct((b, MIX), BF16)] * 2,
        name="convpool_step",
    )(u, u_past, c, c_past, cw, cb, ln, pw_bd, ps)


def _nsa_paged_cmp_kernel(pt_ref, *refs, n_pages, past):
    pages = refs[:PG]
    (q_ref, pos_ref, w_ref, ovl_ref, bias_ref, misc_ref, oc_ref, val_ref, r_ref, scr, tok_ref) = refs[PG:]
    p = pl.program_id(1)
    for g in range(PG):
        r0 = pl.multiple_of((p * PG + g) * 8, 8)
        tok_ref[...] = pages[g][0, 0].reshape(2 * HD, PAGE).T
        for t in range(CMP_STRIDE):
            r_ref[pl.ds(r0, 8), t * 2 * HD:(t + 1) * 2 * HD] = tok_ref[pl.ds(t, 8, stride=CMP_STRIDE), :]

    @pl.when(p == n_pages // PG - 1)
    def _():
        r = r_ref[...]
        n = r.shape[0]
        lo = _dot((r + pos_ref[0]).astype(BF16), w_ref[0])
        hi = _dot((r + pos_ref[1]).astype(BF16), w_ref[1])
        scr[pl.ds(0, n), :] = hi
        scr[pl.ds(n, 8), :] = jnp.zeros((8, 2 * HD), F32)
        kvc = (lo + scr[pl.ds(1, n), :]).astype(BF16)
        q = q_ref[0]
        s = _dot_nt(q, kvc) + bias_ref[...]
        m = jnp.max(s, axis=1, keepdims=True)
        e = jnp.exp(s - m)
        pr = (e / jnp.sum(e, axis=1, keepdims=True)).astype(BF16)
        o = _dot(pr, kvc)[:, HD:2 * HD]
        gates = misc_ref[0]
        oc_ref[0] = o * gates[:, 0:1]
        impf = _dot(pr, ovl_ref[...])
        imp = jnp.sum(impf[0:NH], axis=0, keepdims=True)
        ns = imp.shape[1]
        j = lax.broadcasted_iota(jnp.int32, imp.shape, 1)
        qblk = past // SLC_BLOCK
        forced = (j == 0) | (j == qblk) | (j == qblk - 1)
        val = jnp.where(forced, BIG, jnp.where(j * SLC_BLOCK <= past, imp, -BIG))
        val_ref[0] = jnp.where(j <= qblk, val, -jnp.inf)


def _nsa_paged_cmp(page_table, cache, layer, qx8, pos, w, ovl, bias, misc8, past):
    bsz, n_pages = page_table.shape
    n_rows = past // CMP_STRIDE
    ns = ovl.shape[1]
    wide = CMP_STRIDE * 2 * HD

    def page_spec(g):
        return pl.BlockSpec((1, 1, 2, HD, PAGE), lambda b, p, pt, _g=g: (layer, pt[b, p * PG + _g], 0, 0, 0))

    def const(shape):
        nd = len(shape)
        return pl.BlockSpec(shape, lambda b, p, pt, _nd=nd: (0,) * _nd)

    def per_b(shape):
        return pl.BlockSpec(shape, lambda b, p, pt: (b, 0, 0))

    grid_spec = pltpu.PrefetchScalarGridSpec(
        num_scalar_prefetch=1,
        grid=(bsz, n_pages // PG),
        in_specs=[page_spec(g) for g in range(PG)] + [
            per_b((1, 8, 2 * HD)), const((2, 1, wide)), const((2, wide, 2 * HD)), const((n_rows, ns)),
            const((8, n_rows)), per_b((1, 8, 128))],
        out_specs=[per_b((1, 8, HD)), per_b((1, 1, ns))],
        scratch_shapes=[pltpu.VMEM((n_rows, wide), F32), pltpu.VMEM((n_rows + 8, 2 * HD), F32),
                        pltpu.VMEM((PAGE, 2 * HD), F32)],
    )
    return pl.pallas_call(
        functools.partial(_nsa_paged_cmp_kernel, n_pages=n_pages, past=past),
        grid_spec=grid_spec,
        out_shape=[jax.ShapeDtypeStruct((bsz, 8, HD), F32), jax.ShapeDtypeStruct((bsz, 1, ns), F32)],
        compiler_params=_cparams(("arbitrary", "arbitrary")),
        name="nsa_paged_cmp",
    )(page_table, *([cache] * PG), qx8, pos, w, ovl, bias, misc8)


def _nsa_paged_attn_kernel(idx_ref, *refs, past):
    blocks = refs[:N_SELECT]
    (q_ref, new_ref, win_ref, neww_ref, relt_ref, wbias_ref, misc_ref, oc_ref, ya_ref) = refs[N_SELECT:]
    b = pl.program_id(0)
    q = q_ref[0]
    qf = q.astype(F32)

    def own_logit(k_row):
        return jnp.sum(qf * k_row.astype(BF16).astype(F32), axis=1, keepdims=True)

    def attend(s, s_own, values, v_own):
        m = jnp.maximum(jnp.max(s, axis=1, keepdims=True), s_own)
        e = jnp.exp(s - m)
        e_own = jnp.exp(s_own - m)
        inv = 1.0 / (jnp.sum(e, axis=1, keepdims=True) + e_own)
        pr = (e * inv).astype(BF16)
        o = (e_own * inv) * v_own.astype(BF16).astype(F32)
        for c, v in enumerate(values):
            o = o + _dot_nt(pr[:, c * PAGE:(c + 1) * PAGE], v)
        return o

    lane = lax.broadcasted_iota(jnp.int32, (1, PAGE), 1)
    s_parts, pos_parts, hide_parts, v_parts = [], [], [], []
    for t in range(N_SELECT):
        j = idx_ref[b, N_SELECT + t]
        s_parts.append(_dot(q, blocks[t][0, 0, 0].astype(BF16)))
        v_parts.append(blocks[t][0, 0, 1].astype(BF16))
        pos_parts.append(jnp.minimum(j // 2, past // PAGE - 1) * PAGE + lane)
        half = jnp.where(j < past // SLC_BLOCK, j % 2, 2)
        hide_parts.append(jnp.where(lane // SLC_BLOCK == half, 0.0, NEG))
    pos = jnp.concatenate(pos_parts, axis=1)
    s = jnp.concatenate(s_parts, axis=1) + _t5_bias(relt_ref[...], past - pos) + jnp.concatenate(hide_parts, axis=1)
    new = new_ref[0]
    bias0 = _t5_bias(relt_ref[...], jnp.zeros((1, PAGE), jnp.int32))[:, 0:1]
    o_s = attend(s, own_logit(new[:, 2 * HD:3 * HD]) + bias0, v_parts, new[:, 3 * HD:4 * HD])
    n_win = win_ref.shape[4]
    s_w = _dot(q, win_ref[0, 0, 0].astype(BF16)) + wbias_ref[...]
    vw = [win_ref[0, 0, 1, :, c * PAGE:(c + 1) * PAGE].astype(BF16) for c in range(n_win // PAGE)]
    neww = neww_ref[0]
    o_w = attend(s_w, own_logit(neww[:, 0:HD]) + bias0, vw, neww[:, HD:2 * HD])
    gates = misc_ref[0]
    ya_ref[0] = (oc_ref[0] + gates[:, 1:2] * o_s + gates[:, 2:3] * o_w).astype(BF16)


def _nsa_paged_attn(sel, cache, layer, q8, nsa_new, win_state, win_new, relt8, wbias, misc8, oc, past):
    bsz = sel.shape[0]
    n_win = win_state.shape[4]

    def blk_spec(t):
        return pl.BlockSpec((1, 1, 2, HD, PAGE), lambda b, ix, _t=t: (layer, ix[b, _t], 1, 0, 0))

    def const(shape):
        nd = len(shape)
        return pl.BlockSpec(shape, lambda b, ix, _nd=nd: (0,) * _nd)

    def per_b(shape):
        return pl.BlockSpec(shape, lambda b, ix: (b, 0, 0))

    win_spec = pl.BlockSpec((1, 1, 2, HD, n_win), lambda b, ix: (layer, b, 0, 0, 0))
    grid_spec = pltpu.PrefetchScalarGridSpec(
        num_scalar_prefetch=1,
        grid=(bsz,),
        in_specs=[blk_spec(t) for t in range(N_SELECT)] + [
            per_b((1, 8, HD)), per_b((1, 1, 256)), win_spec, per_b((1, 1, 128)),
            const((8, N_BUCKETS)), const((8, n_win)), per_b((1, 8, 128)), per_b((1, 8, HD))],
        out_specs=per_b((1, 8, HD)),
    )
    return pl.pallas_call(
        functools.partial(_nsa_paged_attn_kernel, past=past),
        grid_spec=grid_spec,
        out_shape=jax.ShapeDtypeStruct((bsz, 8, HD), BF16),
        compiler_params=_cparams(("arbitrary",)),
        name="nsa_paged_attn",
    )(sel, *([cache] * N_SELECT), q8, nsa_new, win_state, win_new, relt8, wbias, misc8, oc)


def _fox_paged_kernel(pt_ref, kv_hbm, lf_hbm, qcol_ref, new_ref, lfnew_ref, o_ref, kv_buf, lf_buf, sem, *,
                      layer, n_pages):
    b = pl.program_id(0)
    n_seq = pl.num_programs(0)
    n_chunk = n_pages // PG

    def copies(seq, chunk, slot):
        out = []
        for g in range(PG):
            page = pt_ref[seq, (n_chunk - 1 - chunk) * PG + g]
            out.append(pltpu.make_async_copy(kv_hbm.at[layer, page], kv_buf.at[slot, g], sem.at[slot]))
            out.append(pltpu.make_async_copy(lf_hbm.at[layer, page], lf_buf.at[slot, g], sem.at[slot]))
        return out

    @pl.when(b == 0)
    def _():
        for cp in copies(0, 0, 0):
            cp.start()

    q_col = qcol_ref[0]
    q_wide = jnp.broadcast_to(q_col, (MIX, PAGE))
    new = new_ref[0]
    row8 = lax.broadcasted_iota(jnp.int32, (8, 1), 0)

    def per_head(x):
        out = jnp.zeros((8, x.shape[1]), F32)
        for hh in range(NH):
            out = jnp.where(row8 == hh, jnp.sum(x[hh * HD:(hh + 1) * HD], axis=0, keepdims=True), out)
        return out

    def per_feature(x, width):
        return jnp.concatenate([jnp.broadcast_to(x[hh:hh + 1], (HD, width)) for hh in range(NH)], axis=0)

    m0 = per_head(q_col * new[0:MIX])
    acc0 = jnp.where(lax.broadcasted_iota(jnp.int32, (MIX, PAGE), 1) == 0, new[MIX:2 * MIX], 0.0)
    later = jnp.where(lax.broadcasted_iota(jnp.int32, (PAGE, PAGE), 0) > lax.broadcasted_iota(jnp.int32, (PAGE, PAGE), 1),
                      1.0, 0.0).astype(BF16)

    def body(c, state):
        m_old, l_old, acc, carry = state
        slot = c % 2

        @pl.when(c + 1 < n_chunk)
        def _():
            for cp in copies(b, c + 1, 1 - slot):
                cp.start()

        @pl.when(jnp.logical_and(c + 1 == n_chunk, b + 1 < n_seq))
        def _():
            for cp in copies(b + 1, 0, 1 - slot):
                cp.start()

        for cp in copies(b, c, slot):
            cp.wait()
        totals = [jnp.sum(lf_buf[slot, g], axis=1, keepdims=True) for g in range(PG)]
        newer = [None] * PG
        for g in range(PG - 1, -1, -1):
            newer[g] = carry
            carry = carry + totals[g]
        within = _dot01(lf_buf[slot].reshape(PG * 8, PAGE), later)
        s_parts = []
        for g in range(PG):
            s_parts.append(per_head(kv_buf[slot, g, 0] * q_wide) + newer[g] + within[g * 8:(g + 1) * 8])
        s = jnp.concatenate(s_parts, axis=1)
        m_new = jnp.maximum(m_old, jnp.max(s, axis=1, keepdims=True))
        alpha = jnp.exp(m_old - m_new)
        pr = jnp.exp(s - m_new)
        l_new = alpha * l_old + jnp.sum(pr, axis=1, keepdims=True)
        acc = acc * per_feature(alpha, 1)
        for g in range(PG):
            acc = acc + kv_buf[slot, g, 1] * per_feature(pr[:, g * PAGE:(g + 1) * PAGE], PAGE)
        return m_new, l_new, acc, carry

    _, l, acc, _ = lax.fori_loop(0, n_chunk, body, (m0, jnp.ones((8, 1), F32), acc0, lfnew_ref[0]))
    o_ref[0] = jnp.sum(acc, axis=1, keepdims=True) / per_feature(l, 1)


def _fox_paged(page_table, cache_kv, cache_lf, layer, q_col, kv_new, lf_new):
    bsz, n_pages = page_table.shape
    assert (n_pages // PG) % 2 == 0

    def per_b(shape):
        return pl.BlockSpec(shape, lambda b, pt: (b, 0, 0))

    grid_spec = pltpu.PrefetchScalarGridSpec(
        num_scalar_prefetch=1,
        grid=(bsz,),
        in_specs=[pl.BlockSpec(memory_space=pl.ANY), pl.BlockSpec(memory_space=pl.ANY),
                  per_b((1, MIX, 1)), per_b((1, 2 * MIX, 1)), per_b((1, 8, 1))],
        out_specs=per_b((1, MIX, 1)),
        scratch_shapes=[pltpu.VMEM((2, PG, 2, MIX, PAGE), F32), pltpu.VMEM((2, PG, 8, PAGE), F32),
                        pltpu.SemaphoreType.DMA((2,))],
    )
    return pl.pallas_call(
        functools.partial(_fox_paged_kernel, layer=layer, n_pages=n_pages),
        grid_spec=grid_spec,
        out_shape=jax.ShapeDtypeStruct((bsz, MIX, 1), F32),
        compiler_params=_cparams(("arbitrary",)),
        name="fox_paged",
    )(page_table, cache_kv, cache_lf, q_col, kv_new, lf_new)


def _layer_params(l, norm_g, ffn_gate, ffn_up, ffn_down, w_in, fox_f_bias, nsa_cmp_pos, nsa_cmp_w,
                  conv_w, conv_b, conv_ln, pool_w, pool_scale, w_branch, w_out):
    wi = w_in[l]
    o_aq, o_akv, o_ag, o_b, o_c, o_d, o_f, o_m = np.cumsum([0, 256, 384, 12, 512, 256, 768, 4])
    misc = jnp.concatenate([wi[:, o_ag:o_ag + 12], wi[:, o_f:o_f + 4], jnp.zeros((D_MODEL, 112), F32)], axis=1)
    w_small = jnp.concatenate([wi[:, o_aq:o_aq + 256], wi[:, o_akv:o_akv + 384], misc, wi[:, o_b:o_b + 512],
                               wi[:, o_c:o_c + 256], wi[:, o_d:o_d + 768]], axis=1).astype(BF16)
    fb_row = jnp.zeros((1, 128), F32).at[0, 12:16].set(fox_f_bias[l])
    cw = nsa_cmp_w[l].reshape(2, 2, CMP_STRIDE * HD, HD).astype(BF16)
    cpos = nsa_cmp_pos[l].reshape(2, 2, 1, CMP_STRIDE * HD)
    w4 = nsa_cmp_w[l].reshape(2, 2, CMP_STRIDE, HD, HD)
    none = jnp.zeros_like(w4[0])
    w_pair = jnp.concatenate([jnp.concatenate([w4[0], none], axis=-1), jnp.concatenate([none, w4[1]], axis=-1)], axis=2)
    p4 = nsa_cmp_pos[l].reshape(2, 2, CMP_STRIDE, HD)
    pos_pair = jnp.concatenate([p4[0], p4[1]], axis=-1).reshape(2, 1, CMP_STRIDE * 2 * HD)
    group = MIX // len(POOL_WINDOWS)
    pw_bd = jnp.zeros((MIX, MIX), F32)
    for g in range(len(POOL_WINDOWS)):
        pw_bd = pw_bd.at[g * group:(g + 1) * group, g * group:(g + 1) * group].set(pool_w[l, g])
    cw_conv = jnp.concatenate([conv_w[l], jnp.zeros((HALO - CONV_WIDTH, MIX), F32)], axis=0)
    return dict(
        g=[norm_g[l, k].reshape(1, D_MODEL) for k in range(3)],
        ffn=[(ffn_gate[l, k].astype(BF16), ffn_up[l, k].astype(BF16), ffn_down[l, k].astype(BF16)) for k in range(2)],
        w_small=w_small, w_merge=wi[:, o_m:].astype(BF16), fb_row=fb_row, cmp_w=cw, cmp_pos=cpos,
        cmp_w_pair=w_pair.reshape(2, CMP_STRIDE * 2 * HD, 2 * HD).astype(BF16), cmp_pos_pair=pos_pair,
        conv_w=cw_conv, conv_b=conv_b[l].reshape(1, MIX), conv_ln=conv_ln[l].reshape(2, 1, MIX),
        pool_w=pw_bd.astype(BF16), pool_scale=pool_scale[l].reshape(1, MIX),
        w_branch=w_branch[l].astype(BF16), w_out=w_out[l].astype(BF16))


def _overlap(n_rows, n_blocks, n_pad):
    cs = np.arange(n_rows)[:, None] * CMP_STRIDE
    j0 = np.arange(n_pad)[None, :] * SLC_BLOCK
    ov = (cs < j0 + SLC_BLOCK) & (cs + CMP_LEN > j0) & (np.arange(n_pad)[None, :] < n_blocks)
    return jnp.asarray(ov.astype(np.float32), dtype=BF16)


def _lookup(bt, dist, valid):
    d = np.clip(dist, 0, NDIST - 1)
    return jnp.where(jnp.asarray(valid)[None], bt[:, d], NEG)


def _toeplitz_kernel(rel_ref, o_ref, *, base, vstep, stride, hi):
    width = o_ref.shape[2]
    r = lax.broadcasted_iota(jnp.int32, (TQ, width), 0)
    m = lax.broadcasted_iota(jnp.int32, (TQ, width), 1)
    dist = base + vstep * pl.program_id(0) + r - stride * m
    bucket = _t5_bucket(dist)
    hide = jnp.where(dist < 0, NEG, jnp.where(dist > hi, NEG, 0.0))
    for hh in range(NH):
        far = rel_ref[N_BUCKETS - 1, hh]
        val = jnp.zeros((TQ, width), F32)
        for b in range(N_BUCKETS - 1):
            val = jnp.where(bucket == b, rel_ref[b, hh] - far, val)
        o_ref[0, hh * TQ:(hh + 1) * TQ, :] = val + hide


def _toeplitz(rel_bias, n_var, width, base, vstep, stride, hi):
    return pl.pallas_call(
        functools.partial(_toeplitz_kernel, base=base, vstep=vstep, stride=stride, hi=hi),
        grid=(n_var,),
        in_specs=[pl.BlockSpec(memory_space=pltpu.SMEM)],
        out_specs=pl.BlockSpec((1, NH * TQ, width), lambda v: (v, 0, 0)),
        out_shape=jax.ShapeDtypeStruct((n_var, NH * TQ, width), F32),
        compiler_params=_cparams(("arbitrary",)),
        name="t5_tables",
    )(rel_bias)


def _prompt_tables(rel_bias):
    big = 1 << 30
    tab_c = _toeplitz(rel_bias, 33, 256, -(CMP_LEN - 1), TQ, CMP_STRIDE, big)
    tnear = _toeplitz(rel_bias, 2, 2 * TQ, 0, TQ, 1, big)
    twin = _toeplitz(rel_bias, 1, WINDOW + TQ, WINDOW, 0, 1, WINDOW)[0]
    return tab_c, tnear, twin


def _prompt_layer(x, lp, tables, fg, final):
    l = x.shape[0]
    tab_c, tnear, twin = tables
    x = _ffn(x, lp["g"][0], *lp["ffn"][0], fg)
    (qn, nsa_kv, win_kv, misc, u, cin, fq, fox_kv, ksel, kwin, vcat, fk, fv, stat) = _inproj(x, lp["g"][1], lp["w_small"], lp["fb_row"])
    n_rows = l // CMP_STRIDE
    ns = l // SLC_BLOCK
    rk = nsa_kv[:, 0:HD].reshape(n_rows, CMP_STRIDE * HD)
    rv = nsa_kv[:, HD:2 * HD].reshape(n_rows, CMP_STRIDE * HD)
    kc, vc = _compress_prompt(rk, rv, lp["cmp_pos"], lp["cmp_w"])
    oc, val = _cmp_attn(qn, kc, vc, _overlap(n_rows, ns, ns), tab_c, misc)
    selb = _topk_mask(val)
    front = jnp.zeros((WINDOW, 128), BF16)
    ya = _nsa_attn(qn, selb, ksel, jnp.concatenate([front, kwin]), jnp.concatenate([front, vcat]), tnear, twin, misc, oc)
    zeros = jnp.zeros((HALO, MIX), F32)
    yb, yc = _convpool(u, zeros, cin, zeros, lp["conv_w"], lp["conv_b"], lp["conv_ln"], lp["pool_w"], lp["pool_scale"], 0)
    logf = misc[:, 12:16]
    per_chunk = TKF // min(TM, l)
    key_reach = jnp.sqrt(jnp.max(stat[:, 1, 0:NH], axis=0)) * 1.001
    forgot = -stat[per_chunk - 1::per_chunk, 0, 3 * NH:4 * NH]
    yd = _fox_attn(jnp.concatenate([key_reach[:, None], forgot.T], axis=1), fq, fk, fv)
    x = _merge(x, lp["g"][1], lp["w_merge"], lp["w_branch"], lp["w_out"], ya, yb, yc, yd)
    x = _ffn(x, lp["g"][2], *lp["ffn"][1], fg, final)
    state = (nsa_kv, fox_kv, logf, win_kv[l - min(WINDOW, l):], u[l - CONV_HIST:], cin[l - POOL_HIST:])
    return x, state


def _sample_layer(x, lp, relt8, bt, fg, final, layer, page_table, nsa_pages, fox_pages, lf_pages, win_rows, st_win,
                  st_conv, st_pool):
    bsz = x.shape[0]
    n_pages = page_table.shape[1]
    past = n_pages * PAGE
    x = _ffn(x, lp["g"][0], *lp["ffn"][0], fg)
    (qn, nsa_kv, win_kv, misc, u, cin, fq, fox_kv, _, _, _, _, _, _) = _inproj(x, lp["g"][1], lp["w_small"], lp["fb_row"])
    pad4 = lambda a: jnp.concatenate([a, jnp.zeros((bsz, 8 - NH) + a.shape[2:], a.dtype)], axis=1)
    q8 = pad4(jnp.transpose(qn, (1, 0, 2)))
    gates = misc[:, 0:3 * NH].reshape(bsz, 3, NH)
    misc8 = jnp.zeros((bsz, 8, 128), F32).at[:, 0:NH, 0:3].set(jnp.transpose(gates, (0, 2, 1)))
    n_rows = past // CMP_STRIDE
    ns = past // SLC_BLOCK + 1
    ns_pad = -(-ns // 128) * 128
    c_end = np.arange(n_rows) * CMP_STRIDE + CMP_LEN - 1
    valid_c = (c_end <= past) & (np.arange(n_rows) < n_rows - 1)
    bias_c = jnp.concatenate([_lookup(bt, past - c_end, valid_c), jnp.full((8 - NH, n_rows), NEG, F32)], axis=0)
    oc, val = _nsa_paged_cmp(page_table, nsa_pages, layer, jnp.concatenate([q8, jnp.zeros_like(q8)], axis=2),
                             lp["cmp_pos_pair"], lp["cmp_w_pair"], _overlap(n_rows, ns, ns_pad), bias_c, misc8, past)
    idx = _topk_index(val.reshape(bsz, ns_pad), min(N_SELECT, ns))
    n_win = st_win.shape[1]
    wbias = jnp.concatenate([_lookup(bt, n_win - np.arange(n_win), np.ones(n_win, bool)),
                             jnp.full((8 - NH, n_win), NEG, F32)], axis=0)
    blk = idx.reshape(bsz, 128)[:, 0:N_SELECT]
    held = jnp.minimum(blk // 2, n_pages - 1)
    pages = jnp.take_along_axis(page_table, held, axis=1)
    ya = _nsa_paged_attn(jnp.concatenate([pages, blk], axis=1), nsa_pages, layer, q8, nsa_kv.reshape(bsz, 1, 256),
                         win_rows, win_kv.reshape(bsz, 1, 128), relt8, wbias, misc8, oc, past)
    ya = jnp.transpose(ya[:, 0:NH], (1, 0, 2))
    yb, yc = _convpool_step(u, jnp.transpose(st_conv, (1, 0, 2)), cin, jnp.transpose(st_pool, (1, 0, 2)),
                            lp["conv_w"], lp["conv_b"], lp["conv_ln"], lp["pool_w"], lp["pool_scale"], past)
    logf = misc[:, 12:16]
    q_col = jnp.transpose(fq[:, :, 0:HD], (1, 0, 2)).reshape(bsz, MIX, 1).astype(F32)
    lf_new = pad4(logf.reshape(bsz, NH, 1))
    yd = _fox_paged(page_table, fox_pages, lf_pages, layer, q_col, fox_kv.reshape(bsz, 2 * MIX, 1), lf_new)
    yd = yd.reshape(bsz, MIX).astype(BF16)
    x = _merge(x, lp["g"][1], lp["w_merge"], lp["w_branch"], lp["w_out"], ya, yb, yc, yd)
    x = _ffn(x, lp["g"][2], *lp["ffn"][1], fg, final)
    new_win =jnp.concatenate([st_win.reshape(bsz, n_win, 128)[:, 1:], win_kv[:, None, :]], axis=1)
    new_conv = jnp.concatenate([st_conv[:, 1:], u[:, None, :]], axis=1)
    new_pool = jnp.concatenate([st_pool[:, 1:], cin[:, None, :]], axis=1)
    state = (nsa_kv, fox_kv, logf, new_win, new_conv, new_pool)
    return x, state


def kernel(x_prompt, x_sample, cache_nsa, cache_fox_kv, cache_fox_logf, state_nsa_win, state_conv, state_pool,
           page_table, norm_g, ffn_gate, ffn_up, ffn_down, w_in, fox_f_bias, nsa_cmp_pos, nsa_cmp_w, rel_bias,
           conv_w, conv_b, conv_ln, pool_w, pool_scale, w_branch, w_out, final_norm_g):
    assert x_prompt.shape[0] == 1 and x_sample.shape[1] == 1
    depth = norm_g.shape[0]
    l = x_prompt.shape[1]
    bsz = x_sample.shape[0]
    win_keep = state_nsa_win.shape[2]
    assert l % SUPER == 0 and page_table.shape[1] * PAGE >= max(WINDOW, SUPER) and win_keep == WINDOW
    relt8 = jnp.concatenate([rel_bias.T, jnp.zeros((8 - NH, N_BUCKETS), F32)], axis=0)
    bt = _bias_table(relt8)[0:NH]
    tables = _prompt_tables(rel_bias)
    fg = final_norm_g.reshape(1, D_MODEL)
    xp = x_prompt.reshape(l, D_MODEL)
    xs = x_sample.reshape(bsz, D_MODEL)
    n_phys = cache_nsa.shape[1]
    nsa_pages = jnp.transpose(cache_nsa, (0, 1, 3, 4, 5, 2)).reshape(depth, n_phys, 4, HD, PAGE)
    fox_pages = jnp.transpose(cache_fox_kv, (0, 1, 3, 4, 5, 2)).reshape(depth, n_phys, 2, MIX, PAGE)
    lf_pages = jnp.pad(jnp.transpose(cache_fox_logf, (0, 1, 3, 2)), ((0, 0), (0, 0), (0, 8 - NH), (0, 0)))
    win_rows = jnp.transpose(state_nsa_win, (0, 1, 3, 4, 5, 2)).reshape(depth, bsz, 2, HD, win_keep)
    st_p, st_s = [], []
    for layer in range(depth):
        lp = _layer_params(layer, norm_g, ffn_gate, ffn_up, ffn_down, w_in, fox_f_bias, nsa_cmp_pos, nsa_cmp_w,
                           conv_w, conv_b, conv_ln, pool_w, pool_scale, w_branch, w_out)
        final = layer == depth - 1
        xp, sp = _prompt_layer(xp, lp, tables, fg, final)
        st_p.append(sp)
        xs, ss = _sample_layer(xs, lp, relt8, bt, fg, final, layer, page_table, nsa_pages, fox_pages, lf_pages, win_rows,
                               state_nsa_win[layer], state_conv[layer], state_pool[layer])
        st_s.append(ss)

    def stack(states, k, shape):
        return jnp.stack([s[k] for s in states]).reshape(shape)

    return (xp.reshape(1, l, D_MODEL), xs.reshape(bsz, 1, D_MODEL),
            stack(st_p, 0, (depth, 1, l, 4, 1, HD)), stack(st_s, 0, (depth, bsz, 1, 4, 1, HD)),
            stack(st_p, 1, (depth, 1, l, 2, NH, HD)), stack(st_s, 1, (depth, bsz, 1, 2, NH, HD)),
            stack(st_p, 2, (depth, 1, l, NH)), stack(st_s, 2, (depth, bsz, 1, NH)),
            stack(st_p, 3, (depth, 1, win_keep, 2, 1, HD)), stack(st_s, 3, (depth, bsz, win_keep, 2, 1, HD)),
            stack(st_p, 4, (depth, 1, CONV_HIST, MIX)), stack(st_s, 4, (depth, bsz, CONV_HIST, MIX)),
            stack(st_p, 5, (depth, 1, POOL_HIST, MIX)), stack(st_s, 5, (depth, bsz, POOL_HIST, MIX)))
```

```python
import functools
import math

import numpy as np
import jax
import jax.numpy as jnp
from jax import lax
from jax.experimental import pallas as pl
from jax.experimental.pallas import tpu as pltpu

F32 = jnp.float32
BF16 = jnp.bfloat16

D_MODEL = 1024
N_BRANCH = 4
MIX = D_MODEL // 4
HD = 64
NH = MIX // HD
CMP_STRIDE = 16
CMP_LEN = 32
SLC_BLOCK = 64
N_SELECT = 16
WINDOW = 512
N_BUCKETS = 32
REL_EXACT = 16
REL_MAX_DIST = 128
CONV_WIDTH = 31
CONV_HIST = CONV_WIDTH - 1
POOL_WINDOWS = (2, 4, 8, 16)
POOL_HIST = 15
D_FF = 2816
EPS = 1e-6
NEG = -1e30
BIG = 1e6
PAGE = 128

LANE = 128
TQ = 128
TOPK_ROWS = 512
TK = 1024
SUPER = 64 * SLC_BLOCK
TF = 256
TKF = 1024
FOX_CUT = 100.0
TM = 256
TM_WIDE = 512
FF_CHUNK = 1408
HALO = 32
NDIST = 1024
PG = 16
PGF = 32
VMEM_LIMIT = 56 * 1024 * 1024

C_AQ, C_NSA, C_WIN, C_MISC, C_CONV, C_POOL, C_FOX = 0, 256, 512, 640, 768, 1280, 1536
N_SMALL = 2304


def _cparams(sem):
    return pltpu.CompilerParams(dimension_semantics=sem, vmem_limit_bytes=VMEM_LIMIT)


def _full(shape, single=True):
    nd = len(shape)
    kw = dict(pipeline_mode=pl.Buffered(1)) if single else {}
    return pl.BlockSpec(shape, lambda *a, _nd=nd: (0,) * _nd, **kw)


def _rms(x, g):
    ms = jnp.mean(x * x, axis=-1, keepdims=True)
    return x * lax.rsqrt(ms + EPS) * g


def _dot(a, b):
    return jnp.dot(a, b, preferred_element_type=F32)


def _dot_nt(a, b):
    return lax.dot_general(a, b, (((1,), (1,)), ((), ())), preferred_element_type=F32)


def _split3(x):
    hi = x.astype(BF16)
    r1 = x - hi.astype(F32)
    mid = r1.astype(BF16)
    lo = (r1 - mid.astype(F32)).astype(BF16)
    return hi, mid, lo


def _dot01(x, ones_bf16):
    hi, mid, lo = _split3(x)
    return _dot(hi, ones_bf16) + _dot(mid, ones_bf16) + _dot(lo, ones_bf16)


def _dot01_l(ones_bf16, x):
    hi, mid, lo = _split3(x)
    return _dot(ones_bf16, hi) + _dot(ones_bf16, mid) + _dot(ones_bf16, lo)


def _log_sigmoid(x):
    return jnp.minimum(x, 0.0) - jnp.log1p(jnp.exp(-jnp.abs(x)))


def _t5_bucket(dist):
    n = jnp.maximum(dist, 0)
    ratio = jnp.log(jnp.maximum(n, 1).astype(F32) / REL_EXACT) / math.log(REL_MAX_DIST / REL_EXACT)
    large = REL_EXACT + (ratio * (N_BUCKETS - REL_EXACT)).astype(jnp.int32)
    return jnp.where(n < REL_EXACT, n, jnp.minimum(large, N_BUCKETS - 1))


def _t5_bias(relt, dist):
    bucket = _t5_bucket(dist)
    rows = lax.broadcasted_iota(jnp.int32, (N_BUCKETS, dist.shape[1]), 0)
    onehot = jnp.where(rows == bucket, 1.0, 0.0).astype(BF16)
    return _dot01(relt, onehot)


def _ffn_kernel(x_ref, g_ref, wg_ref, wu_ref, wd_ref, fg_ref, o_ref, *, final):
    x = x_ref[...]
    h = _rms(x, g_ref[...]).astype(BF16)
    acc = jnp.zeros_like(x)
    for c in range(D_FF // FF_CHUNK):
        sl = slice(c * FF_CHUNK, (c + 1) * FF_CHUNK)
        a = _dot(h, wg_ref[:, sl])
        b = _dot(h, wu_ref[:, sl])
        t = (a * jax.nn.sigmoid(a) * b).astype(BF16)
        acc = acc + _dot(t, wd_ref[sl, :])
    out = x + 0.5 * acc
    o_ref[...] = _rms(out, fg_ref[...]) if final else out


def _ffn(x, g, wg, wu, wd, fg, final=False):
    m = x.shape[0]
    tm = min(TM_WIDE, m)
    row = pl.BlockSpec((tm, D_MODEL), lambda i: (i, 0))
    return pl.pallas_call(
        functools.partial(_ffn_kernel, final=final),
        grid=(m // tm,),
        in_specs=[row, _full((1, D_MODEL)), _full((D_MODEL, D_FF)), _full((D_MODEL, D_FF)),
                  _full((D_FF, D_MODEL)), _full((1, D_MODEL))],
        out_specs=row,
        out_shape=jax.ShapeDtypeStruct((m, D_MODEL), F32),
        compiler_params=_cparams(("arbitrary",)),
        name="ffn",
    )(x, g, wg, wu, wd, fg)


def _inproj_kernel(x_ref, g_ref, w_ref, fb_ref,
                   qn_ref, nsa_ref, win_ref, misc_ref, u_ref, cin_ref, fq_ref, fkv_ref,
                   ksel_ref, kwin_ref, vcat_ref, fk_ref, fv_ref, stat_ref, csum_ref):
    tm = x_ref.shape[0]
    h = _rms(x_ref[...], g_ref[...]).astype(BF16)
    z = _dot(h, w_ref[...])
    scale = HD ** -0.5
    zm = z[:, C_MISC:C_MISC + 128]
    lane = lax.broadcasted_iota(jnp.int32, zm.shape, 1)
    logf = _log_sigmoid(zm + fb_ref[...])
    misc_ref[...] = jnp.where(lane < 3 * NH, jax.nn.sigmoid(zm), logf)
    @pl.when(pl.program_id(0) == 0)
    def _():
        csum_ref[...] = jnp.zeros_like(csum_ref)
    tri = jnp.where(lax.broadcasted_iota(jnp.int32, (tm, tm), 1) <= lax.broadcasted_iota(jnp.int32, (tm, tm), 0),
                    1.0, 0.0).astype(BF16)
    lf = jnp.where((lane >= 3 * NH) & (lane < 4 * NH), logf, 0.0)
    csum = _dot01_l(tri, lf) + csum_ref[...]
    csum_ref[...] = csum[tm - 1:tm, :]
    feat = lax.broadcasted_iota(jnp.int32, (tm, HD), 1)
    ones3 = jnp.where(feat < 3, 1.0, 0.0).astype(BF16)
    knorm = jnp.zeros((1, 128), F32)
    for hh in range(NH):
        qn_ref[hh] = (z[:, C_AQ + HD * hh:C_AQ + HD * (hh + 1)] * scale).astype(BF16)
        fq = (z[:, C_FOX + HD * hh:C_FOX + HD * (hh + 1)] * scale).astype(BF16)
        fq_ref[hh] = jnp.concatenate([fq, ones3], axis=1)
        hi, mid, lo = (p.astype(F32) for p in _split3(-csum[:, 3 * NH + hh:3 * NH + hh + 1]))
        extra = jnp.where(feat == 0, hi, jnp.where(feat == 1, mid, jnp.where(feat == 2, lo, 0.0)))
        fk = z[:, C_FOX + MIX + HD * hh:C_FOX + MIX + HD * (hh + 1)].astype(BF16)
        fk_ref[hh] = jnp.concatenate([fk, extra.astype(BF16)], axis=1)
        fkf = fk.astype(F32)
        k2 = jnp.max(jnp.sum(fkf * fkf, axis=1, keepdims=True), axis=0, keepdims=True)
        knorm = jnp.where(lane[0:1] == hh, k2, knorm)
    row8 = lax.broadcasted_iota(jnp.int32, (8, 128), 0)
    stat_ref[0] = jnp.where(row8 == 0, csum[tm - 1:tm, :], jnp.where(row8 == 1, knorm, 0.0))
    fv_ref[...] = z[:, C_FOX + 2 * MIX:C_FOX + 3 * MIX].astype(BF16)
    nsa = z[:, C_NSA:C_NSA + 256]
    nsa_ref[...] = nsa
    win = z[:, C_WIN:C_WIN + 128]
    win_ref[...] = win
    u_ref[...] = z[:, C_CONV:C_CONV + MIX] * jax.nn.sigmoid(z[:, C_CONV + MIX:C_CONV + 2 * MIX])
    cin_ref[...] = z[:, C_POOL:C_POOL + MIX]
    fkv_ref[...] = z[:, C_FOX + MIX:C_FOX + 3 * MIX]
    t = pl.program_id(0) * tm + lax.broadcasted_iota(jnp.int32, (tm, HD), 0)
    feat = lax.broadcasted_iota(jnp.int32, (tm, HD), 1)
    onehot = jnp.where(((t // SLC_BLOCK) % 64) == feat, 1.0, 0.0).astype(BF16)
    ksel_ref[...] = jnp.concatenate([nsa[:, 128:192].astype(BF16), onehot], axis=1)
    kwin_ref[...] = jnp.concatenate([win[:, 0:64].astype(BF16), jnp.zeros((tm, HD), BF16)], axis=1)
    vcat_ref[...] = jnp.concatenate([nsa[:, 192:256].astype(BF16), win[:, 64:128].astype(BF16)], axis=1)


def _inproj(x, g, w_small, fbias_row):
    m = x.shape[0]
    tm = min(TM, m)

    def row(n):
        return pl.BlockSpec((tm, n), lambda i: (i, 0))

    heads = pl.BlockSpec((NH, tm, HD), lambda i: (0, i, 0))
    wide = pl.BlockSpec((NH, tm, 128), lambda i: (0, i, 0))
    shapes = [((NH, m, HD), BF16, heads), ((m, 256), F32, row(256)), ((m, 128), F32, row(128)),
              ((m, 128), F32, row(128)), ((m, MIX), F32, row(MIX)), ((m, MIX), F32, row(MIX)),
              ((NH, m, 128), BF16, wide), ((m, 2 * MIX), F32, row(2 * MIX)),
              ((m, 128), BF16, row(128)), ((m, 128), BF16, row(128)), ((m, 128), BF16, row(128)),
              ((NH, m, 128), BF16, wide), ((m, MIX), BF16, row(MIX)),
              ((m // tm, 8, 128), F32, pl.BlockSpec((1, 8, 128), lambda i: (i, 0, 0)))]
    return pl.pallas_call(
        _inproj_kernel,
        grid=(m // tm,),
        in_specs=[row(D_MODEL), _full((1, D_MODEL)), _full((D_MODEL, N_SMALL)), _full((1, 128))],
        out_specs=[s[2] for s in shapes],
        out_shape=[jax.ShapeDtypeStruct(s[0], s[1]) for s in shapes],
        scratch_shapes=[pltpu.VMEM((1, 128), F32)],
        compiler_params=_cparams(("arbitrary",)),
        name="inproj",
    )(x, g, w_small, fbias_row)


def _merge_kernel(x_ref, g_ref, wm_ref, wb_ref, wo_ref, ya_ref, yb_ref, yc_ref, yd_ref, o_ref):
    x = x_ref[...]
    h = _rms(x, g_ref[...]).astype(BF16)
    mix = jnp.zeros_like(x)
    for n in range(N_BRANCH):
        gate = jax.nn.sigmoid(_dot(h, wm_ref[:, n * D_MODEL:(n + 1) * D_MODEL]))
        if n == 0:
            proj = jnp.zeros_like(x)
            for hh in range(NH):
                proj = proj + _dot(ya_ref[hh], wb_ref[n, hh * HD:(hh + 1) * HD, :])
        else:
            proj = _dot((yb_ref, yc_ref, yd_ref)[n - 1][...], wb_ref[n])
        mix = mix + gate * proj
    o_ref[...] = x + _dot(mix.astype(BF16), wo_ref[...])


def _merge(x, g, w_merge, w_branch, w_out, ya, yb, yc, yd):
    m = x.shape[0]
    tm = min(TM_WIDE, m)
    row = pl.BlockSpec((tm, D_MODEL), lambda i: (i, 0))
    heads = pl.BlockSpec((NH, tm, HD), lambda i: (0, i, 0))
    mixrow = pl.BlockSpec((tm, MIX), lambda i: (i, 0))
    return pl.pallas_call(
        _merge_kernel,
        grid=(m // tm,),
        in_specs=[row, _full((1, D_MODEL)), _full((D_MODEL, N_BRANCH * D_MODEL)),
                  _full((N_BRANCH, MIX, D_MODEL)), _full((D_MODEL, D_MODEL)),
                  heads, mixrow, mixrow, mixrow],
        out_specs=row,
        out_shape=jax.ShapeDtypeStruct((m, D_MODEL), F32),
        compiler_params=_cparams(("arbitrary",)),
        name="merge",
    )(x, g, w_merge, w_branch, w_out, ya, yb, yc, yd)


def _bias_table_kernel(relt_ref, o_ref):
    dist = lax.broadcasted_iota(jnp.int32, (1, NDIST), 1)
    o_ref[...] = _t5_bias(relt_ref[...], dist)


def _bias_table(relt8):
    return pl.pallas_call(
        _bias_table_kernel,
        out_shape=jax.ShapeDtypeStruct((8, NDIST), F32),
        name="bias_table",
    )(relt8)


def _compress(r, pos_ref, w_ref, t, scr):
    n = r.shape[0]
    a = _dot((r + pos_ref[t, 0]).astype(BF16), w_ref[t, 0])
    b = _dot((r + pos_ref[t, 1]).astype(BF16), w_ref[t, 1])
    scr[pl.ds(0, n), :] = b
    scr[pl.ds(n, 8), :] = jnp.zeros((8, HD), F32)
    return a + scr[pl.ds(1, n), :]


def _compress_kernel(rk_ref, rv_ref, pos_ref, w_ref, kc_ref, vc_ref, scr):
    kc_ref[...] = _compress(rk_ref[...], pos_ref, w_ref, 0, scr).astype(BF16)
    vc_ref[...] = _compress(rv_ref[...], pos_ref, w_ref, 1, scr).astype(BF16)


def _compress_prompt(rk, rv, pos, w):
    r = rk.shape[0]
    return pl.pallas_call(
        _compress_kernel,
        in_specs=[_full((r, 1024), False), _full((r, 1024), False), _full((2, 2, 1, 1024), False),
                  _full((2, 2, 1024, HD), False)],
        out_specs=[_full((r, HD), False)] * 2,
        grid=(1,),
        out_shape=[jax.ShapeDtypeStruct((r, HD), BF16)] * 2,
        scratch_shapes=[pltpu.VMEM((r + 8, HD), F32)],
        compiler_params=_cparams(("arbitrary",)),
        name="nsa_compress",
    )(rk, rv, pos, w)


def _select_topk(val, n_sel):
    ns = val.shape[1]
    j = lax.broadcasted_iota(jnp.int32, val.shape, 1).astype(F32)
    picks = []
    for _ in range(n_sel):
        mx = jnp.max(val, axis=1, keepdims=True)
        first = jnp.min(jnp.where(val == mx, j, float(ns)), axis=1, keepdims=True)
        picks.append(first)
        val = jnp.where(j == first, -jnp.inf, val)
    return picks


def _cmp_attn_kernel(q_ref, kc_ref, vc_ref, ovl_ref, tab_ref, misc_ref, oc_ref, val_ref, *, n_rows):
    i = pl.program_id(0)
    q = q_ref[...].reshape(NH * TQ, HD)
    w0 = 8 * i - 8
    c0 = jnp.clip((w0 // 128) * 128, 0, n_rows - 256)
    c0 = pl.multiple_of(c0, 128)
    s = _dot_nt(q, kc_ref[...])
    n_idx = lax.broadcasted_iota(jnp.int32, s.shape, 1)
    s_far = s + jnp.where(n_idx < c0, 0.0, NEG)
    s_win = _dot_nt(q, kc_ref[pl.ds(c0, 256), :]) + tab_ref[0]
    m = jnp.maximum(jnp.max(s_far, axis=1, keepdims=True), jnp.max(s_win, axis=1, keepdims=True))
    p_far = jnp.exp(s_far - m)
    p_win = jnp.exp(s_win - m)
    l = jnp.sum(p_far, axis=1, keepdims=True) + jnp.sum(p_win, axis=1, keepdims=True)
    inv = jnp.where(m > 0.5 * NEG, 1.0 / l, 0.0)
    pb_far = (p_far * inv).astype(BF16)
    pb_win = (p_win * inv).astype(BF16)
    o = _dot(pb_far, vc_ref[...]) + _dot(pb_win, vc_ref[pl.ds(c0, 256), :])
    impf = _dot(pb_far, ovl_ref[...]) + _dot(pb_win, ovl_ref[pl.ds(c0, 256), :])
    imp = impf[0:TQ] + impf[TQ:2 * TQ] + impf[2 * TQ:3 * TQ] + impf[3 * TQ:4 * TQ]
    gates = misc_ref[...]
    for hh in range(NH):
        oc_ref[hh] = o[hh * TQ:(hh + 1) * TQ] * gates[:, hh:hh + 1]
    j = lax.broadcasted_iota(jnp.int32, imp.shape, 1)
    qpos = i * TQ + lax.broadcasted_iota(jnp.int32, imp.shape, 0)
    qblk = qpos // SLC_BLOCK
    forced = (j == 0) | (j == qblk) | (j == qblk - 1)
    val_ref[...] = jnp.where(forced, BIG, jnp.where(j * SLC_BLOCK <= qpos, imp, -BIG))


def _cmp_attn(qh, kc, vc, ovl, tab, misc):
    l = qh.shape[1]
    r = kc.shape[0]
    ns = ovl.shape[1]

    def variant(i):
        w0 = 8 * i - 8
        c0 = jnp.clip((w0 // 128) * 128, 0, r - 256)
        return ((w0 - c0 + 8) // 8, 0, 0)

    heads = pl.BlockSpec((NH, TQ, HD), lambda i: (0, i, 0))
    return pl.pallas_call(
        functools.partial(_cmp_attn_kernel, n_rows=r),
        grid=(l // TQ,),
        in_specs=[heads, _full((r, HD)), _full((r, HD)), _full((r, ns)),
                  pl.BlockSpec((1, NH * TQ, 256), variant),
                  pl.BlockSpec((TQ, 128), lambda i: (i, 0))],
        out_specs=[heads, pl.BlockSpec((TQ, ns), lambda i: (i, 0))],
        out_shape=[jax.ShapeDtypeStruct((NH, l, HD), F32), jax.ShapeDtypeStruct((l, ns), F32)],
        compiler_params=_cparams(("arbitrary",)),
        name="nsa_cmp_attn",
    )(qh, kc, vc, ovl, tab, misc)


def _topk_mask_kernel(val_ref, selb_ref, *, n_sel):
    val = val_ref[...]
    jf = lax.broadcasted_iota(jnp.int32, val.shape, 1).astype(F32)
    selb = jnp.full(val.shape, NEG, F32)
    for first in _select_topk(val, n_sel):
        selb = jnp.where(jf == first, 0.0, selb)
    selb_ref[...] = selb.astype(BF16)


def _topk_mask(val):
    l, ns = val.shape
    tr = min(TOPK_ROWS, l)
    return pl.pallas_call(
        functools.partial(_topk_mask_kernel, n_sel=min(N_SELECT, ns)),
        grid=(l // tr,),
        in_specs=[pl.BlockSpec((tr, ns), lambda i: (i, 0))],
        out_specs=pl.BlockSpec((tr, ns), lambda i: (i, 0)),
        out_shape=jax.ShapeDtypeStruct((l, ns), BF16),
        compiler_params=_cparams(("arbitrary",)),
        name="nsa_topk",
    )(val)


def _topk_index_kernel(val_ref, idx_ref, *, n_sel):
    lane = lax.broadcasted_iota(jnp.int32, idx_ref.shape, 1)
    idx = jnp.zeros(idx_ref.shape, F32)
    for t, first in enumerate(_select_topk(val_ref[...], n_sel)):
        idx = jnp.where(lane == t, first, idx)
    idx_ref[...] = idx.astype(jnp.int32)


def _topk_index(val, n_sel):
    return pl.pallas_call(
        functools.partial(_topk_index_kernel, n_sel=n_sel),
        out_shape=jax.ShapeDtypeStruct((val.shape[0], 128), jnp.int32),
        name="nsa_topk_index",
    )(val)


def _flash_step(carry, s, v):
    m, l, acc = carry
    m_new = jnp.maximum(m, jnp.max(s, axis=1, keepdims=True))
    alpha = jnp.exp(m - m_new)
    p = jnp.exp(s - m_new)
    l = alpha * l + jnp.sum(p, axis=1, keepdims=True)
    acc = alpha * acc + _dot(p.astype(BF16), v)
    return m_new, l, acc


def _flash_init(rows, width):
    return (jnp.full((rows, 1), NEG, F32), jnp.zeros((rows, 1), F32), jnp.zeros((rows, width), F32))


def _nsa_attn_kernel(q_ref, selb_ref, ksel_ref, kwin_ref, vcat_ref, tnear_ref, twin_ref, misc_ref, oc_ref,
                     ya_ref, qx_ref, sa_ref, sb_ref, *, n_super):
    i = pl.program_id(0)
    q0 = i * TQ
    for jj in range(n_super):
        sb = selb_ref[:, jj * 64:(jj + 1) * 64]
        for hh in range(NH):
            qx_ref[jj, hh * TQ:(hh + 1) * TQ, :] = jnp.concatenate([q_ref[hh], sb], axis=1)

    ks = jnp.maximum(q0 - TQ, 0)
    n_full = ks // TK

    def scores(j):
        k0 = pl.multiple_of(j * TK, TK)
        return _dot_nt(qx_ref[j // (SUPER // TK)], ksel_ref[pl.ds(k0, TK), :])

    def values(j):
        return vcat_ref[pl.ds(pl.multiple_of(WINDOW + j * TK, TK), TK), :]

    col = n_full * TK + lax.broadcasted_iota(jnp.int32, (1, TK), 1)
    hide = jnp.where(col < ks, 0.0, NEG)
    sa_ref[...] = scores(0)

    def pair_body(jj, carry):
        j = 2 * jj
        sb_ref[...] = scores(j + 1)
        carry = _flash_step(carry, sa_ref[...], values(j))
        sa_ref[...] = scores(j + 2)
        return _flash_step(carry, sb_ref[...], values(j + 1))

    carry = lax.fori_loop(0, n_full // 2, pair_body, _flash_init(NH * TQ, 128))

    def odd_tail(carry):
        sb_ref[...] = scores(n_full)
        carry = _flash_step(carry, sa_ref[...], values(n_full - 1))
        return _flash_step(carry, sb_ref[...] + hide, values(n_full))

    def even_tail(carry):
        return _flash_step(carry, sa_ref[...] + hide, values(n_full))

    carry = lax.cond(n_full % 2 == 1, odd_tail, even_tail, carry)
    parts = []
    for half in range(2):
        kh = pl.multiple_of(ks + half * TQ, TQ)
        parts.append(_dot_nt(qx_ref[kh // SUPER], ksel_ref[pl.ds(kh, TQ), :]))
    s = jnp.concatenate(parts, axis=1) + tnear_ref[jnp.minimum(i, 1)]
    _, l_s, acc_s = _flash_step(carry, s, vcat_ref[pl.ds(pl.multiple_of(WINDOW + ks, TQ), 2 * TQ), :])

    w0 = pl.multiple_of(q0, TQ)
    wpos = q0 - WINDOW + lax.broadcasted_iota(jnp.int32, (1, WINDOW + TQ), 1)
    s_w = _dot_nt(qx_ref[0], kwin_ref[pl.ds(w0, WINDOW + TQ), :]) + twin_ref[...] + jnp.where(wpos < 0, NEG, 0.0)
    p_w = jnp.exp(s_w - jnp.max(s_w, axis=1, keepdims=True))
    l_w = jnp.sum(p_w, axis=1, keepdims=True)
    acc_w = _dot(p_w.astype(BF16), vcat_ref[pl.ds(w0, WINDOW + TQ), :])

    o_s = acc_s[:, 0:HD] / l_s
    o_w = acc_w[:, HD:2 * HD] / l_w
    gates = misc_ref[...]
    for hh in range(NH):
        rows = slice(hh * TQ, (hh + 1) * TQ)
        y = oc_ref[hh] + gates[:, NH + hh:NH + hh + 1] * o_s[rows] + gates[:, 2 * NH + hh:2 * NH + hh + 1] * o_w[rows]
        ya_ref[hh] = y.astype(BF16)


def _nsa_attn(qh, selb, ksel, kwin, vcat, tnear, twin, misc, oc):
    l = qh.shape[1]
    ns = selb.shape[1]
    n_super = ns // 64
    heads = pl.BlockSpec((NH, TQ, HD), lambda i: (0, i, 0))
    return pl.pallas_call(
        functools.partial(_nsa_attn_kernel, n_super=n_super),
        grid=(l // TQ,),
        in_specs=[heads, pl.BlockSpec((TQ, ns), lambda i: (i, 0)),
                  _full((l, 128)), _full((l + WINDOW, 128)), _full((l + WINDOW, 128)),
                  _full((2, NH * TQ, 2 * TQ)), _full((NH * TQ, WINDOW + TQ)),
                  pl.BlockSpec((TQ, 128), lambda i: (i, 0)), heads],
        out_specs=heads,
        out_shape=jax.ShapeDtypeStruct((NH, l, HD), BF16),
        scratch_shapes=[pltpu.VMEM((n_super, NH * TQ, 128), BF16), pltpu.VMEM((NH * TQ, TK), F32),
                        pltpu.VMEM((NH * TQ, TK), F32)],
        compiler_params=_cparams(("arbitrary",)),
        name="nsa_attn",
    )(qh, selb, ksel, kwin, vcat, tnear, twin, misc, oc)


def _fox_attn_kernel(bnd_ref, q_ref, k_ref, v_ref, o_ref):
    pair = pl.program_id(0)
    i = pl.program_id(1)
    q0 = i * TF
    n_full = q0 // TKF
    qs = [q_ref[0], q_ref[1]]
    reach = []
    for hh in range(2):
        qf = qs[hh][:, 0:HD].astype(F32)
        reach.append(jnp.sqrt(jnp.sum(qf * qf, axis=1, keepdims=True)) * bnd_ref[2 * pair + hh, 0])

    def tile(j):
        k0 = pl.multiple_of(j * TKF, TKF)
        return [_dot_nt(qs[hh], k_ref[hh, pl.ds(k0, TKF), :]) for hh in range(2)], v_ref[pl.ds(k0, TKF), :]

    ss, v = tile(n_full)
    key = n_full * TKF + lax.broadcasted_iota(jnp.int32, ss[0].shape, 1)
    row = q0 + lax.broadcasted_iota(jnp.int32, ss[0].shape, 0)
    carry = tuple(_flash_step(_flash_init(TF, 128), jnp.where(key <= row, ss[hh], NEG), v) for hh in range(2))

    def reachable(j, carry):
        j = jnp.maximum(j, 0)
        gap = [jnp.max(reach[hh] + bnd_ref[2 * pair + hh, 1 + j] - carry[hh][0]) for hh in range(2)]
        return jnp.maximum(gap[0], gap[1]) > -FOX_CUT

    def body(state):
        j, _, carry = state
        ss, v = tile(j)
        carry = tuple(_flash_step(carry[hh], ss[hh], v) for hh in range(2))
        return j - 1, jnp.logical_and(j >= 1, reachable(j - 1, carry)), carry

    start = (n_full - 1, jnp.logical_and(n_full >= 1, reachable(n_full - 1, carry)), carry)
    _, _, carry = lax.while_loop(lambda state: state[1], body, start)
    outs = [carry[hh][2] / carry[hh][1] for hh in range(2)]
    lane = lax.broadcasted_iota(jnp.int32, outs[0].shape, 1)
    o_ref[...] = jnp.where(lane < HD, outs[0], outs[1]).astype(BF16)


def _fox_attn(bounds, fqx, fkx, fv):
    l = fqx.shape[1]
    return pl.pallas_call(
        _fox_attn_kernel,
        grid=(NH // 2, l // TF),
        in_specs=[pl.BlockSpec(memory_space=pltpu.SMEM),
                  pl.BlockSpec((2, TF, 128), lambda p, i: (p, i, 0)),
                  pl.BlockSpec((2, l, 128), lambda p, i: (p, 0, 0), pipeline_mode=pl.Buffered(1)),
                  pl.BlockSpec((l, 128), lambda p, i: (0, p), pipeline_mode=pl.Buffered(1))],
        out_specs=pl.BlockSpec((TF, 128), lambda p, i: (i, p)),
        out_shape=jax.ShapeDtypeStruct((l, MIX), BF16),
        compiler_params=_cparams(("arbitrary", "arbitrary")),
        name="fox_attn",
    )(bounds, fqx, fkx, fv)


def _layernorm_silu(y, ln_ref):
    mu = jnp.mean(y, axis=-1, keepdims=True)
    d = y - mu
    var = jnp.mean(d * d, axis=-1, keepdims=True)
    z = d * lax.rsqrt(var + EPS) * ln_ref[0] + ln_ref[1]
    return z * jax.nn.sigmoid(z)


def _pool_select(sums, counts, u):
    lane = lax.broadcasted_iota(jnp.int32, u.shape, 1)
    group = MIX // len(POOL_WINDOWS)
    out = sums[-1] / counts[-1]
    for g in range(len(POOL_WINDOWS) - 2, -1, -1):
        out = jnp.where(lane < (g + 1) * group, sums[g] / counts[g], out)
    return out - u


def _convpool_kernel(u_ref, uh_ref, up_ref, c_ref, ch_ref, cp_ref, cw_ref, cb_ref, ln_ref, pw_ref, ps_ref,
                     yb_ref, yc_ref, ext_ref, *, pos0):
    i = pl.program_id(0)
    tm = u_ref.shape[0]
    ext_ref[pl.ds(0, HALO), :] = jnp.where(i == 0, up_ref[...], uh_ref[...])
    ext_ref[pl.ds(HALO, tm), :] = u_ref[...]
    acc = jnp.zeros((tm, MIX), F32) + cb_ref[...]
    for w in range(CONV_WIDTH):
        acc = acc + ext_ref[pl.ds(HALO - CONV_HIST + w, tm), :] * cw_ref[pl.ds(w, 1), :]
    yb_ref[...] = _layernorm_silu(acc, ln_ref).astype(BF16)
    c = c_ref[...]
    ext_ref[pl.ds(0, HALO), :] = jnp.where(i == 0, cp_ref[...], ch_ref[...])
    ext_ref[pl.ds(HALO, tm), :] = c
    pos = pos0 + i * tm + lax.broadcasted_iota(jnp.int32, (tm, 1), 0)
    run = c
    sums, counts = [], []
    for k in range(1, max(POOL_WINDOWS)):
        run = run + ext_ref[pl.ds(HALO - k, tm), :]
        if k + 1 in POOL_WINDOWS:
            sums.append(run)
            counts.append(jnp.minimum(k + 1, pos + 1).astype(F32))
    pooled = _pool_select(sums, counts, c).astype(BF16)
    yc_ref[...] = (_dot(pooled, pw_ref[...]) * ps_ref[...]).astype(BF16)


def _convpool(u, u_past, c, c_past, cw, cb, ln, pw_bd, ps, pos0):
    l = u.shape[0]
    tm = min(TM, l)
    nh = tm // HALO
    row = pl.BlockSpec((tm, MIX), lambda i: (i, 0))
    halo = pl.BlockSpec((HALO, MIX), lambda i: (jnp.maximum(i * nh - 1, 0), 0))
    return pl.pallas_call(
        functools.partial(_convpool_kernel, pos0=pos0),
        grid=(l // tm,),
        in_specs=[row, halo, _full((HALO, MIX)), row, halo, _full((HALO, MIX)),
                  _full((HALO, MIX)), _full((1, MIX)), _full((2, 1, MIX)), _full((MIX, MIX)), _full((1, MIX))],
        out_specs=[row, row],
        out_shape=[jax.ShapeDtypeStruct((l, MIX), BF16)] * 2,
        scratch_shapes=[pltpu.VMEM((tm + HALO, MIX), F32)],
        compiler_params=_cparams(("arbitrary",)),
        name="convpool",
    )(u, u, u_past, c, c, c_past, cw, cb, ln, pw_bd, ps)


def _convpool_step_kernel(u_ref, up_ref, c_ref, cp_ref, cw_ref, cb_ref, ln_ref, pw_ref, ps_ref, yb_ref, yc_ref,
                          *, pos0):
    u = u_ref[...]
    acc = cb_ref[...] + u * cw_ref[pl.ds(CONV_WIDTH - 1, 1), :]
    for w in range(CONV_HIST):
        acc = acc + up_ref[w] * cw_ref[pl.ds(w, 1), :]
    yb_ref[...] = _layernorm_silu(acc, ln_ref).astype(BF16)
    c = c_ref[...]
    run = c
    sums, counts = [], []
    for k in range(1, max(POOL_WINDOWS)):
        run = run + cp_ref[POOL_HIST - k]
        if k + 1 in POOL_WINDOWS:
            sums.append(run)
            counts.append(float(min(k + 1, pos0 + 1)))
    pooled = _pool_select(sums, counts, c).astype(BF16)
    yc_ref[...] = (_dot(pooled, pw_ref[...]) * ps_ref[...]).astype(BF16)


def _convpool_step(u, u_past, c, c_past, cw, cb, ln, pw_bd, ps, pos0):
    b = u.shape[0]
    return pl.pallas_call(
        functools.partial(_convpool_step_kernel, pos0=pos0),
        out_shape=[jax.ShapeDtypeStruct((b, MIX), BF16)] * 2,
        name="convpool_step",
    )(u, u_past, c, c_past, cw, cb, ln, pw_bd, ps)


def _nsa_paged_cmp_kernel(pt_ref, *refs, n_pages, past):
    pages = refs[:PG]
    (q_ref, pos_ref, w_ref, ovl_ref, bias_ref, misc_ref, oc_ref, val_ref, r_ref, scr, tok_ref) = refs[PG:]
    p = pl.program_id(1)
    for g in range(PG):
        r0 = pl.multiple_of((p * PG + g) * 8, 8)
        tok_ref[...] = pages[g][0, 0].reshape(2 * HD, PAGE).T
        for t in range(CMP_STRIDE):
            r_ref[pl.ds(r0, 8), t * 2 * HD:(t + 1) * 2 * HD] = tok_ref[pl.ds(t, 8, stride=CMP_STRIDE), :]

    @pl.when(p == n_pages // PG - 1)
    def _():
        r = r_ref[...]
        n = r.shape[0]
        lo = _dot((r + pos_ref[0]).astype(BF16), w_ref[0])
        hi = _dot((r + pos_ref[1]).astype(BF16), w_ref[1])
        scr[pl.ds(0, n), :] = hi
        scr[pl.ds(n, 8), :] = jnp.zeros((8, 2 * HD), F32)
        kvc = (lo + scr[pl.ds(1, n), :]).astype(BF16)
        q = q_ref[0]
        s = _dot_nt(q, kvc) + bias_ref[...]
        m = jnp.max(s, axis=1, keepdims=True)
        e = jnp.exp(s - m)
        pr = (e / jnp.sum(e, axis=1, keepdims=True)).astype(BF16)
        o = _dot(pr, kvc)[:, HD:2 * HD]
        gates = misc_ref[0]
        oc_ref[0] = o * gates[:, 0:1]
        impf = _dot(pr, ovl_ref[...])
        imp = jnp.sum(impf[0:NH], axis=0, keepdims=True)
        ns = imp.shape[1]
        j = lax.broadcasted_iota(jnp.int32, imp.shape, 1)
        qblk = past // SLC_BLOCK
        forced = (j == 0) | (j == qblk) | (j == qblk - 1)
        val = jnp.where(forced, BIG, jnp.where(j * SLC_BLOCK <= past, imp, -BIG))
        val_ref[0] = jnp.where(j <= qblk, val, -jnp.inf)


def _nsa_paged_cmp(page_table, cache, layer, qx8, pos, w, ovl, bias, misc8, past):
    bsz, n_pages = page_table.shape
    n_rows = past // CMP_STRIDE
    ns = ovl.shape[1]
    wide = CMP_STRIDE * 2 * HD

    def page_spec(g):
        return pl.BlockSpec((1, 1, 2, HD, PAGE), lambda b, p, pt, _g=g: (layer, pt[b, p * PG + _g], 0, 0, 0))

    def const(shape):
        nd = len(shape)
        return pl.BlockSpec(shape, lambda b, p, pt, _nd=nd: (0,) * _nd)

    def per_b(shape):
        return pl.BlockSpec(shape, lambda b, p, pt: (b, 0, 0))

    grid_spec = pltpu.PrefetchScalarGridSpec(
        num_scalar_prefetch=1,
        grid=(bsz, n_pages // PG),
        in_specs=[page_spec(g) for g in range(PG)] + [
            per_b((1, 8, 2 * HD)), const((2, 1, wide)), const((2, wide, 2 * HD)), const((n_rows, ns)),
            const((8, n_rows)), per_b((1, 8, 128))],
        out_specs=[per_b((1, 8, HD)), per_b((1, 1, ns))],
        scratch_shapes=[pltpu.VMEM((n_rows, wide), F32), pltpu.VMEM((n_rows + 8, 2 * HD), F32),
                        pltpu.VMEM((PAGE, 2 * HD), F32)],
    )
    return pl.pallas_call(
        functools.partial(_nsa_paged_cmp_kernel, n_pages=n_pages, past=past),
        grid_spec=grid_spec,
        out_shape=[jax.ShapeDtypeStruct((bsz, 8, HD), F32), jax.ShapeDtypeStruct((bsz, 1, ns), F32)],
        compiler_params=_cparams(("arbitrary", "arbitrary")),
        name="nsa_paged_cmp",
    )(page_table, *([cache] * PG), qx8, pos, w, ovl, bias, misc8)


def _nsa_paged_attn_kernel(idx_ref, *refs, past):
    blocks = refs[:N_SELECT]
    (q_ref, new_ref, win_ref, neww_ref, relt_ref, wbias_ref, misc_ref, oc_ref, ya_ref) = refs[N_SELECT:]
    b = pl.program_id(0)
    q = q_ref[0]
    qf = q.astype(F32)

    def own_logit(k_row):
        return jnp.sum(qf * k_row.astype(BF16).astype(F32), axis=1, keepdims=True)

    def attend(s, s_own, values, v_own):
        m = jnp.maximum(jnp.max(s, axis=1, keepdims=True), s_own)
        e = jnp.exp(s - m)
        e_own = jnp.exp(s_own - m)
        inv = 1.0 / (jnp.sum(e, axis=1, keepdims=True) + e_own)
        pr = (e * inv).astype(BF16)
        o = (e_own * inv) * v_own.astype(BF16).astype(F32)
        for c, v in enumerate(values):
            o = o + _dot_nt(pr[:, c * PAGE:(c + 1) * PAGE], v)
        return o

    lane = lax.broadcasted_iota(jnp.int32, (1, PAGE), 1)
    s_parts, pos_parts, hide_parts, v_parts = [], [], [], []
    for t in range(N_SELECT):
        j = idx_ref[b, N_SELECT + t]
        s_parts.append(_dot(q, blocks[t][0, 0, 0].astype(BF16)))
        v_parts.append(blocks[t][0, 0, 1].astype(BF16))
        pos_parts.append(jnp.minimum(j // 2, past // PAGE - 1) * PAGE + lane)
        half = jnp.where(j < past // SLC_BLOCK, j % 2, 2)
        hide_parts.append(jnp.where(lane // SLC_BLOCK == half, 0.0, NEG))
    pos = jnp.concatenate(pos_parts, axis=1)
    s = jnp.concatenate(s_parts, axis=1) + _t5_bias(relt_ref[...], past - pos) + jnp.concatenate(hide_parts, axis=1)
    new = new_ref[0]
    bias0 = _t5_bias(relt_ref[...], jnp.zeros((1, PAGE), jnp.int32))[:, 0:1]
    o_s = attend(s, own_logit(new[:, 2 * HD:3 * HD]) + bias0, v_parts, new[:, 3 * HD:4 * HD])
    n_win = win_ref.shape[4]
    s_w = _dot(q, win_ref[0, 0, 0].astype(BF16)) + wbias_ref[...]
    vw = [win_ref[0, 0, 1, :, c * PAGE:(c + 1) * PAGE].astype(BF16) for c in range(n_win // PAGE)]
    neww = neww_ref[0]
    o_w = attend(s_w, own_logit(neww[:, 0:HD]) + bias0, vw, neww[:, HD:2 * HD])
    gates = misc_ref[0]
    ya_ref[0] = (oc_ref[0] + gates[:, 1:2] * o_s + gates[:, 2:3] * o_w).astype(BF16)


def _nsa_paged_attn(sel, cache, layer, q8, nsa_new, win_state, win_new, relt8, wbias, misc8, oc, past):
    bsz = sel.shape[0]
    n_win = win_state.shape[4]

    def blk_spec(t):
        return pl.BlockSpec((1, 1, 2, HD, PAGE), lambda b, ix, _t=t: (layer, ix[b, _t], 1, 0, 0))

    def const(shape):
        nd = len(shape)
        return pl.BlockSpec(shape, lambda b, ix, _nd=nd: (0,) * _nd)

    def per_b(shape):
        return pl.BlockSpec(shape, lambda b, ix: (b, 0, 0))

    win_spec = pl.BlockSpec((1, 1, 2, HD, n_win), lambda b, ix: (layer, b, 0, 0, 0))
    grid_spec = pltpu.PrefetchScalarGridSpec(
        num_scalar_prefetch=1,
        grid=(bsz,),
        in_specs=[blk_spec(t) for t in range(N_SELECT)] + [
            per_b((1, 8, HD)), per_b((1, 1, 256)), win_spec, per_b((1, 1, 128)),
            const((8, N_BUCKETS)), const((8, n_win)), per_b((1, 8, 128)), per_b((1, 8, HD))],
        out_specs=per_b((1, 8, HD)),
    )
    return pl.pallas_call(
        functools.partial(_nsa_paged_attn_kernel, past=past),
        grid_spec=grid_spec,
        out_shape=jax.ShapeDtypeStruct((bsz, 8, HD), BF16),
        compiler_params=_cparams(("arbitrary",)),
        name="nsa_paged_attn",
    )(sel, *([cache] * N_SELECT), q8, nsa_new, win_state, win_new, relt8, wbias, misc8, oc)


def _fox_paged_kernel(pt_ref, kv_hbm, lf_hbm, qcol_ref, new_ref, lfnew_ref, o_ref, kv_buf, lf_buf, sem, *,
                      layer, n_pages):
    b = pl.program_id(0)
    n_seq = pl.num_programs(0)
    n_chunk = n_pages // PGF

    def copies(seq, chunk, slot):
        out = []
        for g in range(PGF):
            page = pt_ref[seq, (n_chunk - 1 - chunk) * PGF + g]
            out.append(pltpu.make_async_copy(kv_hbm.at[layer, page], kv_buf.at[slot, g], sem.at[slot]))
            out.append(pltpu.make_async_copy(lf_hbm.at[layer, page], lf_buf.at[slot, g], sem.at[slot]))
        return out

    @pl.when(b == 0)
    def _():
        for cp in copies(0, 0, 0):
            cp.start()

    q_col = qcol_ref[0]
    q_wide = jnp.broadcast_to(q_col, (MIX, PAGE))
    new = new_ref[0]
    row8 = lax.broadcasted_iota(jnp.int32, (8, 1), 0)

    def per_head(x):
        out = jnp.zeros((8, x.shape[1]), F32)
        for hh in range(NH):
            out = jnp.where(row8 == hh, jnp.sum(x[hh * HD:(hh + 1) * HD], axis=0, keepdims=True), out)
        return out

    def per_feature(x, width):
        return jnp.concatenate([jnp.broadcast_to(x[hh:hh + 1], (HD, width)) for hh in range(NH)], axis=0)

    m0 = per_head(q_col * new[0:MIX])
    acc0 = jnp.where(lax.broadcasted_iota(jnp.int32, (MIX, PAGE), 1) == 0, new[MIX:2 * MIX], 0.0)
    later = jnp.where(lax.broadcasted_iota(jnp.int32, (PAGE, PAGE), 0) > lax.broadcasted_iota(jnp.int32, (PAGE, PAGE), 1),
                      1.0, 0.0).astype(BF16)

    def body(c, state):
        m_old, l_old, acc, carry = state
        slot = (b * n_chunk + c) % 2

        @pl.when(c + 1 < n_chunk)
        def _():
            for cp in copies(b, c + 1, 1 - slot):
                cp.start()

        @pl.when(jnp.logical_and(c + 1 == n_chunk, b + 1 < n_seq))
        def _():
            for cp in copies(b + 1, 0, 1 - slot):
                cp.start()

        for cp in copies(b, c, slot):
            cp.wait()
        totals = [jnp.sum(lf_buf[slot, g], axis=1, keepdims=True) for g in range(PGF)]
        newer = [None] * PGF
        for g in range(PGF - 1, -1, -1):
            newer[g] = carry
            carry = carry + totals[g]
        within = _dot01(lf_buf[slot].reshape(PGF * 8, PAGE), later)
        s_parts = []
        for g in range(PGF):
            s_parts.append(per_head(kv_buf[slot, g, 0] * q_wide) + newer[g] + within[g * 8:(g + 1) * 8])
        s = jnp.concatenate(s_parts, axis=1)
        m_new = jnp.maximum(m_old, jnp.max(s, axis=1, keepdims=True))
        alpha = jnp.exp(m_old - m_new)
        pr = jnp.exp(s - m_new)
        l_new = alpha * l_old + jnp.sum(pr, axis=1, keepdims=True)
        acc = acc * per_feature(alpha, 1)
        for g in range(PGF):
            acc = acc + kv_buf[slot, g, 1] * per_feature(pr[:, g * PAGE:(g + 1) * PAGE], PAGE)
        return m_new, l_new, acc, carry

    _, l, acc, _ = lax.fori_loop(0, n_chunk, body, (m0, jnp.ones((8, 1), F32), acc0, lfnew_ref[0]))
    o_ref[0] = jnp.sum(acc, axis=1, keepdims=True) / per_feature(l, 1)


def _fox_paged(page_table, cache_kv, cache_lf, layer, q_col, kv_new, lf_new):
    bsz, n_pages = page_table.shape
    assert n_pages % PGF == 0

    def per_b(shape):
        return pl.BlockSpec(shape, lambda b, pt: (b, 0, 0))

    grid_spec = pltpu.PrefetchScalarGridSpec(
        num_scalar_prefetch=1,
        grid=(bsz,),
        in_specs=[pl.BlockSpec(memory_space=pl.ANY), pl.BlockSpec(memory_space=pl.ANY),
                  per_b((1, MIX, 1)), per_b((1, 2 * MIX, 1)), per_b((1, 8, 1))],
        out_specs=per_b((1, MIX, 1)),
        scratch_shapes=[pltpu.VMEM((2, PGF, 2, MIX, PAGE), F32), pltpu.VMEM((2, PGF, 8, PAGE), F32),
                        pltpu.SemaphoreType.DMA((2,))],
    )
    return pl.pallas_call(
        functools.partial(_fox_paged_kernel, layer=layer, n_pages=n_pages),
        grid_spec=grid_spec,
        out_shape=jax.ShapeDtypeStruct((bsz, MIX, 1), F32),
        compiler_params=_cparams(("arbitrary",)),
        name="fox_paged",
    )(page_table, cache_kv, cache_lf, q_col, kv_new, lf_new)


def _layer_params(l, norm_g, ffn_gate, ffn_up, ffn_down, w_in, fox_f_bias, nsa_cmp_pos, nsa_cmp_w,
                  conv_w, conv_b, conv_ln, pool_w, pool_scale, w_branch, w_out):
    wi = w_in[l]
    o_aq, o_akv, o_ag, o_b, o_c, o_d, o_f, o_m = np.cumsum([0, 256, 384, 12, 512, 256, 768, 4])
    misc = jnp.concatenate([wi[:, o_ag:o_ag + 12], wi[:, o_f:o_f + 4], jnp.zeros((D_MODEL, 112), F32)], axis=1)
    w_small = jnp.concatenate([wi[:, o_aq:o_aq + 256], wi[:, o_akv:o_akv + 384], misc, wi[:, o_b:o_b + 512],
                               wi[:, o_c:o_c + 256], wi[:, o_d:o_d + 768]], axis=1).astype(BF16)
    fb_row = jnp.zeros((1, 128), F32).at[0, 12:16].set(fox_f_bias[l])
    cw = nsa_cmp_w[l].reshape(2, 2, CMP_STRIDE * HD, HD).astype(BF16)
    cpos = nsa_cmp_pos[l].reshape(2, 2, 1, CMP_STRIDE * HD)
    w4 = nsa_cmp_w[l].reshape(2, 2, CMP_STRIDE, HD, HD)
    none = jnp.zeros_like(w4[0])
    w_pair = jnp.concatenate([jnp.concatenate([w4[0], none], axis=-1), jnp.concatenate([none, w4[1]], axis=-1)], axis=2)
    p4 = nsa_cmp_pos[l].reshape(2, 2, CMP_STRIDE, HD)
    pos_pair = jnp.concatenate([p4[0], p4[1]], axis=-1).reshape(2, 1, CMP_STRIDE * 2 * HD)
    group = MIX // len(POOL_WINDOWS)
    pw_bd = jnp.zeros((MIX, MIX), F32)
    for g in range(len(POOL_WINDOWS)):
        pw_bd = pw_bd.at[g * group:(g + 1) * group, g * group:(g + 1) * group].set(pool_w[l, g])
    cw_conv = jnp.concatenate([conv_w[l], jnp.zeros((HALO - CONV_WIDTH, MIX), F32)], axis=0)
    return dict(
        g=[norm_g[l, k].reshape(1, D_MODEL) for k in range(3)],
        ffn=[(ffn_gate[l, k].astype(BF16), ffn_up[l, k].astype(BF16), ffn_down[l, k].astype(BF16)) for k in range(2)],
        w_small=w_small, w_merge=wi[:, o_m:].astype(BF16), fb_row=fb_row, cmp_w=cw, cmp_pos=cpos,
        cmp_w_pair=w_pair.reshape(2, CMP_STRIDE * 2 * HD, 2 * HD).astype(BF16), cmp_pos_pair=pos_pair,
        conv_w=cw_conv, conv_b=conv_b[l].reshape(1, MIX), conv_ln=conv_ln[l].reshape(2, 1, MIX),
        pool_w=pw_bd.astype(BF16), pool_scale=pool_scale[l].reshape(1, MIX),
        w_branch=w_branch[l].astype(BF16), w_out=w_out[l].astype(BF16))


def _overlap(n_rows, n_blocks, n_pad):
    cs = np.arange(n_rows)[:, None] * CMP_STRIDE
    j0 = np.arange(n_pad)[None, :] * SLC_BLOCK
    ov = (cs < j0 + SLC_BLOCK) & (cs + CMP_LEN > j0) & (np.arange(n_pad)[None, :] < n_blocks)
    return jnp.asarray(ov.astype(np.float32), dtype=BF16)


def _lookup(bt, dist, valid):
    d = np.clip(dist, 0, NDIST - 1)
    return jnp.where(jnp.asarray(valid)[None], bt[:, d], NEG)


def _toeplitz_kernel(rel_ref, o_ref, *, base, vstep, stride, hi):
    width = o_ref.shape[2]
    r = lax.broadcasted_iota(jnp.int32, (TQ, width), 0)
    m = lax.broadcasted_iota(jnp.int32, (TQ, width), 1)
    dist = base + vstep * pl.program_id(0) + r - stride * m
    bucket = _t5_bucket(dist)
    hide = jnp.where(dist < 0, NEG, jnp.where(dist > hi, NEG, 0.0))
    for hh in range(NH):
        far = rel_ref[N_BUCKETS - 1, hh]
        val = jnp.zeros((TQ, width), F32)
        for b in range(N_BUCKETS - 1):
            val = jnp.where(bucket == b, rel_ref[b, hh] - far, val)
        o_ref[0, hh * TQ:(hh + 1) * TQ, :] = val + hide


def _toeplitz(rel_bias, n_var, width, base, vstep, stride, hi):
    return pl.pallas_call(
        functools.partial(_toeplitz_kernel, base=base, vstep=vstep, stride=stride, hi=hi),
        grid=(n_var,),
        in_specs=[pl.BlockSpec(memory_space=pltpu.SMEM)],
        out_specs=pl.BlockSpec((1, NH * TQ, width), lambda v: (v, 0, 0)),
        out_shape=jax.ShapeDtypeStruct((n_var, NH * TQ, width), F32),
        compiler_params=_cparams(("arbitrary",)),
        name="t5_tables",
    )(rel_bias)


def _prompt_tables(rel_bias):
    big = 1 << 30
    tab_c = _toeplitz(rel_bias, 33, 256, -(CMP_LEN - 1), TQ, CMP_STRIDE, big)
    tnear = _toeplitz(rel_bias, 2, 2 * TQ, 0, TQ, 1, big)
    twin = _toeplitz(rel_bias, 1, WINDOW + TQ, WINDOW, 0, 1, WINDOW)[0]
    return tab_c, tnear, twin


def _prompt_layer(x, lp, tables, fg, final):
    l = x.shape[0]
    tab_c, tnear, twin = tables
    x = _ffn(x, lp["g"][0], *lp["ffn"][0], fg)
    (qn, nsa_kv, win_kv, misc, u, cin, fq, fox_kv, ksel, kwin, vcat, fk, fv, stat) = _inproj(x, lp["g"][1], lp["w_small"], lp["fb_row"])
    n_rows = l // CMP_STRIDE
    ns = l // SLC_BLOCK
    rk = nsa_kv[:, 0:HD].reshape(n_rows, CMP_STRIDE * HD)
    rv = nsa_kv[:, HD:2 * HD].reshape(n_rows, CMP_STRIDE * HD)
    kc, vc = _compress_prompt(rk, rv, lp["cmp_pos"], lp["cmp_w"])
    oc, val = _cmp_attn(qn, kc, vc, _overlap(n_rows, ns, ns), tab_c, misc)
    selb = _topk_mask(val)
    front = jnp.zeros((WINDOW, 128), BF16)
    ya = _nsa_attn(qn, selb, ksel, jnp.concatenate([front, kwin]), jnp.concatenate([front, vcat]), tnear, twin, misc, oc)
    zeros = jnp.zeros((HALO, MIX), F32)
    yb, yc = _convpool(u, zeros, cin, zeros, lp["conv_w"], lp["conv_b"], lp["conv_ln"], lp["pool_w"], lp["pool_scale"], 0)
    logf = misc[:, 12:16]
    per_chunk = TKF // min(TM, l)
    key_reach = jnp.sqrt(jnp.max(stat[:, 1, 0:NH], axis=0)) * 1.001
    forgot = -stat[per_chunk - 1::per_chunk, 0, 3 * NH:4 * NH]
    yd = _fox_attn(jnp.concatenate([key_reach[:, None], forgot.T], axis=1), fq, fk, fv)
    x = _merge(x, lp["g"][1], lp["w_merge"], lp["w_branch"], lp["w_out"], ya, yb, yc, yd)
    x = _ffn(x, lp["g"][2], *lp["ffn"][1], fg, final)
    state = (nsa_kv, fox_kv, logf, win_kv[l - min(WINDOW, l):], u[l - CONV_HIST:], cin[l - POOL_HIST:])
    return x, state


def _sample_layer(x, lp, relt8, bt, fg, final, layer, page_table, nsa_pages, fox_pages, lf_pages, win_rows, st_win,
                  st_conv, st_pool):
    bsz = x.shape[0]
    n_pages = page_table.shape[1]
    past = n_pages * PAGE
    x = _ffn(x, lp["g"][0], *lp["ffn"][0], fg)
    (qn, nsa_kv, win_kv, misc, u, cin, fq, fox_kv, _, _, _, _, _, _) = _inproj(x, lp["g"][1], lp["w_small"], lp["fb_row"])
    pad4 = lambda a: jnp.concatenate([a, jnp.zeros((bsz, 8 - NH) + a.shape[2:], a.dtype)], axis=1)
    q8 = pad4(jnp.transpose(qn, (1, 0, 2)))
    gates = misc[:, 0:3 * NH].reshape(bsz, 3, NH)
    misc8 = jnp.zeros((bsz, 8, 128), F32).at[:, 0:NH, 0:3].set(jnp.transpose(gates, (0, 2, 1)))
    n_rows = past // CMP_STRIDE
    ns = past // SLC_BLOCK + 1
    ns_pad = -(-ns // 128) * 128
    c_end = np.arange(n_rows) * CMP_STRIDE + CMP_LEN - 1
    valid_c = (c_end <= past) & (np.arange(n_rows) < n_rows - 1)
    bias_c = jnp.concatenate([_lookup(bt, past - c_end, valid_c), jnp.full((8 - NH, n_rows), NEG, F32)], axis=0)
    oc, val = _nsa_paged_cmp(page_table, nsa_pages, layer, jnp.concatenate([q8, jnp.zeros_like(q8)], axis=2),
                             lp["cmp_pos_pair"], lp["cmp_w_pair"], _overlap(n_rows, ns, ns_pad), bias_c, misc8, past)
    idx = _topk_index(val.reshape(bsz, ns_pad), min(N_SELECT, ns))
    n_win = st_win.shape[1]
    wbias = jnp.concatenate([_lookup(bt, n_win - np.arange(n_win), np.ones(n_win, bool)),
                             jnp.full((8 - NH, n_win), NEG, F32)], axis=0)
    blk = idx.reshape(bsz, 128)[:, 0:N_SELECT]
    held = jnp.minimum(blk // 2, n_pages - 1)
    pages = jnp.take_along_axis(page_table, held, axis=1)
    ya = _nsa_paged_attn(jnp.concatenate([pages, blk], axis=1), nsa_pages, layer, q8, nsa_kv.reshape(bsz, 1, 256),
                         win_rows, win_kv.reshape(bsz, 1, 128), relt8, wbias, misc8, oc, past)
    ya = jnp.transpose(ya[:, 0:NH], (1, 0, 2))
    yb, yc = _convpool_step(u, jnp.transpose(st_conv, (1, 0, 2)), cin, jnp.transpose(st_pool, (1, 0, 2)),
                            lp["conv_w"], lp["conv_b"], lp["conv_ln"], lp["pool_w"], lp["pool_scale"], past)
    logf = misc[:, 12:16]
    q_col = jnp.transpose(fq[:, :, 0:HD], (1, 0, 2)).reshape(bsz, MIX, 1).astype(F32)
    lf_new = pad4(logf.reshape(bsz, NH, 1))
    yd = _fox_paged(page_table, fox_pages, lf_pages, layer, q_col, fox_kv.reshape(bsz, 2 * MIX, 1), lf_new)
    yd = yd.reshape(bsz, MIX).astype(BF16)
    x = _merge(x, lp["g"][1], lp["w_merge"], lp["w_branch"], lp["w_out"], ya, yb, yc, yd)
    x = _ffn(x, lp["g"][2], *lp["ffn"][1], fg, final)
    new_win =jnp.concatenate([st_win.reshape(bsz, n_win, 128)[:, 1:], win_kv[:, None, :]], axis=1)
    new_conv = jnp.concatenate([st_conv[:, 1:], u[:, None, :]], axis=1)
    new_pool = jnp.concatenate([st_pool[:, 1:], cin[:, None, :]], axis=1)
    state = (nsa_kv, fox_kv, logf, new_win, new_conv, new_pool)
    return x, state


def kernel(x_prompt, x_sample, cache_nsa, cache_fox_kv, cache_fox_logf, state_nsa_win, state_conv, state_pool,
           page_table, norm_g, ffn_gate, ffn_up, ffn_down, w_in, fox_f_bias, nsa_cmp_pos, nsa_cmp_w, rel_bias,
           conv_w, conv_b, conv_ln, pool_w, pool_scale, w_branch, w_out, final_norm_g):
    assert x_prompt.shape[0] == 1 and x_sample.shape[1] == 1
    depth = norm_g.shape[0]
    l = x_prompt.shape[1]
    bsz = x_sample.shape[0]
    win_keep = state_nsa_win.shape[2]
    assert l % SUPER == 0 and page_table.shape[1] * PAGE >= max(WINDOW, SUPER) and win_keep == WINDOW
    relt8 = jnp.concatenate([rel_bias.T, jnp.zeros((8 - NH, N_BUCKETS), F32)], axis=0)
    bt = _bias_table(relt8)[0:NH]
    tables = _prompt_tables(rel_bias)
    fg = final_norm_g.reshape(1, D_MODEL)
    xp = x_prompt.reshape(l, D_MODEL)
    xs = x_sample.reshape(bsz, D_MODEL)
    n_phys = cache_nsa.shape[1]
    nsa_pages = jnp.transpose(cache_nsa, (0, 1, 3, 4, 5, 2)).reshape(depth, n_phys, 4, HD, PAGE)
    fox_pages = jnp.transpose(cache_fox_kv, (0, 1, 3, 4, 5, 2)).reshape(depth, n_phys, 2, MIX, PAGE)
    lf_pages = jnp.pad(jnp.transpose(cache_fox_logf, (0, 1, 3, 2)), ((0, 0), (0, 0), (0, 8 - NH), (0, 0)))
    win_rows = jnp.transpose(state_nsa_win, (0, 1, 3, 4, 5, 2)).reshape(depth, bsz, 2, HD, win_keep)
    st_p, st_s = [], []
    for layer in range(depth):
        lp = _layer_params(layer, norm_g, ffn_gate, ffn_up, ffn_down, w_in, fox_f_bias, nsa_cmp_pos, nsa_cmp_w,
                           conv_w, conv_b, conv_ln, pool_w, pool_scale, w_branch, w_out)
        final = layer == depth - 1
        xp, sp = _prompt_layer(xp, lp, tables, fg, final)
        st_p.append(sp)
        xs, ss = _sample_layer(xs, lp, relt8, bt, fg, final, layer, page_table, nsa_pages, fox_pages, lf_pages, win_rows,
                               state_nsa_win[layer], state_conv[layer], state_pool[layer])
        st_s.append(ss)

    def stack(states, k, shape):
        return jnp.stack([s[k] for s in states]).reshape(shape)

    return (xp.reshape(1, l, D_MODEL), xs.reshape(bsz, 1, D_MODEL),
            stack(st_p, 0, (depth, 1, l, 4, 1, HD)), stack(st_s, 0, (depth, bsz, 1, 4, 1, HD)),
            stack(st_p, 1, (depth, 1, l, 2, NH, HD)), stack(st_s, 1, (depth, bsz, 1, 2, NH, HD)),
            stack(st_p, 2, (depth, 1, l, NH)), stack(st_s, 2, (depth, bsz, 1, NH)),
            stack(st_p, 3, (depth, 1, win_keep, 2, 1, HD)), stack(st_s, 3, (depth, bsz, win_keep, 2, 1, HD)),
            stack(st_p, 4, (depth, 1, CONV_HIST, MIX)), stack(st_s, 4, (depth, bsz, CONV_HIST, MIX)),
            stack(st_p, 5, (depth, 1, POOL_HIST, MIX)), stack(st_s, 5, (depth, bsz, POOL_HIST, MIX)))
```
